```python
import jax, jax.numpy as jnp
from jax import lax
import numpy as np

D_MODEL = 2048
BATCH = 4
SEQ = 2048
DEPTH = 1
DEC_BATCH = 32
DEC_SEQ = 4
PAST_LEN = 8192
PAGE_SIZE = 128

POOL_DIM = D_MODEL // 2
POOL_WINDOWS = (2, 4, 8, 16)
N_POOL_GROUPS = len(POOL_WINDOWS)
POOL_GROUP = POOL_DIM // N_POOL_GROUPS
POOL_HIST = max(POOL_WINDOWS) - 1
ATTN_DIM = D_MODEL - POOL_DIM
HEAD_DIM = 128
N_HEADS = ATTN_DIM // HEAD_DIM
ROT_DIM = HEAD_DIM // 4
ROPE_THETA = 500000.0
MOBA_BLOCK = 256
MOBA_TOPK = 3
Q_CHUNK = 32
D_FF = 5632
CONV_W = 3
EPS = 1e-6

kernel_name = 'hymba_pool_moba_convffn_step'


def rmsnorm(x, g):
    xf = x.astype(jnp.float32)
    y = xf * lax.rsqrt(jnp.mean(xf * xf, axis=-1, keepdims=True) + EPS)
    return (y * g.astype(jnp.float32)).astype(x.dtype)


def rope(x, pos):
    half = ROT_DIM // 2
    inv = ROPE_THETA ** (-jnp.arange(half, dtype=jnp.float32) * (2.0 / ROT_DIM))
    ang = pos.astype(jnp.float32)[:, None] * inv[None, :]
    cos = jnp.cos(ang)[:, None, :]
    sin = jnp.sin(ang)[:, None, :]
    xr = x[..., :ROT_DIM].astype(jnp.float32)
    x1, x2 = xr[..., :half], xr[..., half:]
    rot = jnp.concatenate([x1 * cos - x2 * sin, x2 * cos + x1 * sin], axis=-1)
    return jnp.concatenate([rot.astype(x.dtype), x[..., ROT_DIM:]], axis=-1)


def pool_mix(u, hist, start, w_pool, pool_scale):
    B, L, _ = u.shape
    ext = jnp.concatenate([hist.astype(u.dtype), u], axis=1).astype(jnp.float32)
    cs = jnp.concatenate([jnp.zeros((B, 1, POOL_DIM), jnp.float32),
                          jnp.cumsum(ext, axis=1)], axis=1)
    pos = start + jnp.arange(L)
    end = cs[:, POOL_HIST + 1:]
    means = []
    for g, w in enumerate(POOL_WINDOWS):
        sl = slice(g * POOL_GROUP, (g + 1) * POOL_GROUP)
        begin = cs[:, POOL_HIST + 1 - w: POOL_HIST + 1 - w + L, sl]
        cnt = jnp.minimum(pos + 1, w).astype(jnp.float32)[None, :, None]
        means.append((end[..., sl] - begin) / cnt)
    d = (jnp.concatenate(means, axis=-1) - u.astype(jnp.float32)).astype(u.dtype)
    d = d.reshape(B, L, N_POOL_GROUPS, POOL_GROUP)
    y = jnp.einsum('blgc,gcd->blgd', d, w_pool).reshape(B, L, POOL_DIM)
    return y * pool_scale


def moba_prompt(q, k, v):
    B, H, S, D = q.shape
    nb = -(-S // MOBA_BLOCK)
    pad = nb * MOBA_BLOCK - S
    kp = jnp.pad(k, ((0, 0), (0, 0), (0, pad), (0, 0)))
    vp = jnp.pad(v, ((0, 0), (0, 0), (0, pad), (0, 0)))
    kb = kp.reshape(B, H, nb, MOBA_BLOCK, D)
    vb = vp.reshape(B, H, nb, MOBA_BLOCK, D)
    kmean = jnp.mean(kb.astype(jnp.float32), axis=3)
    n_sel = min(MOBA_TOPK, nb - 1)
    n_s = n_sel * MOBA_BLOCK
    scale = HEAD_DIM ** -0.5
    bidx = jnp.arange(B)[:, None, None, None]
    hidx = jnp.arange(H)[None, :, None, None]

    def chunk(c):
        q0 = c * Q_CHUNK
        qc = lax.dynamic_slice_in_dim(q, q0, Q_CHUNK, axis=2)
        qpos = q0 + jnp.arange(Q_CHUNK)
        blk = q0 // MOBA_BLOCK
        own_k = lax.dynamic_slice_in_dim(kp, blk * MOBA_BLOCK, MOBA_BLOCK, axis=2)
        own_v = lax.dynamic_slice_in_dim(vp, blk * MOBA_BLOCK, MOBA_BLOCK, axis=2)
        kpos = blk * MOBA_BLOCK + jnp.arange(MOBA_BLOCK)
        s_own = jnp.einsum('bhqd,bhkd->bhqk', qc, own_k).astype(jnp.float32) * scale
        s_own = jnp.where(kpos[None, :] <= qpos[:, None], s_own, -jnp.inf)
        if n_sel == 0:
            p = jax.nn.softmax(s_own, axis=-1).astype(v.dtype)
            return jnp.einsum('bhqk,bhkd->bhqd', p, own_v)
        gate = jnp.einsum('bhqd,bhnd->bhqn', qc.astype(jnp.float32), kmean)
        gate = jnp.where(jnp.arange(nb) < blk, gate, -jnp.inf)
        _, sel = lax.top_k(gate, n_sel)
        ok = sel < blk
        ks = kb[bidx, hidx, sel]
        vs = vb[bidx, hidx, sel]
        s_sel = jnp.einsum('bhqd,bhqnkd->bhqnk', qc, ks).astype(jnp.float32) * scale
        s_sel = jnp.where(ok[..., None], s_sel, -jnp.inf).reshape(B, H, Q_CHUNK, n_s)
        p = jax.nn.softmax(jnp.concatenate([s_sel, s_own], axis=-1), axis=-1)
        p_sel = p[..., :n_s].reshape(B, H, Q_CHUNK, n_sel, MOBA_BLOCK).astype(v.dtype)
        out = jnp.einsum('bhqnk,bhqnkd->bhqd', p_sel, vs)
        return out + jnp.einsum('bhqk,bhkd->bhqd', p[..., n_s:].astype(v.dtype), own_v)

    outs = lax.map(chunk, jnp.arange(S // Q_CHUNK))
    return outs.transpose(1, 2, 0, 3, 4).reshape(B, H, S, D)


def moba_sample(q, k_new, v_new, cache_k, cache_v, page_table):
    Bd, H, L, D = q.shape
    n_pages = page_table.shape[1]
    ppb = MOBA_BLOCK // PAGE_SIZE
    nb_past = (n_pages * PAGE_SIZE) // MOBA_BLOCK
    n_tail = n_pages - nb_past * ppb
    n_sel = min(MOBA_TOPK, nb_past)
    scale = HEAD_DIM ** -0.5
    s_parts, v_parts = [], []
    if n_sel > 0:
        blk_pages = page_table[:, :nb_past * ppb].reshape(Bd, nb_past, ppb)
        page_sums = jnp.sum(cache_k.astype(jnp.float32), axis=2)
        kmean = jnp.sum(page_sums[blk_pages], axis=2) / MOBA_BLOCK
        gate = jnp.einsum('bhqd,bnhd->bhqn', q.astype(jnp.float32), kmean)
        _, sel = lax.top_k(gate, n_sel)
        phys = blk_pages[jnp.arange(Bd)[:, None, None, None], sel]
        hidx = jnp.arange(H)[None, :, None, None, None]
        k_sel = cache_k[phys, hidx].reshape(Bd, H, L, n_sel * MOBA_BLOCK, D)
        v_sel = cache_v[phys, hidx].reshape(Bd, H, L, n_sel * MOBA_BLOCK, D)
        s_parts.append(jnp.einsum('bhqd,bhqkd->bhqk', q, k_sel).astype(jnp.float32) * scale)
        v_parts.append((v_sel, True))
    if n_tail > 0:
        tail_phys = page_table[:, nb_past * ppb:]
        k_t = cache_k[tail_phys].transpose(0, 2, 1, 3, 4).reshape(Bd, H, n_tail * PAGE_SIZE, D)
        v_t = cache_v[tail_phys].transpose(0, 2, 1, 3, 4).reshape(Bd, H, n_tail * PAGE_SIZE, D)
        s_parts.append(jnp.einsum('bhqd,bhkd->bhqk', q, k_t).astype(jnp.float32) * scale)
        v_parts.append((v_t, False))
    causal = jnp.tril(jnp.ones((L, L), dtype=bool))
    s_new = jnp.einsum('bhqd,bhkd->bhqk', q, k_new).astype(jnp.float32) * scale
    s_parts.append(jnp.where(causal, s_new, -jnp.inf))
    v_parts.append((v_new, False))
    p = jax.nn.softmax(jnp.concatenate(s_parts, axis=-1), axis=-1)
    out = None
    off = 0
    for sc, (vals, per_query) in zip(s_parts, v_parts):
        n = sc.shape[-1]
        pp = p[..., off:off + n].astype(vals.dtype)
        off += n
        term = (jnp.einsum('bhqk,bhqkd->bhqd', pp, vals) if per_query
                else jnp.einsum('bhqk,bhkd->bhqd', pp, vals))
        out = term if out is None else out + term
    return out


def conv_ffn(h, conv_hist, w_gate, w_up, conv_w, conv_b, w_down):
    L = h.shape[1]
    gt = h @ w_gate
    ext = jnp.concatenate([conv_hist.astype(gt.dtype), gt], axis=1)
    c = conv_b
    for j in range(CONV_W):
        c = c + ext[:, j:j + L] * conv_w[j]
    f = jax.nn.gelu(c, approximate=True) * (h @ w_up)
    return f @ w_down, ext[:, L:]


def decoder_layer(x, pool_hist, conv_hist, start, attend, lp):
    B, L, _ = x.shape
    h = rmsnorm(x, lp['g_mix_pre'])
    proj = h @ lp['w_in']
    u = proj[..., :POOL_DIM]
    qkv = proj[..., POOL_DIM:].reshape(B, L, 3, N_HEADS, HEAD_DIM)
    pos = start + jnp.arange(L)
    q = rope(qkv[:, :, 0], pos).transpose(0, 2, 1, 3)
    k = rope(qkv[:, :, 1], pos).transpose(0, 2, 1, 3)
    v = qkv[:, :, 2].transpose(0, 2, 1, 3)
    pooled = pool_mix(u, pool_hist, start, lp['w_pool'], lp['pool_scale'])
    attn = attend(q, k, v).transpose(0, 2, 1, 3).reshape(B, L, ATTN_DIM)
    mix = jnp.concatenate([pooled.astype(x.dtype), attn.astype(x.dtype)], axis=-1) @ lp['w_out']
    x = x + rmsnorm(mix, lp['g_mix_post'])
    f, new_conv = conv_ffn(rmsnorm(x, lp['g_ffn_pre']), conv_hist, lp['w_gate'], lp['w_up'],
                           lp['conv_w'], lp['conv_b'], lp['w_down'])
    x = x + rmsnorm(f, lp['g_ffn_post'])
    new_pool = jnp.concatenate([pool_hist.astype(u.dtype), u], axis=1)[:, L:]
    return x, k, v, new_pool, new_conv


def setup_inputs(seed: int = 0) -> dict:
    key = jax.random.key(seed)
    ks = jax.random.split(key, 24)
    n_pages = PAST_LEN // PAGE_SIZE
    n_used = DEC_BATCH * n_pages
    n_phys = n_used + n_used // 4

    def nrm(k, shape, s=1.0):
        return s * jax.random.normal(k, shape, jnp.float32)

    perm = jax.random.permutation(ks[0], n_phys)
    page_table = perm[:n_used].reshape(DEC_BATCH, n_pages).astype(jnp.int32)
    mix_w = POOL_DIM + ATTN_DIM
    return {
        'x_prompt': nrm(ks[1], (BATCH, SEQ, D_MODEL)),
        'x_sample': nrm(ks[2], (DEC_BATCH, DEC_SEQ, D_MODEL)),
        'cache_k': nrm(ks[3], (DEPTH, n_phys, N_HEADS, PAGE_SIZE, HEAD_DIM)),
        'cache_v': nrm(ks[4], (DEPTH, n_phys, N_HEADS, PAGE_SIZE, HEAD_DIM)),
        'state_pool': nrm(ks[5], (DEPTH, DEC_BATCH, POOL_HIST, POOL_DIM)),
        'state_conv': nrm(ks[6], (DEPTH, DEC_BATCH, CONV_W - 1, D_FF)),
        'page_table': page_table,
        'w_in': nrm(ks[7], (DEPTH, D_MODEL, POOL_DIM + 3 * ATTN_DIM), D_MODEL ** -0.5),
        'w_pool': nrm(ks[8], (DEPTH, N_POOL_GROUPS, POOL_GROUP, POOL_GROUP), POOL_GROUP ** -0.5),
        'pool_scale': 1.0 + nrm(ks[9], (DEPTH, POOL_DIM), 0.1),
        'w_out': nrm(ks[10], (DEPTH, mix_w, D_MODEL), mix_w ** -0.5),
        'g_mix_pre': 1.0 + nrm(ks[11], (DEPTH, D_MODEL), 0.05),
        'g_mix_post': 1.0 + nrm(ks[12], (DEPTH, D_MODEL), 0.05),
        'w_gate': nrm(ks[13], (DEPTH, D_MODEL, D_FF), D_MODEL ** -0.5),
        'w_up': nrm(ks[14], (DEPTH, D_MODEL, D_FF), D_MODEL ** -0.5),
        'conv_w': nrm(ks[15], (DEPTH, CONV_W, D_FF), CONV_W ** -0.5),
        'conv_b': nrm(ks[16], (DEPTH, D_FF), 0.02),
        'w_down': nrm(ks[17], (DEPTH, D_FF, D_MODEL), D_FF ** -0.5),
        'g_ffn_pre': 1.0 + nrm(ks[18], (DEPTH, D_MODEL), 0.05),
        'g_ffn_post': 1.0 + nrm(ks[19], (DEPTH, D_MODEL), 0.05),
    }


def reference(x_prompt, x_sample, cache_k, cache_v, state_pool, state_conv, page_table,
              w_in, w_pool, pool_scale, w_out, g_mix_pre, g_mix_post,
              w_gate, w_up, conv_w, conv_b, w_down, g_ffn_pre, g_ffn_post):
    past_len = page_table.shape[1] * PAGE_SIZE
    B = x_prompt.shape[0]
    yp, ys = x_prompt, x_sample
    kp_l, vp_l, ks_l, vs_l, pp_l, ps_l, cp_l, cs_l = [], [], [], [], [], [], [], []
    for l in range(DEPTH):
        lp = {'w_in': w_in[l], 'w_pool': w_pool[l], 'pool_scale': pool_scale[l],
              'w_out': w_out[l], 'g_mix_pre': g_mix_pre[l], 'g_mix_post': g_mix_post[l],
              'w_gate': w_gate[l], 'w_up': w_up[l], 'conv_w': conv_w[l], 'conv_b': conv_b[l],
              'w_down': w_down[l], 'g_ffn_pre': g_ffn_pre[l], 'g_ffn_post': g_ffn_post[l]}
        zero_pool = jnp.zeros((B, POOL_HIST, POOL_DIM), x_prompt.dtype)
        zero_conv = jnp.zeros((B, CONV_W - 1, D_FF), x_prompt.dtype)
        yp, kp, vp, pp, cp = decoder_layer(yp, zero_pool, zero_conv, 0, moba_prompt, lp)
        ck, cv = cache_k[l], cache_v[l]
        attend_s = lambda q, k, v: moba_sample(q, k, v, ck, cv, page_table)
        ys, ksn, vsn, ps, cs = decoder_layer(ys, state_pool[l], state_conv[l], past_len, attend_s, lp)
        kp_l.append(kp); vp_l.append(vp); ks_l.append(ksn); vs_l.append(vsn)
        pp_l.append(pp); ps_l.append(ps); cp_l.append(cp); cs_l.append(cs)
    return (yp, ys, jnp.stack(kp_l), jnp.stack(vp_l), jnp.stack(ks_l), jnp.stack(vs_l),
            jnp.stack(pp_l), jnp.stack(ps_l), jnp.stack(cp_l), jnp.stack(cs_l))
```

```python
import functools

import jax
import jax.numpy as jnp
from jax import lax
from jax.experimental import pallas as pl
from jax.experimental.pallas import tpu as pltpu

F32 = jnp.float32
BF16 = jnp.bfloat16

POOL_WINDOWS = (2, 4, 8, 16)
POOL_HIST = max(POOL_WINDOWS) - 1
HEAD_DIM = 128
ROT_DIM = HEAD_DIM // 4
ROPE_THETA = 500000.0
MOBA_BLOCK = 256
MOBA_TOPK = 3
PAGE_SIZE = 128
CONV_W = 3
EPS = 1e-6
NEG_INF = float("-inf")

V7X_SUBLANES = 8
V7X_VMEM_BYTES = 64 * 1024 * 1024
VMEM_RESERVE_BYTES = 8 * 1024 * 1024


def _vmem_limit(block_bytes):
    return int(min(2 * block_bytes, V7X_VMEM_BYTES - VMEM_RESERVE_BYTES))


def _params(semantics, block_bytes):
    return pltpu.CompilerParams(dimension_semantics=semantics,
                                vmem_limit_bytes=_vmem_limit(block_bytes))


def _rms(x, g):
    return x * lax.rsqrt(jnp.mean(x * x, axis=-1, keepdims=True) + EPS) * g


def _dot(a, b):
    return jnp.dot(a, b, preferred_element_type=F32)


def _dot_nt(a, b, precision=None):
    return lax.dot_general(a, b, (((1,), (1,)), ((), ())), precision=precision,
                           preferred_element_type=F32)


def _rope_tables(pos):
    half = ROT_DIM // 2
    inv = ROPE_THETA ** (-jnp.arange(half, dtype=F32) * (2.0 / ROT_DIM))
    ang = pos.astype(F32)[:, None] * inv[None, :]
    cos, sin = jnp.cos(ang), jnp.sin(ang)
    n = pos.shape[0]
    c = jnp.concatenate([cos, cos, jnp.ones((n, HEAD_DIM - ROT_DIM), F32)], axis=1)
    s_lo = jnp.concatenate([-sin, jnp.zeros((n, HEAD_DIM - half), F32)], axis=1)
    s_hi = jnp.concatenate([jnp.zeros((n, half), F32), sin,
                            jnp.zeros((n, HEAD_DIM - ROT_DIM), F32)], axis=1)
    return c, s_lo, s_hi


def _inproj_kernel(x_ref, g_ref, w_ref, c_ref, slo_ref, shi_ref,
                   u_ref, q_ref, k_ref, v_ref, hn_ref, *, n_heads):
    j = pl.program_id(2)

    @pl.when(j == 0)
    def _():
        hn_ref[...] = _rms(x_ref[0], g_ref[...]).astype(BF16)

    res = _dot(hn_ref[...], w_ref[...])

    def head(hh):
        return res[:, hh * HEAD_DIM:(hh + 1) * HEAD_DIM]

    def rope(h):
        return (h * c_ref[...]
                + pltpu.roll(h, HEAD_DIM - ROT_DIM // 2, 1) * slo_ref[...]
                + pltpu.roll(h, ROT_DIM // 2, 1) * shi_ref[...])

    @pl.when(j == 0)
    def _():
        u_ref[0] = res

    @pl.when(j == 1)
    def _():
        for hh in range(n_heads):
            q_ref[0, hh] = rope(head(hh))

    @pl.when(j == 2)
    def _():
        for hh in range(n_heads):
            k_ref[0, hh] = rope(head(hh))

    @pl.when(j == 3)
    def _():
        for hh in range(n_heads):
            v_ref[0, hh] = head(hh)


def _inproj(x, g, w_bf, pos, tm):
    B, S, D = x.shape
    width = w_bf.shape[1] // 4
    n_heads = width // HEAD_DIM
    c, s_lo, s_hi = _rope_tables(pos)
    tab_spec = pl.BlockSpec((tm, HEAD_DIM), lambda b, s, j: (s, 0))
    hm_spec = pl.BlockSpec((1, n_heads, tm, HEAD_DIM), lambda b, s, j: (b, 0, s, 0))
    hm_shape = jax.ShapeDtypeStruct((B, n_heads, S, HEAD_DIM), F32)
    block_bytes = (2 * tm * D * 4 + tm * D * 2 + 2 * D * width * 2 + 2 * 4 * tm * width * 4
                   + tm * width * 4)
    return pl.pallas_call(
        functools.partial(_inproj_kernel, n_heads=n_heads),
        grid=(B, S // tm, 4),
        in_specs=[
            pl.BlockSpec((1, tm, D), lambda b, s, j: (b, s, 0)),
            pl.BlockSpec((1, D), lambda b, s, j: (0, 0)),
            pl.BlockSpec((D, width), lambda b, s, j: (0, j)),
            tab_spec, tab_spec, tab_spec,
        ],
        out_specs=[pl.BlockSpec((1, tm, width), lambda b, s, j: (b, s, 0)), hm_spec, hm_spec, hm_spec],
        out_shape=[jax.ShapeDtypeStruct((B, S, width), F32), hm_shape, hm_shape, hm_shape],
        scratch_shapes=[pltpu.VMEM((tm, D), BF16)],
        compiler_params=_params(("arbitrary", "arbitrary", "arbitrary"), block_bytes),
        name="inproj",
    )(x, g.reshape(1, D), w_bf, c, s_lo, s_hi)


def _pool_kernel(u_ref, halo_ref, w_ref, scale_ref, o_ref, ext_ref, *, ts, group):
    s = pl.program_id(1)
    hist = POOL_HIST + 1
    u = u_ref[0]
    ext_ref[hist:, :] = u

    @pl.when(s == 0)
    def _():
        ext_ref[0:hist, :] = jnp.zeros((hist, ext_ref.shape[1]), F32)

    @pl.when(s > 0)
    def _():
        ext_ref[0:hist, :] = halo_ref[0]

    pos = s * ts + lax.broadcasted_iota(jnp.int32, (ts, 1), 0)
    for gi, w in enumerate(POOL_WINDOWS):
        cols = slice(gi * group, (gi + 1) * group)
        acc = u[:, cols]
        for back in range(1, w):
            acc = acc + ext_ref[hist - back:hist - back + ts, cols]
        cnt = jnp.minimum(pos + 1, w).astype(F32)
        d = (acc / cnt - u[:, cols]).astype(BF16)
        y = _dot(d, w_ref[gi]) * scale_ref[:, cols]
        o_ref[0, :, cols] = y.astype(BF16)


def _pool_prompt(u, w_pool_bf, scale, ts):
    B, S, C = u.shape
    n_groups, group, _ = w_pool_bf.shape
    hist = POOL_HIST + 1
    per = ts // hist
    block_bytes = 2 * ts * C * 4 + 2 * hist * C * 4 + 2 * n_groups * group * group * 2 \
        + 2 * ts * C * 2 + (ts + hist) * C * 4 + 4 * ts * group * 4
    return pl.pallas_call(
        functools.partial(_pool_kernel, ts=ts, group=group),
        grid=(B, S // ts),
        in_specs=[
            pl.BlockSpec((1, ts, C), lambda b, s: (b, s, 0)),
            pl.BlockSpec((1, hist, C), lambda b, s: (b, jnp.maximum(s * per - 1, 0), 0)),
            pl.BlockSpec((n_groups, group, group), lambda b, s: (0, 0, 0)),
            pl.BlockSpec((1, C), lambda b, s: (0, 0)),
        ],
        out_specs=pl.BlockSpec((1, ts, C), lambda b, s: (b, s, 0)),
        out_shape=jax.ShapeDtypeStruct((B, S, C), BF16),
        scratch_shapes=[pltpu.VMEM((ts + hist, C), F32)],
        compiler_params=_params(("arbitrary", "arbitrary"), block_bytes),
        name="pool_prompt",
    )(u, u, w_pool_bf, scale.reshape(1, C))


def _pool_sample_kernel(u_ref, hist_ref, w_ref, scale_ref, o_ref, *, n_steps, n_seq, group, past_len):
    def ext(i):
        if i < POOL_HIST:
            return hist_ref[i * n_seq:(i + 1) * n_seq, :]
        return u_ref[(i - POOL_HIST) * n_seq:(i - POOL_HIST + 1) * n_seq, :]

    for gi, w in enumerate(POOL_WINDOWS):
        cols = slice(gi * group, (gi + 1) * group)
        ds = []
        for t in range(n_steps):
            cur = ext(POOL_HIST + t)[:, cols]
            acc = cur
            for back in range(1, w):
                acc = acc + ext(POOL_HIST + t - back)[:, cols]
            cnt = float(min(past_len + t + 1, w))
            ds.append((acc / cnt - cur).astype(BF16))
        d = jnp.concatenate(ds, axis=0)
        y = _dot(d, w_ref[gi]) * scale_ref[:, cols]
        o_ref[:, cols] = y.astype(BF16)


def _pool_sample(u_tm, hist_tm, w_pool_bf, scale, n_steps, n_seq, past_len):
    rows, C = u_tm.shape
    n_groups, group, _ = w_pool_bf.shape
    block_bytes = 2 * (rows + hist_tm.shape[0]) * C * 4 + 2 * n_groups * group * group * 2 + 2 * rows * C * 2
    return pl.pallas_call(
        functools.partial(_pool_sample_kernel, n_steps=n_steps, n_seq=n_seq, group=group,
                          past_len=past_len),
        out_shape=jax.ShapeDtypeStruct((rows, C), BF16),
        compiler_params=pltpu.CompilerParams(vmem_limit_bytes=_vmem_limit(block_bytes)),
        name="pool_sample",
    )(u_tm, hist_tm, w_pool_bf, scale.reshape(1, C))


def _moba_prompt_kernel(q_ref, k_ref, v_ref, o_ref, kb_ref, vt_ref, km_ref, s_ref, p_ref, *, seq):
    blk = MOBA_BLOCK
    nb = seq // blk
    n_sel = min(MOBA_TOPK, nb - 1)
    scale = HEAD_DIM ** -0.5

    for n in range(nb):
        rows = slice(n * blk, (n + 1) * blk)
        kn = k_ref[0, 0, rows, :]
        kb_ref[rows, :] = kn.astype(BF16)
        km_ref[n:n + 1, :] = jnp.sum(kn, axis=0, keepdims=True) * (1.0 / blk)
        vt_ref[:, rows] = v_ref[0, 0, rows, :].T.astype(BF16)

    kmean = km_ref[...]
    blk_id = lax.broadcasted_iota(jnp.int32, (nb, blk), 0)
    key_i = lax.broadcasted_iota(jnp.int32, (blk, blk), 0)
    qry_i = lax.broadcasted_iota(jnp.int32, (blk, blk), 1)
    causal_bias = jnp.where(key_i <= qry_i, 0.0, NEG_INF).astype(F32)

    for i in range(nb):
        qi = q_ref[0, 0, i * blk:(i + 1) * blk, :]
        nk = (i + 1) * blk
        s_ref[0:nk, :] = _dot_nt(kb_ref[0:nk, :], qi.astype(BF16))

        biases = []
        if i > 0:
            gate = _dot_nt(kmean, qi, precision=lax.Precision.HIGHEST)
            valid = jnp.where(blk_id < i, 1.0, 0.0)
            for n in range(i):
                gn = gate[n:n + 1, :]
                beats = jnp.where(blk_id < n, jnp.where(gate >= gn, 1.0, 0.0), jnp.where(gate > gn, 1.0, 0.0))
                rank = jnp.sum(beats * valid, axis=0, keepdims=True)
                biases.append(jnp.where(rank < n_sel, 0.0, NEG_INF).astype(F32))
        biases.append(causal_bias)

        m = jnp.full((1, blk), NEG_INF, F32)
        for n in range(i + 1):
            rows = slice(n * blk, (n + 1) * blk)
            sb = s_ref[rows, :] * scale + biases[n]
            s_ref[rows, :] = sb
            m = jnp.maximum(m, jnp.max(sb, axis=0, keepdims=True))

        l = jnp.zeros((1, blk), F32)
        for n in range(i + 1):
            rows = slice(n * blk, (n + 1) * blk)
            p = jnp.exp(s_ref[rows, :] - m)
            l = l + jnp.sum(p, axis=0, keepdims=True)
            p_ref[rows, :] = p.astype(BF16)

        o_t = _dot(vt_ref[:, 0:nk], p_ref[0:nk, :]) / l
        o_ref[0, i * blk:(i + 1) * blk, :] = o_t.T.astype(BF16)


def _moba_prompt(q, k, v):
    B, H, S, Dh = q.shape
    nb = S // MOBA_BLOCK
    in_spec = pl.BlockSpec((1, 1, S, Dh), lambda b, h: (b, h, 0, 0))
    block_bytes = 2 * 3 * S * Dh * 4 + 2 * S * Dh * 2 + 2 * S * Dh * 2 + S * MOBA_BLOCK * 6
    return pl.pallas_call(
        functools.partial(_moba_prompt_kernel, seq=S),
        grid=(B, H),
        in_specs=[in_spec, in_spec, in_spec],
        out_specs=pl.BlockSpec((1, S, Dh), lambda b, h: (b, 0, h)),
        out_shape=jax.ShapeDtypeStruct((B, S, H * Dh), BF16),
        scratch_shapes=[
            pltpu.VMEM((S, Dh), BF16),
            pltpu.VMEM((Dh, S), BF16),
            pltpu.VMEM((nb, Dh), F32),
            pltpu.VMEM((S, MOBA_BLOCK), F32),
            pltpu.VMEM((S, MOBA_BLOCK), BF16),
        ],
        compiler_params=_params(("arbitrary", "arbitrary"), block_bytes),
        name="moba_prompt",
    )(q, k, v)


def _moba_sample_kernel(pt_ref, q_ref, kn_ref, vn_ref, ck_ref, cv_ref, o_ref,
                        kbuf, vbuf, km_ref, q8_ref, ksem, vsem, *, n_pages, n_heads, n_new):
    ppb = MOBA_BLOCK // PAGE_SIZE
    nblk = n_pages // ppb
    n_sel = min(MOBA_TOPK, nblk)
    scale = HEAD_DIM ** -0.5
    b = pl.program_id(0)
    h = pl.program_id(1)
    step = b * n_heads + h
    n_steps = pl.num_programs(0) * n_heads
    slot = lax.rem(step, 2)

    def k_copy(bb, hh, page, sl):
        return pltpu.make_async_copy(ck_ref.at[pt_ref[bb, page], hh],
                                     kbuf.at[sl, pl.ds(page * PAGE_SIZE, PAGE_SIZE), :], ksem.at[sl])

    def start_k(bb, hh, sl):
        def body(page, carry):
            k_copy(bb, hh, page, sl).start()
            return carry
        lax.fori_loop(0, n_pages, body, 0)

    @pl.when(step == 0)
    def _():
        start_k(b, h, slot)

    @pl.when(step + 1 < n_steps)
    def _():
        nxt = step + 1
        start_k(nxt // n_heads, lax.rem(nxt, n_heads), 1 - slot)

    def wait_body(page, carry):
        k_copy(b, h, page, slot).wait()
        return carry
    lax.fori_loop(0, n_pages, wait_body, 0)

    for n in range(nblk):
        kn_blk = kbuf[slot, n * MOBA_BLOCK:(n + 1) * MOBA_BLOCK, :]
        km_ref[n:n + 1, :] = jnp.sum(kn_blk, axis=0, keepdims=True) * (1.0 / MOBA_BLOCK)
    q = q_ref[0, 0]
    q8_ref[...] = jnp.zeros(q8_ref.shape, F32)
    q8_ref[0:n_new, :] = q
    gate = _dot_nt(q8_ref[...], km_ref[...], precision=lax.Precision.HIGHEST)
    lane = lax.broadcasted_iota(jnp.int32, gate.shape, 1)
    picks = [jnp.zeros((gate.shape[0], 1), F32) for _ in range(n_sel)]
    for n in range(nblk):
        gn = gate[:, n:n + 1]
        beats = jnp.where(lane < n, jnp.where(gate >= gn, 1.0, 0.0), jnp.where(gate > gn, 1.0, 0.0))
        rank = jnp.sum(beats, axis=1, keepdims=True)
        for c in range(n_sel):
            picks[c] = jnp.where(rank == float(c), float(n), picks[c])

    sel = [[picks[c][j, 0].astype(jnp.int32) for c in range(n_sel)] for j in range(n_new)]

    def v_copy(j, c, pg):
        page = pt_ref[b, sel[j][c] * ppb + pg]
        dst = ((j * n_sel + c) * ppb + pg) * PAGE_SIZE
        return pltpu.make_async_copy(cv_ref.at[page, h], vbuf.at[pl.ds(dst, PAGE_SIZE), :], vsem.at[0])

    for j in range(n_new):
        for c in range(n_sel):
            for pg in range(ppb):
                v_copy(j, c, pg).start()

    kn = kn_ref[0, 0]
    vn = vn_ref[0, 0]
    new_row = lax.broadcasted_iota(jnp.int32, (n_new, 1), 0)
    scores = []
    for j in range(n_new):
        qj = q[j:j + 1, :]
        s_sel = []
        for c in range(n_sel):
            start = pl.multiple_of(sel[j][c] * MOBA_BLOCK, MOBA_BLOCK)
            kblk = kbuf[slot, pl.ds(start, MOBA_BLOCK), :]
            s_sel.append(jnp.sum(kblk * qj, axis=1, keepdims=True) * scale)
        s_new = jnp.sum(kn * qj, axis=1, keepdims=True) * scale
        s_new = jnp.where(new_row <= j, s_new, NEG_INF)
        m = jnp.max(s_new, axis=0, keepdims=True)
        for s in s_sel:
            m = jnp.maximum(m, jnp.max(s, axis=0, keepdims=True))
        p_sel = [jnp.exp(s - m) for s in s_sel]
        p_new = jnp.exp(s_new - m)
        l = jnp.sum(p_new, axis=0, keepdims=True)
        for p in p_sel:
            l = l + jnp.sum(p, axis=0, keepdims=True)
        scores.append((p_sel, p_new, l))

    for j in range(n_new):
        for c in range(n_sel):
            for pg in range(ppb):
                v_copy(j, c, pg).wait()

    for j in range(n_new):
        p_sel, p_new, l = scores[j]
        acc = jnp.sum(p_new * vn, axis=0, keepdims=True)
        for c in range(n_sel):
            vblk = vbuf[(j * n_sel + c) * MOBA_BLOCK:(j * n_sel + c + 1) * MOBA_BLOCK, :]
            acc = acc + jnp.sum(p_sel[c] * vblk, axis=0, keepdims=True)
        o_ref[0, 0, j:j + 1, :] = acc / l


def _moba_sample(q, k_new, v_new, cache_k, cache_v, page_table):
    Bd, H, L, Dh = q.shape
    n_pages = page_table.shape[1]
    ppb = MOBA_BLOCK // PAGE_SIZE
    assert n_pages % ppb == 0, "past length must be a whole number of MoBA blocks"
    nblk = n_pages // ppb
    n_sel = min(MOBA_TOPK, nblk)
    assert n_sel > 0 and L <= V7X_SUBLANES
    new_spec = pl.BlockSpec((1, 1, L, Dh), lambda b, h, pt: (b, h, 0, 0))
    any_spec = pl.BlockSpec(memory_space=pl.ANY)
    kbuf_bytes = 2 * n_pages * PAGE_SIZE * Dh * 4
    vbuf_rows = L * n_sel * MOBA_BLOCK
    grid_spec = pltpu.PrefetchScalarGridSpec(
        num_scalar_prefetch=1,
        grid=(Bd, H),
        in_specs=[new_spec, new_spec, new_spec, any_spec, any_spec],
        out_specs=new_spec,
        scratch_shapes=[
            pltpu.VMEM((2, n_pages * PAGE_SIZE, Dh), F32),
            pltpu.VMEM((vbuf_rows, Dh), F32),
            pltpu.VMEM((nblk, Dh), F32),
            pltpu.VMEM((V7X_SUBLANES, Dh), F32),
            pltpu.SemaphoreType.DMA((2,)),
            pltpu.SemaphoreType.DMA((1,)),
        ],
    )
    return pl.pallas_call(
        functools.partial(_moba_sample_kernel, n_pages=n_pages, n_heads=H, n_new=L),
        grid_spec=grid_spec,
        out_shape=jax.ShapeDtypeStruct((Bd, H, L, Dh), F32),
        compiler_params=_params(("arbitrary", "arbitrary"), kbuf_bytes + vbuf_rows * Dh * 4),
        name="moba_sample",
    )(page_table, q, k_new, v_new, cache_k, cache_v)


def _outproj_kernel(pooled_ref, attn_ref, x_ref, w_ref, gpost_ref, gffn_ref, x1_ref, h2_ref, *, split):
    mix = _dot(pooled_ref[...], w_ref[0:split, :]) + _dot(attn_ref[...], w_ref[split:, :])
    x1 = x_ref[...] + _rms(mix, gpost_ref[...])
    x1_ref[...] = x1
    h2_ref[...] = _rms(x1, gffn_ref[...]).astype(BF16)


def _outproj(pooled, attn, x, w_bf, g_post, g_ffn, tm):
    rows, D = x.shape
    split = pooled.shape[1]
    wa = attn.shape[1]
    row_spec = lambda width: pl.BlockSpec((tm, width), lambda i: (i, 0))
    vec_spec = pl.BlockSpec((1, D), lambda i: (0, 0))
    block_bytes = 2 * tm * (split + wa) * 2 + 2 * tm * D * 4 + 2 * (split + wa) * D * 2 \
        + 2 * tm * D * 6 + 2 * tm * D * 4
    return pl.pallas_call(
        functools.partial(_outproj_kernel, split=split),
        grid=(rows // tm,),
        in_specs=[row_spec(split), row_spec(wa), row_spec(D),
                  pl.BlockSpec((split + wa, D), lambda i: (0, 0)), vec_spec, vec_spec],
        out_specs=[row_spec(D), row_spec(D)],
        out_shape=[jax.ShapeDtypeStruct((rows, D), F32), jax.ShapeDtypeStruct((rows, D), BF16)],
        compiler_params=_params(("arbitrary",), block_bytes),
        name="outproj",
    )(pooled, attn, x, w_bf, g_post.reshape(1, D), g_ffn.reshape(1, D))


def _gelu_tanh(c):
    return c * (0.5 * (1.0 + jnp.tanh(0.7978845608028654 * (c + 0.044715 * (c * c * c)))))


def _ffn_up_kernel(h_ref, wg_ref, wu_ref, cw_ref, cb_ref, f_ref, tail_ref, ext_ref, *, tm):
    s = pl.program_id(2)
    pad = V7X_SUBLANES
    h = h_ref[0]
    gt = _dot(h, wg_ref[...])
    up = _dot(h, wu_ref[...])

    @pl.when(s == 0)
    def _():
        ext_ref[0:pad, :] = jnp.zeros((pad, ext_ref.shape[1]), F32)

    @pl.when(s > 0)
    def _():
        ext_ref[0:pad, :] = ext_ref[tm:tm + pad, :]

    ext_ref[pad:pad + tm, :] = gt
    g1 = ext_ref[pad - 1:pad - 1 + tm, :]
    g2 = ext_ref[pad - 2:pad - 2 + tm, :]
    c = cb_ref[...] + g2 * cw_ref[0:1, :] + g1 * cw_ref[1:2, :] + gt * cw_ref[2:3, :]
    f_ref[0] = (_gelu_tanh(c) * up).astype(BF16)
    tail_ref[0] = ext_ref[tm:tm + pad, :]


def _ffn_up_prompt(h2, wg_bf, wu_bf, conv_w, conv_b, tm, tn):
    B, S, D = h2.shape
    F = wg_bf.shape[1]
    pad = V7X_SUBLANES
    w_spec = pl.BlockSpec((D, tn), lambda j, b, s: (0, j))
    block_bytes = 2 * tm * D * 2 + 2 * 2 * D * tn * 2 + 2 * tm * tn * 2 + (tm + pad) * tn * 4 + 4 * tm * tn * 4
    return pl.pallas_call(
        functools.partial(_ffn_up_kernel, tm=tm),
        grid=(F // tn, B, S // tm),
        in_specs=[
            pl.BlockSpec((1, tm, D), lambda j, b, s: (b, s, 0)),
            w_spec, w_spec,
            pl.BlockSpec((CONV_W, tn), lambda j, b, s: (0, j)),
            pl.BlockSpec((1, tn), lambda j, b, s: (0, j)),
        ],
        out_specs=[pl.BlockSpec((1, tm, tn), lambda j, b, s: (b, s, j)),
                   pl.BlockSpec((1, pad, tn), lambda j, b, s: (b, 0, j))],
        out_shape=[jax.ShapeDtypeStruct((B, S, F), BF16), jax.ShapeDtypeStruct((B, pad, F), F32)],
        scratch_shapes=[pltpu.VMEM((tm + pad, tn), F32)],
        compiler_params=_params(("arbitrary", "arbitrary", "arbitrary"), block_bytes),
        name="ffn_up_prompt",
    )(h2, wg_bf, wu_bf, conv_w, conv_b.reshape(1, F))


def _ffn_up_sample_kernel(h_ref, wg_ref, wu_ref, cw_ref, cb_ref, hist_ref, f_ref, gt_ref, *, n_steps, n_seq):
    h = h_ref[...]
    gt = _dot(h, wg_ref[...])
    up = _dot(h, wu_ref[...])
    gt_ref[...] = gt
    n_hist = CONV_W - 1
    ext = [hist_ref[i * n_seq:(i + 1) * n_seq, :] for i in range(n_hist)]
    ext += [gt[t * n_seq:(t + 1) * n_seq, :] for t in range(n_steps)]
    for t in range(n_steps):
        c = cb_ref[...]
        for i in range(CONV_W):
            c = c + ext[t + i] * cw_ref[i:i + 1, :]
        rows = slice(t * n_seq, (t + 1) * n_seq)
        f_ref[rows, :] = (_gelu_tanh(c) * up[rows, :]).astype(BF16)


def _ffn_up_sample(h2, wg_bf, wu_bf, conv_w, conv_b, hist_tm, n_steps, n_seq, tn):
    rows, D = h2.shape
    F = wg_bf.shape[1]
    col_spec = lambda r: pl.BlockSpec((r, tn), lambda j: (0, j))
    block_bytes = 2 * rows * D * 2 + 2 * 2 * D * tn * 2 + 2 * rows * tn * 6 + 2 * hist_tm.shape[0] * tn * 4 \
        + 4 * rows * tn * 4
    return pl.pallas_call(
        functools.partial(_ffn_up_sample_kernel, n_steps=n_steps, n_seq=n_seq),
        grid=(F // tn,),
        in_specs=[pl.BlockSpec((rows, D), lambda j: (0, 0)), col_spec(D), col_spec(D),
                  col_spec(CONV_W), col_spec(1), col_spec(hist_tm.shape[0])],
        out_specs=[col_spec(rows), col_spec(rows)],
        out_shape=[jax.ShapeDtypeStruct((rows, F), BF16), jax.ShapeDtypeStruct((rows, F), F32)],
        compiler_params=_params(("arbitrary",), block_bytes),
        name="ffn_up_sample",
    )(h2, wg_bf, wu_bf, conv_w, conv_b.reshape(1, F), hist_tm)


def _ffn_down_kernel(f_ref, w_ref, x1_ref, g_ref, y_ref):
    y_ref[...] = x1_ref[...] + _rms(_dot(f_ref[...], w_ref[...]), g_ref[...])


def _ffn_down(f, w_bf, x1, g, tm):
    rows, F = f.shape
    D = w_bf.shape[1]
    block_bytes = 2 * tm * F * 2 + F * D * 2 + 4 * tm * D * 4 + 2 * tm * D * 4
    return pl.pallas_call(
        _ffn_down_kernel,
        grid=(rows // tm,),
        in_specs=[pl.BlockSpec((tm, F), lambda i: (i, 0)),
                  pl.BlockSpec((F, D), lambda i: (0, 0), pipeline_mode=pl.Buffered(1)),
                  pl.BlockSpec((tm, D), lambda i: (i, 0)),
                  pl.BlockSpec((1, D), lambda i: (0, 0))],
        out_specs=pl.BlockSpec((tm, D), lambda i: (i, 0)),
        out_shape=jax.ShapeDtypeStruct((rows, D), F32),
        compiler_params=pltpu.CompilerParams(dimension_semantics=("arbitrary",),
                                             vmem_limit_bytes=int(min(block_bytes + VMEM_RESERVE_BYTES,
                                                                      V7X_VMEM_BYTES - VMEM_RESERVE_BYTES))),
        name="ffn_down",
    )(f, w_bf, x1, g.reshape(1, D))


def _pick_tile(n, target):
    t = min(n, target)
    while n % t:
        t //= 2
    return t


def _prompt_layer(x, lp):
    B, S, D = x.shape
    F = lp["w_gate"].shape[1]
    tm = _pick_tile(S, 512)
    u, q, k, v = _inproj(x, lp["g_mix_pre"], lp["w_in"], jnp.arange(S), tm)
    pooled = _pool_prompt(u, lp["w_pool"], lp["pool_scale"], tm)
    attn = _moba_prompt(q, k, v)
    x1, h2 = _outproj(pooled.reshape(B * S, -1), attn.reshape(B * S, -1), x.reshape(B * S, D),
                      lp["w_out"], lp["g_mix_post"], lp["g_ffn_pre"], tm)
    f, tail = _ffn_up_prompt(h2.reshape(B, S, D), lp["w_gate"], lp["w_up"], lp["conv_w"], lp["conv_b"],
                             tm, _pick_tile(F, 512))
    y = _ffn_down(f.reshape(B * S, F), lp["w_down"], x1, lp["g_ffn_post"], _pick_tile(B * S, 256))
    new_pool = u[:, S - POOL_HIST:, :]
    new_conv = tail[:, V7X_SUBLANES - (CONV_W - 1):, :]
    return y.reshape(B, S, D), k, v, new_pool, new_conv


def _sample_layer(x_tm, pool_hist, conv_hist, cache_k, cache_v, page_table, lp, n_seq, n_steps):
    rows, D = x_tm.shape
    F = lp["w_gate"].shape[1]
    past_len = page_table.shape[1] * PAGE_SIZE
    pos = past_len + jnp.repeat(jnp.arange(n_steps), n_seq)
    u, q, k, v = _inproj(x_tm[None], lp["g_mix_pre"], lp["w_in"], pos, rows)

    def to_seq_major(t):
        H = t.shape[0]
        return t.reshape(H, n_steps, n_seq, HEAD_DIM).transpose(2, 0, 1, 3)

    def rows_to_seq(t):
        return t.reshape(n_steps, n_seq, t.shape[-1]).transpose(1, 0, 2)

    def seq_to_rows(t):
        return t.transpose(1, 0, 2).reshape(-1, t.shape[-1])

    qs, ks, vs = to_seq_major(q[0]), to_seq_major(k[0]), to_seq_major(v[0])
    pooled = _pool_sample(u[0], seq_to_rows(pool_hist), lp["w_pool"], lp["pool_scale"],
                          n_steps, n_seq, past_len)
    attn = _moba_sample(qs, ks, vs, cache_k, cache_v, page_table)
    attn_tm = attn.transpose(2, 0, 1, 3).reshape(rows, -1).astype(BF16)
    x1, h2 = _outproj(pooled, attn_tm, x_tm, lp["w_out"], lp["g_mix_post"], lp["g_ffn_pre"], rows)
    f, gt = _ffn_up_sample(h2, lp["w_gate"], lp["w_up"], lp["conv_w"], lp["conv_b"],
                           seq_to_rows(conv_hist), n_steps, n_seq, _pick_tile(F, 512))
    y = _ffn_down(f, lp["w_down"], x1, lp["g_ffn_post"], rows)
    new_pool = jnp.concatenate([pool_hist, rows_to_seq(u[0])], axis=1)[:, n_steps:]
    new_conv = jnp.concatenate([conv_hist, rows_to_seq(gt)], axis=1)[:, n_steps:]
    return y, ks, vs, new_pool, new_conv


def kernel(x_prompt, x_sample, cache_k, cache_v, state_pool, state_conv, page_table,
           w_in, w_pool, pool_scale, w_out, g_mix_pre, g_mix_post,
           w_gate, w_up, conv_w, conv_b, w_down, g_ffn_pre, g_ffn_post):
    depth = w_in.shape[0]
    n_seq, n_steps, D = x_sample.shape
    assert w_in.shape[2] == 4 * pool_scale.shape[1], "pooling and attention widths must match"
    yp = x_prompt
    ys = x_sample.transpose(1, 0, 2).reshape(n_steps * n_seq, D)
    outs = [[] for _ in range(8)]
    for l in range(depth):
        lp = {"w_in": w_in[l].astype(BF16), "w_pool": w_pool[l].astype(BF16), "pool_scale": pool_scale[l],
              "w_out": w_out[l].astype(BF16), "g_mix_pre": g_mix_pre[l], "g_mix_post": g_mix_post[l],
              "w_gate": w_gate[l].astype(BF16), "w_up": w_up[l].astype(BF16), "conv_w": conv_w[l],
              "conv_b": conv_b[l], "w_down": w_down[l].astype(BF16), "g_ffn_pre": g_ffn_pre[l],
              "g_ffn_post": g_ffn_post[l]}
        yp, kp, vp, pp, cp = _prompt_layer(yp, lp)
        ys, ksn, vsn, ps, cs = _sample_layer(ys, state_pool[l], state_conv[l], cache_k[l], cache_v[l],
                                             page_table, lp, n_seq, n_steps)
        for lst, val in zip(outs, (kp, vp, ksn, vsn, pp, ps, cp, cs)):
            lst.append(val)
    y_sample = ys.reshape(n_steps, n_seq, D).transpose(1, 0, 2)
    return (yp, y_sample) + tuple(jnp.stack(o) for o in outs)
```

```python
import functools

import jax
import jax.numpy as jnp
from jax import lax
from jax.experimental import pallas as pl
from jax.experimental.pallas import tpu as pltpu

F32 = jnp.float32
BF16 = jnp.bfloat16

POOL_WINDOWS = (2, 4, 8, 16)
POOL_HIST = max(POOL_WINDOWS) - 1
HEAD_DIM = 128
ROT_DIM = HEAD_DIM // 4
ROPE_THETA = 500000.0
MOBA_BLOCK = 256
MOBA_TOPK = 3
PAGE_SIZE = 128
CONV_W = 3
EPS = 1e-6
NEG_INF = float("-inf")
LOG2E = 1.4426950408889634

V7X_SUBLANES = 8
V7X_VMEM_BYTES = 64 * 1024 * 1024
VMEM_RESERVE_BYTES = 8 * 1024 * 1024


def _vmem_limit(block_bytes):
    return int(min(2 * block_bytes, V7X_VMEM_BYTES - VMEM_RESERVE_BYTES))


def _params(semantics, block_bytes):
    return pltpu.CompilerParams(dimension_semantics=semantics,
                                vmem_limit_bytes=_vmem_limit(block_bytes))


def _resident_params(semantics, block_bytes):
    limit = int(min(block_bytes + VMEM_RESERVE_BYTES, V7X_VMEM_BYTES - VMEM_RESERVE_BYTES))
    return pltpu.CompilerParams(dimension_semantics=semantics, vmem_limit_bytes=limit)


def _rms(x, g):
    return x * lax.rsqrt(jnp.mean(x * x, axis=-1, keepdims=True) + EPS) * g


def _dot(a, b):
    return jnp.dot(a, b, preferred_element_type=F32)


def _dot_nt(a, b, precision=None):
    return lax.dot_general(a, b, (((1,), (1,)), ((), ())), precision=precision,
                           preferred_element_type=F32)


def _rope_tables(pos):
    half = ROT_DIM // 2
    inv = ROPE_THETA ** (-jnp.arange(half, dtype=F32) * (2.0 / ROT_DIM))
    ang = pos.astype(F32)[:, None] * inv[None, :]
    cos, sin = jnp.cos(ang), jnp.sin(ang)
    n = pos.shape[0]
    c = jnp.concatenate([cos, cos, jnp.ones((n, HEAD_DIM - ROT_DIM), F32)], axis=1)
    s_lo = jnp.concatenate([-sin, jnp.zeros((n, HEAD_DIM - half), F32)], axis=1)
    s_hi = jnp.concatenate([jnp.zeros((n, half), F32), sin,
                            jnp.zeros((n, HEAD_DIM - ROT_DIM), F32)], axis=1)
    return c, s_lo, s_hi


def _inproj_kernel(x_ref, g_ref, w_ref, c_ref, slo_ref, shi_ref,
                   u_ref, q_ref, k_ref, v_ref, hn_ref, *, n_heads, width):
    hn_ref[...] = _rms(x_ref[0], g_ref[...]).astype(BF16)
    pair = 2 * HEAD_DIM

    def rope(h):
        return (h * c_ref[...]
                + pltpu.roll(h, HEAD_DIM - ROT_DIM // 2, 1) * slo_ref[...]
                + pltpu.roll(h, ROT_DIM // 2, 1) * shi_ref[...])

    for p in range(width // pair):
        u_ref[0, :, p * pair:(p + 1) * pair] = _dot(hn_ref[...], w_ref[:, p * pair:(p + 1) * pair])
    for out_ref, part, rotary in ((q_ref, 1, True), (k_ref, 2, True), (v_ref, 3, False)):
        for p in range(n_heads // 2):
            col = part * width + p * pair
            res = _dot(hn_ref[...], w_ref[:, col:col + pair])
            for half in range(2):
                h = res[:, half * HEAD_DIM:(half + 1) * HEAD_DIM]
                out_ref[0, 2 * p + half] = rope(h) if rotary else h


def _inproj(x, g, w_bf, pos, tm):
    B, S, D = x.shape
    width = w_bf.shape[1] // 4
    n_heads = width // HEAD_DIM
    assert n_heads % 2 == 0
    c, s_lo, s_hi = _rope_tables(pos)
    tab_spec = pl.BlockSpec((tm, HEAD_DIM), lambda b, s: (s, 0))
    hm_spec = pl.BlockSpec((1, n_heads, tm, HEAD_DIM), lambda b, s: (b, 0, s, 0))
    hm_shape = jax.ShapeDtypeStruct((B, n_heads, S, HEAD_DIM), F32)
    block_bytes = 2 * tm * D * 4 + tm * D * 2 + D * 4 * width * 2 + 2 * 4 * tm * width * 4
    return pl.pallas_call(
        functools.partial(_inproj_kernel, n_heads=n_heads, width=width),
        grid=(B, S // tm),
        in_specs=[
            pl.BlockSpec((1, tm, D), lambda b, s: (b, s, 0)),
            pl.BlockSpec((1, D), lambda b, s: (0, 0)),
            pl.BlockSpec((D, 4 * width), lambda b, s: (0, 0), pipeline_mode=pl.Buffered(1)),
            tab_spec, tab_spec, tab_spec,
        ],
        out_specs=[pl.BlockSpec((1, tm, width), lambda b, s: (b, s, 0)), hm_spec, hm_spec, hm_spec],
        out_shape=[jax.ShapeDtypeStruct((B, S, width), F32), hm_shape, hm_shape, hm_shape],
        scratch_shapes=[pltpu.VMEM((tm, D), BF16)],
        compiler_params=_resident_params(("arbitrary", "arbitrary"), block_bytes),
        name="inproj",
    )(x, g.reshape(1, D), w_bf, c, s_lo, s_hi)


def _pool_kernel(u_ref, halo_ref, w_ref, scale_ref, o_ref, ext_ref, *, ts, group):
    s = pl.program_id(1)
    hist = POOL_HIST + 1
    u = u_ref[0]
    ext_ref[hist:, :] = u

    @pl.when(s == 0)
    def _():
        ext_ref[0:hist, :] = jnp.zeros((hist, ext_ref.shape[1]), F32)

    @pl.when(s > 0)
    def _():
        ext_ref[0:hist, :] = halo_ref[0]

    pos = s * ts + lax.broadcasted_iota(jnp.int32, (ts, 1), 0)
    for gi, w in enumerate(POOL_WINDOWS):
        cols = slice(gi * group, (gi + 1) * group)
        acc = u[:, cols]
        for back in range(1, w):
            acc = acc + ext_ref[hist - back:hist - back + ts, cols]
        cnt = jnp.minimum(pos + 1, w).astype(F32)
        d = (acc / cnt - u[:, cols]).astype(BF16)
        y = _dot(d, w_ref[gi]) * scale_ref[:, cols]
        o_ref[0, :, cols] = y.astype(BF16)


def _pool_prompt(u, w_pool_bf, scale, ts):
    B, S, C = u.shape
    n_groups, group, _ = w_pool_bf.shape
    hist = POOL_HIST + 1
    per = ts // hist
    block_bytes = 2 * ts * C * 4 + 2 * hist * C * 4 + 2 * n_groups * group * group * 2 \
        + 2 * ts * C * 2 + (ts + hist) * C * 4 + 4 * ts * group * 4
    return pl.pallas_call(
        functools.partial(_pool_kernel, ts=ts, group=group),
        grid=(B, S // ts),
        in_specs=[
            pl.BlockSpec((1, ts, C), lambda b, s: (b, s, 0)),
            pl.BlockSpec((1, hist, C), lambda b, s: (b, jnp.maximum(s * per - 1, 0), 0)),
            pl.BlockSpec((n_groups, group, group), lambda b, s: (0, 0, 0)),
            pl.BlockSpec((1, C), lambda b, s: (0, 0)),
        ],
        out_specs=pl.BlockSpec((1, ts, C), lambda b, s: (b, s, 0)),
        out_shape=jax.ShapeDtypeStruct((B, S, C), BF16),
        scratch_shapes=[pltpu.VMEM((ts + hist, C), F32)],
        compiler_params=_params(("arbitrary", "arbitrary"), block_bytes),
        name="pool_prompt",
    )(u, u, w_pool_bf, scale.reshape(1, C))


def _pool_sample_kernel(u_ref, hist_ref, w_ref, scale_ref, o_ref, *, n_steps, n_seq, group, past_len):
    def ext(i):
        if i < POOL_HIST:
            return hist_ref[i * n_seq:(i + 1) * n_seq, :]
        return u_ref[(i - POOL_HIST) * n_seq:(i - POOL_HIST + 1) * n_seq, :]

    for gi, w in enumerate(POOL_WINDOWS):
        cols = slice(gi * group, (gi + 1) * group)
        ds = []
        for t in range(n_steps):
            cur = ext(POOL_HIST + t)[:, cols]
            acc = cur
            for back in range(1, w):
                acc = acc + ext(POOL_HIST + t - back)[:, cols]
            cnt = float(min(past_len + t + 1, w))
            ds.append((acc / cnt - cur).astype(BF16))
        d = jnp.concatenate(ds, axis=0)
        y = _dot(d, w_ref[gi]) * scale_ref[:, cols]
        o_ref[:, cols] = y.astype(BF16)


def _pool_sample(u_tm, hist_tm, w_pool_bf, scale, n_steps, n_seq, past_len):
    rows, C = u_tm.shape
    n_groups, group, _ = w_pool_bf.shape
    block_bytes = 2 * (rows + hist_tm.shape[0]) * C * 4 + 2 * n_groups * group * group * 2 + 2 * rows * C * 2
    return pl.pallas_call(
        functools.partial(_pool_sample_kernel, n_steps=n_steps, n_seq=n_seq, group=group,
                          past_len=past_len),
        out_shape=jax.ShapeDtypeStruct((rows, C), BF16),
        compiler_params=pltpu.CompilerParams(vmem_limit_bytes=_vmem_limit(block_bytes)),
        name="pool_sample",
    )(u_tm, hist_tm, w_pool_bf, scale.reshape(1, C))


def _moba_prompt_kernel(q_ref, k_ref, v_ref, o_ref, kb_ref, vt_ref, km_ref, s2_ref, p2_ref, *, seq):
    blk = MOBA_BLOCK
    nb = seq // blk
    n_sel = min(MOBA_TOPK, nb - 1)
    scale = HEAD_DIM ** -0.5

    for n in range(nb):
        rows = slice(n * blk, (n + 1) * blk)
        kn = k_ref[0, 0, rows, :]
        kb_ref[rows, :] = kn.astype(BF16)
        km_ref[n:n + 1, :] = jnp.sum(kn, axis=0, keepdims=True) * (1.0 / blk)
        vt_ref[:, rows] = v_ref[0, 0, rows, :].T.astype(BF16)

    kmean = km_ref[...]
    blk_id = lax.broadcasted_iota(jnp.int32, (nb, blk), 0)
    key_i = lax.broadcasted_iota(jnp.int32, (blk, blk), 0)
    qry_i = lax.broadcasted_iota(jnp.int32, (blk, blk), 1)
    causal_bias = jnp.where(key_i <= qry_i, 0.0, NEG_INF).astype(F32)

    for i in range(nb):
        qi = q_ref[0, 0, i * blk:(i + 1) * blk, :]
        nk = (i + 1) * blk
        own = slice(i * blk, nk)
        s_ref = s2_ref.at[i % 2]
        p_ref = p2_ref.at[i % 2]
        s_ref[0:nk, :] = _dot_nt(kb_ref[0:nk, :], (qi * (scale * LOG2E)).astype(BF16))

        biases = []
        if i > 0:
            gate = _dot_nt(kmean, qi, precision=lax.Precision.HIGHEST)
            valid = jnp.where(blk_id < i, 1.0, 0.0)
            for n in range(i):
                gn = gate[n:n + 1, :]
                beats = jnp.where(blk_id < n, jnp.where(gate >= gn, 1.0, 0.0), jnp.where(gate > gn, 1.0, 0.0))
                rank = jnp.sum(beats * valid, axis=0, keepdims=True)
                biases.append(jnp.where(rank < n_sel, 0.0, NEG_INF).astype(F32))

        s_own = s_ref[own, :] + causal_bias
        m = jnp.max(s_own, axis=0, keepdims=True)
        for n in range(i):
            m = jnp.maximum(m, jnp.max(s_ref[n * blk:(n + 1) * blk, :], axis=0, keepdims=True) + biases[n])

        p = jnp.exp2(s_own - m)
        l = jnp.sum(p, axis=0, keepdims=True)
        p_ref[own, :] = p.astype(BF16)
        for n in range(i):
            rows = slice(n * blk, (n + 1) * blk)
            p = jnp.exp2(s_ref[rows, :] + (biases[n] - m))
            l = l + jnp.sum(p, axis=0, keepdims=True)
            p_ref[rows, :] = p.astype(BF16)

        o_t = _dot(vt_ref[:, 0:nk], p_ref[0:nk, :]) / l
        o_ref[0, i * blk:(i + 1) * blk, :] = o_t.T.astype(BF16)


def _moba_prompt(q, k, v):
    B, H, S, Dh = q.shape
    nb = S // MOBA_BLOCK
    in_spec = pl.BlockSpec((1, 1, S, Dh), lambda b, h: (b, h, 0, 0))
    block_bytes = 2 * 3 * S * Dh * 4 + 2 * S * Dh * 2 + 2 * S * Dh * 2 + 2 * S * MOBA_BLOCK * 6
    return pl.pallas_call(
        functools.partial(_moba_prompt_kernel, seq=S),
        grid=(B, H),
        in_specs=[in_spec, in_spec, in_spec],
        out_specs=pl.BlockSpec((1, S, Dh), lambda b, h: (b, 0, h)),
        out_shape=jax.ShapeDtypeStruct((B, S, H * Dh), BF16),
        scratch_shapes=[
            pltpu.VMEM((S, Dh), BF16),
            pltpu.VMEM((Dh, S), BF16),
            pltpu.VMEM((nb, Dh), F32),
            pltpu.VMEM((2, S, MOBA_BLOCK), F32),
            pltpu.VMEM((2, S, MOBA_BLOCK), BF16),
        ],
        compiler_params=_params(("arbitrary", "arbitrary"), block_bytes),
        name="moba_prompt",
    )(q, k, v)


def _moba_sample_kernel(pt_ref, qc_ref, qp_ref, kn_ref, vn_ref, ck_ref, cv_ref, o_ref,
                        kbuf, vbuf, km_ref, q8_ref, sel_ref, ksem, vsem,
                        *, n_pages, n_groups, hp, n_new, n_units):
    ppb = MOBA_BLOCK // PAGE_SIZE
    nblk = n_pages // ppb
    n_sel = min(MOBA_TOPK, nblk)
    scale = HEAD_DIM ** -0.5
    t = pl.program_id(0)
    n_slots_k = kbuf.shape[0]
    n_slots_v = vbuf.shape[0]

    def unit_bh(u):
        return u // n_groups, lax.rem(u, n_groups) * hp

    def k_copy(u, page, sl):
        bb, h0 = unit_bh(u)
        return pltpu.make_async_copy(ck_ref.at[pt_ref[bb, page], pl.ds(h0, hp)],
                                     kbuf.at[sl, :, pl.ds(page * PAGE_SIZE, PAGE_SIZE), :], ksem.at[sl])

    def start_k(u):
        sl = lax.rem(u, n_slots_k)

        def body(page, carry):
            k_copy(u, page, sl).start()
            return carry
        lax.fori_loop(0, n_pages, body, 0, unroll=8)

    def v_copy(u, vs, hh, j, c, pg, blk):
        bb, h0 = unit_bh(u)
        page = pt_ref[bb, blk * ppb + pg]
        dst = ((j * n_sel + c) * ppb + pg) * PAGE_SIZE
        return pltpu.make_async_copy(cv_ref.at[page, h0 + hh], vbuf.at[vs, hh, pl.ds(dst, PAGE_SIZE), :],
                                     vsem.at[vs])

    def sel_index(hh, j, c):
        return (hh * n_new + j) * n_sel + c

    def select(u):
        ks = lax.rem(u, n_slots_k)
        vs = lax.rem(u, n_slots_v)
        for page in range(n_pages):
            k_copy(u, page, ks).wait()
        for hh in range(hp):
            for n in range(nblk):
                kn_blk = kbuf[ks, hh, n * MOBA_BLOCK:(n + 1) * MOBA_BLOCK, :]
                km_ref[hh, n:n + 1, :] = jnp.sum(kn_blk, axis=0, keepdims=True) * (1.0 / MOBA_BLOCK)
            q8_ref[hh] = jnp.zeros(q8_ref.shape[1:], F32)
            q8_ref[hh, 0:n_new, :] = qc_ref[0, hh]
            gate = _dot_nt(q8_ref[hh], km_ref[hh], precision=lax.Precision.HIGHEST)
            lane = lax.broadcasted_iota(jnp.int32, gate.shape, 1)
            picks = [jnp.zeros((gate.shape[0], 1), F32) for _ in range(n_sel)]
            for n in range(nblk):
                gn = gate[:, n:n + 1]
                beats = jnp.where(lane < n, jnp.where(gate >= gn, 1.0, 0.0), jnp.where(gate > gn, 1.0, 0.0))
                rank = jnp.sum(beats, axis=1, keepdims=True)
                for c in range(n_sel):
                    picks[c] = jnp.where(rank == float(c), float(n), picks[c])
            for j in range(n_new):
                for c in range(n_sel):
                    blk = picks[c][j, 0].astype(jnp.int32)
                    sel_ref[vs, sel_index(hh, j, c)] = blk
                    for pg in range(ppb):
                        v_copy(u, vs, hh, j, c, pg, blk).start()

    def attend(u):
        ks = lax.rem(u, n_slots_k)
        vs = lax.rem(u, n_slots_v)
        new_row = lax.broadcasted_iota(jnp.int32, (n_new, 1), 0)
        probs = []
        for hh in range(hp):
            q = qp_ref[0, hh]
            kn = kn_ref[0, hh]
            for j in range(n_new):
                qj = q[j:j + 1, :]
                s_sel = []
                for c in range(n_sel):
                    start = pl.multiple_of(sel_ref[vs, sel_index(hh, j, c)] * MOBA_BLOCK, MOBA_BLOCK)
                    kblk = kbuf[ks, hh, pl.ds(start, MOBA_BLOCK), :]
                    s_sel.append(jnp.sum(kblk * qj, axis=1, keepdims=True) * scale)
                s_new = jnp.sum(kn * qj, axis=1, keepdims=True) * scale
                s_new = jnp.where(new_row <= j, s_new, NEG_INF)
                m = jnp.max(s_new, axis=0, keepdims=True)
                for s in s_sel:
                    m = jnp.maximum(m, jnp.max(s, axis=0, keepdims=True))
                p_sel = [jnp.exp(s - m) for s in s_sel]
                p_new = jnp.exp(s_new - m)
                l = jnp.sum(p_new, axis=0, keepdims=True)
                for p in p_sel:
                    l = l + jnp.sum(p, axis=0, keepdims=True)
                probs.append((p_sel, p_new, l))
        for hh in range(hp):
            for j in range(n_new):
                for c in range(n_sel):
                    blk = sel_ref[vs, sel_index(hh, j, c)]
                    for pg in range(ppb):
                        v_copy(u, vs, hh, j, c, pg, blk).wait()
        for hh in range(hp):
            vn = vn_ref[0, hh]
            for j in range(n_new):
                p_sel, p_new, l = probs[hh * n_new + j]
                acc = jnp.sum(p_new * vn, axis=0, keepdims=True)
                for c in range(n_sel):
                    row0 = (j * n_sel + c) * MOBA_BLOCK
                    acc = acc + jnp.sum(p_sel[c] * vbuf[vs, hh, row0:row0 + MOBA_BLOCK, :], axis=0,
                                        keepdims=True)
                o_ref[0, hh, j:j + 1, :] = acc / l

    @pl.when(t == 0)
    def _():
        start_k(t)

    @pl.when(t + 1 < n_units)
    def _():
        start_k(t + 1)

    @pl.when(t < n_units)
    def _():
        select(t)

    @pl.when(t > 0)
    def _():
        attend(t - 1)


def _moba_sample(q, k_new, v_new, cache_k, cache_v, page_table):
    Bd, H, L, Dh = q.shape
    n_pages = page_table.shape[1]
    ppb = MOBA_BLOCK // PAGE_SIZE
    assert n_pages % ppb == 0, "past length must be a whole number of MoBA blocks"
    nblk = n_pages // ppb
    n_sel = min(MOBA_TOPK, nblk)
    assert n_sel > 0 and L <= V7X_SUBLANES
    hp = 2 if H % 2 == 0 else 1
    n_groups = H // hp
    n_units = Bd * n_groups

    def unit_spec(lag):
        def index_map(t, pt):
            u = jnp.clip(t - lag, 0, n_units - 1)
            return (u // n_groups, lax.rem(u, n_groups), 0, 0)
        return pl.BlockSpec((1, hp, L, Dh), index_map)

    any_spec = pl.BlockSpec(memory_space=pl.ANY)
    k_slots, v_slots = 3, 2
    kbuf_shape = (k_slots, hp, n_pages * PAGE_SIZE, Dh)
    vbuf_shape = (v_slots, hp, L * n_sel * MOBA_BLOCK, Dh)
    scratch_bytes = 4 * (k_slots * hp * n_pages * PAGE_SIZE * Dh + v_slots * hp * L * n_sel * MOBA_BLOCK * Dh)
    grid_spec = pltpu.PrefetchScalarGridSpec(
        num_scalar_prefetch=1,
        grid=(n_units + 1,),
        in_specs=[unit_spec(0), unit_spec(1), unit_spec(1), unit_spec(1), any_spec, any_spec],
        out_specs=unit_spec(1),
        scratch_shapes=[
            pltpu.VMEM(kbuf_shape, F32),
            pltpu.VMEM(vbuf_shape, F32),
            pltpu.VMEM((hp, nblk, Dh), F32),
            pltpu.VMEM((hp, V7X_SUBLANES, Dh), F32),
            pltpu.SMEM((v_slots, hp * L * n_sel), jnp.int32),
            pltpu.SemaphoreType.DMA((k_slots,)),
            pltpu.SemaphoreType.DMA((v_slots,)),
        ],
    )
    limit = int(min(scratch_bytes + VMEM_RESERVE_BYTES, V7X_VMEM_BYTES - VMEM_RESERVE_BYTES))
    return pl.pallas_call(
        functools.partial(_moba_sample_kernel, n_pages=n_pages, n_groups=n_groups, hp=hp, n_new=L,
                          n_units=n_units),
        grid_spec=grid_spec,
        out_shape=jax.ShapeDtypeStruct((Bd, H, L, Dh), F32),
        compiler_params=pltpu.CompilerParams(dimension_semantics=("arbitrary",), vmem_limit_bytes=limit),
        name="moba_sample",
    )(page_table, q, q, k_new, v_new, cache_k, cache_v)


def _outproj_kernel(pooled_ref, attn_ref, x_ref, w_ref, gpost_ref, gffn_ref, x1_ref, h2_ref, *, split):
    mix = _dot(pooled_ref[...], w_ref[0:split, :]) + _dot(attn_ref[...], w_ref[split:, :])
    x1 = x_ref[...] + _rms(mix, gpost_ref[...])
    x1_ref[...] = x1
    h2_ref[...] = _rms(x1, gffn_ref[...]).astype(BF16)


def _outproj(pooled, attn, x, w_bf, g_post, g_ffn, tm):
    rows, D = x.shape
    split = pooled.shape[1]
    wa = attn.shape[1]
    row_spec = lambda width: pl.BlockSpec((tm, width), lambda i: (i, 0))
    vec_spec = pl.BlockSpec((1, D), lambda i: (0, 0))
    block_bytes = 2 * tm * (split + wa) * 2 + 2 * tm * D * 4 + 2 * (split + wa) * D * 2 \
        + 2 * tm * D * 6 + 2 * tm * D * 4
    return pl.pallas_call(
        functools.partial(_outproj_kernel, split=split),
        grid=(rows // tm,),
        in_specs=[row_spec(split), row_spec(wa), row_spec(D),
                  pl.BlockSpec((split + wa, D), lambda i: (0, 0)), vec_spec, vec_spec],
        out_specs=[row_spec(D), row_spec(D)],
        out_shape=[jax.ShapeDtypeStruct((rows, D), F32), jax.ShapeDtypeStruct((rows, D), BF16)],
        compiler_params=_params(("arbitrary",), block_bytes),
        name="outproj",
    )(pooled, attn, x, w_bf, g_post.reshape(1, D), g_ffn.reshape(1, D))


def _gelu_tanh(c):
    return c * (0.5 * (1.0 + jnp.tanh(0.7978845608028654 * (c + 0.044715 * (c * c * c)))))


def _conv_gelu_gate(gt, prev8, up, cw_ref, cb_ref):
    pad = V7X_SUBLANES
    row8 = lax.broadcasted_iota(jnp.int32, prev8.shape, 0)

    def shifted(k):
        if k == 0:
            return gt
        rolled = pltpu.roll(gt, k, 0)
        head = jnp.where(row8 < k, pltpu.roll(prev8, k, 0), rolled[0:pad])
        return jnp.concatenate([head, rolled[pad:]], axis=0)

    c = cb_ref[...]
    for i in range(CONV_W):
        c = c + shifted(CONV_W - 1 - i) * cw_ref[i:i + 1, :]
    return _gelu_tanh(c) * up


def _ffn_up_kernel(h_ref, hs_ref, wg_ref, wu_ref, cw_ref, cb_ref, hist_ref,
                   f_ref, tail_ref, fs_ref, gts_ref, wgb_ref, wub_ref, halo_ref,
                   *, tm, sub, n_steps, n_seq):
    b = pl.program_id(1)
    s = pl.program_id(2)
    pad = V7X_SUBLANES

    @pl.when((b == 0) & (s == 0))
    def _():
        wgb_ref[...] = wg_ref[...].astype(BF16)
        wub_ref[...] = wu_ref[...].astype(BF16)
        hs = hs_ref[...]
        gt = _dot(hs, wgb_ref[...])
        up = _dot(hs, wub_ref[...])
        gts_ref[...] = gt
        ext = [hist_ref[i * n_seq:(i + 1) * n_seq, :] for i in range(CONV_W - 1)]
        ext += [gt[t * n_seq:(t + 1) * n_seq, :] for t in range(n_steps)]
        for t in range(n_steps):
            c = cb_ref[...]
            for i in range(CONV_W):
                c = c + ext[t + i] * cw_ref[i:i + 1, :]
            rows = slice(t * n_seq, (t + 1) * n_seq)
            fs_ref[rows, :] = (_gelu_tanh(c) * up[rows, :]).astype(BF16)

    @pl.when(s == 0)
    def _():
        halo_ref[...] = jnp.zeros(halo_ref.shape, F32)

    prev8 = halo_ref[...]
    for r in range(tm // sub):
        rows = slice(r * sub, (r + 1) * sub)
        hr = h_ref[0, rows, :]
        gt = _dot(hr, wgb_ref[...])
        up = _dot(hr, wub_ref[...])
        f_ref[0, rows, :] = _conv_gelu_gate(gt, prev8, up, cw_ref, cb_ref).astype(BF16)
        prev8 = gt[sub - pad:sub, :]
    halo_ref[...] = prev8
    tail_ref[0] = prev8


def _ffn_up(h2, h2_s, w_gate, w_up, conv_w, conv_b, hist_tm, n_steps, n_seq, tm, tn, sub):
    B, S, D = h2.shape
    rows_s = h2_s.shape[0]
    F = w_gate.shape[1]
    pad = V7X_SUBLANES
    w_spec = pl.BlockSpec((D, tn), lambda j, b, s: (0, j))
    col_spec = lambda r: pl.BlockSpec((r, tn), lambda j, b, s: (0, j))
    block_bytes = (2 * tm * D * 2 + 2 * rows_s * D * 2 + 2 * 2 * D * tn * 4 + 2 * D * tn * 2
                   + 2 * tm * tn * 2 + 2 * rows_s * tn * 6 + 8 * sub * tn * 4)
    return pl.pallas_call(
        functools.partial(_ffn_up_kernel, tm=tm, sub=sub, n_steps=n_steps, n_seq=n_seq),
        grid=(F // tn, B, S // tm),
        in_specs=[
            pl.BlockSpec((1, tm, D), lambda j, b, s: (b, s, 0)),
            pl.BlockSpec((rows_s, D), lambda j, b, s: (0, 0)),
            w_spec, w_spec,
            col_spec(CONV_W), col_spec(1), col_spec(hist_tm.shape[0]),
        ],
        out_specs=[pl.BlockSpec((1, tm, tn), lambda j, b, s: (b, s, j)),
                   pl.BlockSpec((1, pad, tn), lambda j, b, s: (b, 0, j)),
                   col_spec(rows_s), col_spec(rows_s)],
        out_shape=[jax.ShapeDtypeStruct((B, S, F), BF16), jax.ShapeDtypeStruct((B, pad, F), F32),
                   jax.ShapeDtypeStruct((rows_s, F), BF16), jax.ShapeDtypeStruct((rows_s, F), F32)],
        scratch_shapes=[pltpu.VMEM((D, tn), BF16), pltpu.VMEM((D, tn), BF16), pltpu.VMEM((pad, tn), F32)],
        compiler_params=_params(("arbitrary", "arbitrary", "arbitrary"), block_bytes),
        name="ffn_up",
    )(h2, h2_s, w_gate, w_up, conv_w, conv_b.reshape(1, F), hist_tm)


def _ffn_down_kernel(f_ref, w_ref, x1_ref, g_ref, y_ref):
    y_ref[...] = x1_ref[...] + _rms(_dot(f_ref[...], w_ref[...]), g_ref[...])


def _ffn_down(f, w_bf, x1, g, tm):
    rows, F = f.shape
    D = w_bf.shape[1]
    block_bytes = 2 * tm * F * 2 + F * D * 2 + 4 * tm * D * 4 + 2 * tm * D * 4
    return pl.pallas_call(
        _ffn_down_kernel,
        grid=(rows // tm,),
        in_specs=[pl.BlockSpec((tm, F), lambda i: (i, 0)),
                  pl.BlockSpec((F, D), lambda i: (0, 0), pipeline_mode=pl.Buffered(1)),
                  pl.BlockSpec((tm, D), lambda i: (i, 0)),
                  pl.BlockSpec((1, D), lambda i: (0, 0))],
        out_specs=pl.BlockSpec((tm, D), lambda i: (i, 0)),
        out_shape=jax.ShapeDtypeStruct((rows, D), F32),
        compiler_params=_resident_params(("arbitrary",), block_bytes),
        name="ffn_down",
    )(f, w_bf, x1, g.reshape(1, D))


def _pick_tile(n, target):
    t = min(n, target)
    while n % t:
        t //= 2
    return t


def _prompt_mixer(x, lp):
    B, S, D = x.shape
    tm = _pick_tile(S, 512)
    u, q, k, v = _inproj(x, lp["g_mix_pre"], lp["w_in"], jnp.arange(S), tm)
    pooled = _pool_prompt(u, lp["w_pool"], lp["pool_scale"], tm)
    attn = _moba_prompt(q, k, v)
    x1, h2 = _outproj(pooled.reshape(B * S, -1), attn.reshape(B * S, -1), x.reshape(B * S, D),
                      lp["w_out"], lp["g_mix_post"], lp["g_ffn_pre"], tm)
    return x1, h2, k, v, u


def _sample_mixer(x_tm, pool_hist, cache_k, cache_v, page_table, lp, n_seq, n_steps):
    rows = x_tm.shape[0]
    past_len = page_table.shape[1] * PAGE_SIZE
    pos = past_len + jnp.repeat(jnp.arange(n_steps), n_seq)
    u, q, k, v = _inproj(x_tm[None], lp["g_mix_pre"], lp["w_in"], pos, rows)

    def to_seq_major(t):
        H = t.shape[0]
        return t.reshape(H, n_steps, n_seq, HEAD_DIM).transpose(2, 0, 1, 3)

    qs, ks, vs = to_seq_major(q[0]), to_seq_major(k[0]), to_seq_major(v[0])
    pooled = _pool_sample(u[0], _seq_to_rows(pool_hist), lp["w_pool"], lp["pool_scale"],
                          n_steps, n_seq, past_len)
    attn = _moba_sample(qs, ks, vs, cache_k, cache_v, page_table)
    attn_tm = attn.transpose(2, 0, 1, 3).reshape(rows, -1).astype(BF16)
    x1, h2 = _outproj(pooled, attn_tm, x_tm, lp["w_out"], lp["g_mix_post"], lp["g_ffn_pre"], rows)
    return x1, h2, ks, vs, u[0]


def _rows_to_seq(t, n_steps, n_seq):
    return t.reshape(n_steps, n_seq, t.shape[-1]).transpose(1, 0, 2)


def _seq_to_rows(t):
    return t.transpose(1, 0, 2).reshape(-1, t.shape[-1])


def _layer(xp, xs_tm, pool_hist, conv_hist, cache_k, cache_v, page_table, lp, n_seq, n_steps):
    B, S, D = xp.shape
    F = lp["w_gate"].shape[1]
    x1p, h2p, kp, vp, up = _prompt_mixer(xp, lp)
    x1s, h2s, ks, vs, us = _sample_mixer(xs_tm, pool_hist, cache_k, cache_v, page_table, lp, n_seq, n_steps)
    tm = _pick_tile(S, 1024)
    fp, tail, fs, gts = _ffn_up(h2p.reshape(B, S, D), h2s, lp["w_gate"], lp["w_up"], lp["conv_w"],
                                lp["conv_b"], _seq_to_rows(conv_hist), n_steps, n_seq,
                                tm, _pick_tile(F, 512), _pick_tile(tm, 256))
    yp = _ffn_down(fp.reshape(B * S, F), lp["w_down"], x1p, lp["g_ffn_post"], _pick_tile(B * S, 256))
    ys = _ffn_down(fs, lp["w_down"], x1s, lp["g_ffn_post"], xs_tm.shape[0])
    pool_p = up[:, S - POOL_HIST:, :]
    conv_p = tail[:, V7X_SUBLANES - (CONV_W - 1):, :]
    pool_s = jnp.concatenate([pool_hist, _rows_to_seq(us, n_steps, n_seq)], axis=1)[:, n_steps:]
    conv_s = jnp.concatenate([conv_hist, _rows_to_seq(gts, n_steps, n_seq)], axis=1)[:, n_steps:]
    return yp.reshape(B, S, D), ys, (kp, vp, ks, vs, pool_p, pool_s, conv_p, conv_s)


def kernel(x_prompt, x_sample, cache_k, cache_v, state_pool, state_conv, page_table,
           w_in, w_pool, pool_scale, w_out, g_mix_pre, g_mix_post,
           w_gate, w_up, conv_w, conv_b, w_down, g_ffn_pre, g_ffn_post):
    depth = w_in.shape[0]
    n_seq, n_steps, D = x_sample.shape
    assert w_in.shape[2] == 4 * pool_scale.shape[1], "pooling and attention widths must match"
    yp = x_prompt
    ys = x_sample.transpose(1, 0, 2).reshape(n_steps * n_seq, D)
    outs = [[] for _ in range(8)]
    for l in range(depth):
        lp = {"w_in": w_in[l].astype(BF16), "w_pool": w_pool[l].astype(BF16), "pool_scale": pool_scale[l],
              "w_out": w_out[l].astype(BF16), "g_mix_pre": g_mix_pre[l], "g_mix_post": g_mix_post[l],
              "w_gate": w_gate[l], "w_up": w_up[l], "conv_w": conv_w[l],
              "conv_b": conv_b[l], "w_down": w_down[l].astype(BF16), "g_ffn_pre": g_ffn_pre[l],
              "g_ffn_post": g_ffn_post[l]}
        yp, ys, states = _layer(yp, ys, state_pool[l], state_conv[l], cache_k[l], cache_v[l],
                                page_table, lp, n_seq, n_steps)
        for lst, val in zip(outs, states):
            lst.append(val)
    y_sample = ys.reshape(n_steps, n_seq, D).transpose(1, 0, 2)
    return (yp, y_sample) + tuple(jnp.stack(o) for o in outs)
```

```python
import functools

import jax
import jax.numpy as jnp
from jax import lax
from jax.experimental import pallas as pl
from jax.experimental.pallas import tpu as pltpu

F32 = jnp.float32
BF16 = jnp.bfloat16

POOL_WINDOWS = (2, 4, 8, 16)
POOL_HIST = max(POOL_WINDOWS) - 1
HEAD_DIM = 128
ROT_DIM = HEAD_DIM // 4
ROPE_THETA = 500000.0
MOBA_BLOCK = 256
MOBA_TOPK = 3
QUERY_BLOCKS_PER_DOT = 2
PAGE_SIZE = 128
CONV_W = 3
EPS = 1e-6
NEG_INF = float("-inf")
LOG2E = 1.4426950408889634

V7X_SUBLANES = 8
V7X_VMEM_BYTES = 64 * 1024 * 1024
VMEM_RESERVE_BYTES = 8 * 1024 * 1024


def _vmem_limit(block_bytes):
    return int(min(2 * block_bytes, V7X_VMEM_BYTES - VMEM_RESERVE_BYTES))


def _params(semantics, block_bytes):
    return pltpu.CompilerParams(dimension_semantics=semantics,
                                vmem_limit_bytes=_vmem_limit(block_bytes))


def _resident_params(semantics, block_bytes):
    limit = int(min(block_bytes + VMEM_RESERVE_BYTES, V7X_VMEM_BYTES - VMEM_RESERVE_BYTES))
    return pltpu.CompilerParams(dimension_semantics=semantics, vmem_limit_bytes=limit)


def _rms(x, g):
    return x * lax.rsqrt(jnp.mean(x * x, axis=-1, keepdims=True) + EPS) * g


def _dot(a, b):
    return jnp.dot(a, b, preferred_element_type=F32)


def _dot_nt(a, b, precision=None):
    return lax.dot_general(a, b, (((1,), (1,)), ((), ())), precision=precision,
                           preferred_element_type=F32)


def _rope_tables(pos):
    half = ROT_DIM // 2
    inv = ROPE_THETA ** (-jnp.arange(half, dtype=F32) * (2.0 / ROT_DIM))
    ang = pos.astype(F32)[:, None] * inv[None, :]
    cos, sin = jnp.cos(ang), jnp.sin(ang)
    n = pos.shape[0]
    c = jnp.concatenate([cos, cos, jnp.ones((n, HEAD_DIM - ROT_DIM), F32)], axis=1)
    s_lo = jnp.concatenate([-sin, jnp.zeros((n, HEAD_DIM - half), F32)], axis=1)
    s_hi = jnp.concatenate([jnp.zeros((n, half), F32), sin,
                            jnp.zeros((n, HEAD_DIM - ROT_DIM), F32)], axis=1)
    return c, s_lo, s_hi


def _inproj_kernel(x_ref, g_ref, w_ref, c_ref, slo_ref, shi_ref,
                   u_ref, q_ref, k_ref, v_ref, hn_ref, *, n_heads, width):
    hn_ref[...] = _rms(x_ref[0], g_ref[...]).astype(BF16)
    pair = 2 * HEAD_DIM

    def rope(h):
        return (h * c_ref[...]
                + pltpu.roll(h, HEAD_DIM - ROT_DIM // 2, 1) * slo_ref[...]
                + pltpu.roll(h, ROT_DIM // 2, 1) * shi_ref[...])

    for p in range(width // pair):
        u_ref[0, :, p * pair:(p + 1) * pair] = _dot(hn_ref[...], w_ref[:, p * pair:(p + 1) * pair])
    for out_ref, part, rotary in ((q_ref, 1, True), (k_ref, 2, True), (v_ref, 3, False)):
        for p in range(n_heads // 2):
            col = part * width + p * pair
            res = _dot(hn_ref[...], w_ref[:, col:col + pair])
            for half in range(2):
                h = res[:, half * HEAD_DIM:(half + 1) * HEAD_DIM]
                out_ref[0, 2 * p + half] = rope(h) if rotary else h


def _inproj(x, g, w_bf, pos, tm):
    B, S, D = x.shape
    width = w_bf.shape[1] // 4
    n_heads = width // HEAD_DIM
    assert n_heads % 2 == 0
    c, s_lo, s_hi = _rope_tables(pos)
    tab_spec = pl.BlockSpec((tm, HEAD_DIM), lambda b, s: (s, 0))
    hm_spec = pl.BlockSpec((1, n_heads, tm, HEAD_DIM), lambda b, s: (b, 0, s, 0))
    hm_shape = jax.ShapeDtypeStruct((B, n_heads, S, HEAD_DIM), F32)
    block_bytes = 2 * tm * D * 4 + tm * D * 2 + D * 4 * width * 2 + 2 * 4 * tm * width * 4
    return pl.pallas_call(
        functools.partial(_inproj_kernel, n_heads=n_heads, width=width),
        grid=(B, S // tm),
        in_specs=[
            pl.BlockSpec((1, tm, D), lambda b, s: (b, s, 0)),
            pl.BlockSpec((1, D), lambda b, s: (0, 0)),
            pl.BlockSpec((D, 4 * width), lambda b, s: (0, 0), pipeline_mode=pl.Buffered(1)),
            tab_spec, tab_spec, tab_spec,
        ],
        out_specs=[pl.BlockSpec((1, tm, width), lambda b, s: (b, s, 0)), hm_spec, hm_spec, hm_spec],
        out_shape=[jax.ShapeDtypeStruct((B, S, width), F32), hm_shape, hm_shape, hm_shape],
        scratch_shapes=[pltpu.VMEM((tm, D), BF16)],
        compiler_params=_resident_params(("arbitrary", "arbitrary"), block_bytes),
        name="inproj",
    )(x, g.reshape(1, D), w_bf, c, s_lo, s_hi)


def _pool_kernel(u_ref, halo_ref, w_ref, scale_ref, o_ref, ext_ref, *, ts, group):
    s = pl.program_id(1)
    hist = POOL_HIST + 1
    u = u_ref[0]
    ext_ref[hist:, :] = u

    @pl.when(s == 0)
    def _():
        ext_ref[0:hist, :] = jnp.zeros((hist, ext_ref.shape[1]), F32)

    @pl.when(s > 0)
    def _():
        ext_ref[0:hist, :] = halo_ref[0]

    pos = s * ts + lax.broadcasted_iota(jnp.int32, (ts, 1), 0)
    for gi, w in enumerate(POOL_WINDOWS):
        cols = slice(gi * group, (gi + 1) * group)
        acc = u[:, cols]
        for back in range(1, w):
            acc = acc + ext_ref[hist - back:hist - back + ts, cols]
        cnt = jnp.minimum(pos + 1, w).astype(F32)
        d = (acc / cnt - u[:, cols]).astype(BF16)
        y = _dot(d, w_ref[gi]) * scale_ref[:, cols]
        o_ref[0, :, cols] = y.astype(BF16)


def _pool_prompt(u, w_pool_bf, scale, ts):
    B, S, C = u.shape
    n_groups, group, _ = w_pool_bf.shape
    hist = POOL_HIST + 1
    per = ts // hist
    block_bytes = 2 * ts * C * 4 + 2 * hist * C * 4 + 2 * n_groups * group * group * 2 \
        + 2 * ts * C * 2 + (ts + hist) * C * 4 + 4 * ts * group * 4
    return pl.pallas_call(
        functools.partial(_pool_kernel, ts=ts, group=group),
        grid=(B, S // ts),
        in_specs=[
            pl.BlockSpec((1, ts, C), lambda b, s: (b, s, 0)),
            pl.BlockSpec((1, hist, C), lambda b, s: (b, jnp.maximum(s * per - 1, 0), 0)),
            pl.BlockSpec((n_groups, group, group), lambda b, s: (0, 0, 0)),
            pl.BlockSpec((1, C), lambda b, s: (0, 0)),
        ],
        out_specs=pl.BlockSpec((1, ts, C), lambda b, s: (b, s, 0)),
        out_shape=jax.ShapeDtypeStruct((B, S, C), BF16),
        scratch_shapes=[pltpu.VMEM((ts + hist, C), F32)],
        compiler_params=_params(("arbitrary", "arbitrary"), block_bytes),
        name="pool_prompt",
    )(u, u, w_pool_bf, scale.reshape(1, C))


def _pool_sample_kernel(u_ref, hist_ref, w_ref, scale_ref, o_ref, *, n_steps, n_seq, group, past_len):
    def ext(i):
        if i < POOL_HIST:
            return hist_ref[i * n_seq:(i + 1) * n_seq, :]
        return u_ref[(i - POOL_HIST) * n_seq:(i - POOL_HIST + 1) * n_seq, :]

    for gi, w in enumerate(POOL_WINDOWS):
        cols = slice(gi * group, (gi + 1) * group)
        ds = []
        for t in range(n_steps):
            cur = ext(POOL_HIST + t)[:, cols]
            acc = cur
            for back in range(1, w):
                acc = acc + ext(POOL_HIST + t - back)[:, cols]
            cnt = float(min(past_len + t + 1, w))
            ds.append((acc / cnt - cur).astype(BF16))
        d = jnp.concatenate(ds, axis=0)
        y = _dot(d, w_ref[gi]) * scale_ref[:, cols]
        o_ref[:, cols] = y.astype(BF16)


def _pool_sample(u_tm, hist_tm, w_pool_bf, scale, n_steps, n_seq, past_len):
    rows, C = u_tm.shape
    n_groups, group, _ = w_pool_bf.shape
    block_bytes = 2 * (rows + hist_tm.shape[0]) * C * 4 + 2 * n_groups * group * group * 2 + 2 * rows * C * 2
    return pl.pallas_call(
        functools.partial(_pool_sample_kernel, n_steps=n_steps, n_seq=n_seq, group=group,
                          past_len=past_len),
        out_shape=jax.ShapeDtypeStruct((rows, C), BF16),
        compiler_params=pltpu.CompilerParams(vmem_limit_bytes=_vmem_limit(block_bytes)),
        name="pool_sample",
    )(u_tm, hist_tm, w_pool_bf, scale.reshape(1, C))


class _PerHead:
    def __init__(self, refs):
        self.refs = refs

    def __getitem__(self, idx):
        return self.refs[idx[0]][idx[1:]] if isinstance(idx, tuple) else self.refs[idx][...]

    def __setitem__(self, idx, value):
        self.refs[idx[0]][idx[1:]] = value


def _moba_prompt_kernel(q_ref, k_ref, v_ref, o_ref, *scratch, seq, hp):
    blk = MOBA_BLOCK
    nb = seq // blk
    n_sel = min(MOBA_TOPK, nb - 1)
    scale = HEAD_DIM ** -0.5
    kb_ref, vt_ref, km_ref, s_ref, p_ref = (_PerHead(scratch[kind * hp:(kind + 1) * hp]) for kind in range(5))

    for hh in range(hp):
        for n in range(nb):
            rows = slice(n * blk, (n + 1) * blk)
            kn = k_ref[0, hh, rows, :]
            kb_ref[hh, rows, :] = kn.astype(BF16)
            km_ref[hh, n:n + 1, :] = jnp.sum(kn, axis=0, keepdims=True) * (1.0 / blk)
            vt_ref[hh, :, rows] = v_ref[0, hh, rows, :].T.astype(BF16)

    blk_id = lax.broadcasted_iota(jnp.int32, (nb, blk), 0)
    key_i = lax.broadcasted_iota(jnp.int32, (blk, blk), 0)
    qry_i = lax.broadcasted_iota(jnp.int32, (blk, blk), 1)
    causal_bias = jnp.where(key_i <= qry_i, 0.0, NEG_INF).astype(F32)

    def scores(hh, qblocks):
        first, last = qblocks[0], qblocks[-1]
        nk = (last + 1) * blk
        width = len(qblocks) * blk
        qg = q_ref[0, hh, first * blk:nk, :]
        s_ref[hh, 0:nk, 0:width] = _dot_nt(kb_ref[hh, 0:nk, :], (qg * (scale * LOG2E)).astype(BF16))
        return _dot_nt(km_ref[hh], qg, precision=lax.Precision.HIGHEST) if last > 0 else None

    def softmax(hh, qblocks, gate):
        last = qblocks[-1]
        sums = []
        for a, i in enumerate(qblocks):
            cols = slice(a * blk, (a + 1) * blk)
            own = slice(i * blk, (i + 1) * blk)
            biases = []
            if i > 0:
                gate_a = gate[:, cols]
                valid = jnp.where(blk_id < i, 1.0, 0.0)
                for n in range(i):
                    gn = gate_a[n:n + 1, :]
                    beats = jnp.where(blk_id < n, jnp.where(gate_a >= gn, 1.0, 0.0),
                                      jnp.where(gate_a > gn, 1.0, 0.0))
                    rank = jnp.sum(beats * valid, axis=0, keepdims=True)
                    biases.append(jnp.where(rank < n_sel, 0.0, NEG_INF).astype(F32))

            s_own = s_ref[hh, own, cols] + causal_bias
            m = jnp.max(s_own, axis=0, keepdims=True)
            for n in range(i):
                m = jnp.maximum(m, jnp.max(s_ref[hh, n * blk:(n + 1) * blk, cols], axis=0, keepdims=True)
                                + biases[n])

            p = jnp.exp2(s_own - m)
            l = jnp.sum(p, axis=0, keepdims=True)
            p_ref[hh, own, cols] = p.astype(BF16)
            for n in range(i):
                rows = slice(n * blk, (n + 1) * blk)
                p = jnp.exp2(s_ref[hh, rows, cols] + (biases[n] - m))
                l = l + jnp.sum(p, axis=0, keepdims=True)
                p_ref[hh, rows, cols] = p.astype(BF16)
            for n in range(i + 1, last + 1):
                p_ref[hh, n * blk:(n + 1) * blk, cols] = jnp.zeros((blk, blk), BF16)
            sums.append(l)
        return sums

    def output(hh, qblocks, sums):
        nk = (qblocks[-1] + 1) * blk
        width = len(qblocks) * blk
        o_t = _dot(vt_ref[hh, :, 0:nk], p_ref[hh, 0:nk, 0:width])
        for a, i in enumerate(qblocks):
            o_a = o_t[:, a * blk:(a + 1) * blk] / sums[a]
            o_ref[0, i * blk:(i + 1) * blk, hh * HEAD_DIM:(hh + 1) * HEAD_DIM] = o_a.T.astype(BF16)

    items = [(hh, list(range(first, min(first + QUERY_BLOCKS_PER_DOT, nb))))
             for first in range(0, nb, QUERY_BLOCKS_PER_DOT) for hh in range(hp)]
    gate = scores(*items[0])
    for cur, nxt in zip(items, items[1:] + [None]):
        next_gate = scores(*nxt) if nxt is not None else None
        output(*cur, softmax(*cur, gate))
        gate = next_gate


def _moba_prompt(q, k, v):
    B, H, S, Dh = q.shape
    nb = S // MOBA_BLOCK
    hp = 2 if H % 2 == 0 else 1
    width = QUERY_BLOCKS_PER_DOT * MOBA_BLOCK
    in_spec = pl.BlockSpec((1, hp, S, Dh), lambda b, g: (b, g, 0, 0))
    block_bytes = hp * (2 * 3 * S * Dh * 4 + 2 * S * Dh * 2 + 2 * S * Dh * 2 + S * width * 6)
    return pl.pallas_call(
        functools.partial(_moba_prompt_kernel, seq=S, hp=hp),
        grid=(B, H // hp),
        in_specs=[in_spec, in_spec, in_spec],
        out_specs=pl.BlockSpec((1, S, hp * Dh), lambda b, g: (b, 0, g)),
        out_shape=jax.ShapeDtypeStruct((B, S, H * Dh), BF16),
        scratch_shapes=(
            [pltpu.VMEM((S, Dh), BF16)] * hp
            + [pltpu.VMEM((Dh, S), BF16)] * hp
            + [pltpu.VMEM((nb, Dh), F32)] * hp
            + [pltpu.VMEM((S, width), F32)] * hp
            + [pltpu.VMEM((S, width), BF16)] * hp
        ),
        compiler_params=_params(("arbitrary", "arbitrary"), block_bytes),
        name="moba_prompt",
    )(q, k, v)


def _moba_sample_kernel(pt_ref, qc_ref, qp_ref, kn_ref, vn_ref, ck_ref, cv_ref, o_ref,
                        kbuf, vbuf, km_ref, q8_ref, sel_ref, ksem, vsem,
                        *, n_pages, n_groups, hp, n_new, n_units):
    ppb = MOBA_BLOCK // PAGE_SIZE
    nblk = n_pages // ppb
    n_sel = min(MOBA_TOPK, nblk)
    scale = HEAD_DIM ** -0.5
    t = pl.program_id(0)
    n_slots_k = kbuf.shape[0]
    n_slots_v = vbuf.shape[0]

    def unit_bh(u):
        return u // n_groups, lax.rem(u, n_groups) * hp

    def k_copy(u, page, sl):
        bb, h0 = unit_bh(u)
        return pltpu.make_async_copy(ck_ref.at[pt_ref[bb, page], pl.ds(h0, hp)],
                                     kbuf.at[sl, :, pl.ds(page * PAGE_SIZE, PAGE_SIZE), :], ksem.at[sl])

    def start_k(u):
        sl = lax.rem(u, n_slots_k)

        def body(page, carry):
            k_copy(u, page, sl).start()
            return carry
        lax.fori_loop(0, n_pages, body, 0, unroll=8)

    def v_copy(u, vs, hh, j, c, pg, blk):
        bb, h0 = unit_bh(u)
        page = pt_ref[bb, blk * ppb + pg]
        dst = ((j * n_sel + c) * ppb + pg) * PAGE_SIZE
        return pltpu.make_async_copy(cv_ref.at[page, h0 + hh], vbuf.at[vs, hh, pl.ds(dst, PAGE_SIZE), :],
                                     vsem.at[vs])

    def sel_index(hh, j, c):
        return (hh * n_new + j) * n_sel + c

    def select(u):
        ks = lax.rem(u, n_slots_k)
        vs = lax.rem(u, n_slots_v)
        for page in range(n_pages):
            k_copy(u, page, ks).wait()
        for hh in range(hp):
            for n in range(nblk):
                kn_blk = kbuf[ks, hh, n * MOBA_BLOCK:(n + 1) * MOBA_BLOCK, :]
                km_ref[hh, n:n + 1, :] = jnp.sum(kn_blk, axis=0, keepdims=True) * (1.0 / MOBA_BLOCK)
            q8_ref[hh] = jnp.zeros(q8_ref.shape[1:], F32)
            q8_ref[hh, 0:n_new, :] = qc_ref[0, hh]
            gate = _dot_nt(q8_ref[hh], km_ref[hh], precision=lax.Precision.HIGHEST)
            lane = lax.broadcasted_iota(jnp.int32, gate.shape, 1)
            picks = [jnp.zeros((gate.shape[0], 1), F32) for _ in range(n_sel)]
            for n in range(nblk):
                gn = gate[:, n:n + 1]
                beats = jnp.where(lane < n, jnp.where(gate >= gn, 1.0, 0.0), jnp.where(gate > gn, 1.0, 0.0))
                rank = jnp.sum(beats, axis=1, keepdims=True)
                for c in range(n_sel):
                    picks[c] = jnp.where(rank == float(c), float(n), picks[c])
            for j in range(n_new):
                for c in range(n_sel):
                    blk = picks[c][j, 0].astype(jnp.int32)
                    sel_ref[vs, sel_index(hh, j, c)] = blk
                    for pg in range(ppb):
                        v_copy(u, vs, hh, j, c, pg, blk).start()

    def attend(u):
        ks = lax.rem(u, n_slots_k)
        vs = lax.rem(u, n_slots_v)
        new_row = lax.broadcasted_iota(jnp.int32, (n_new, 1), 0)
        probs = []
        for hh in range(hp):
            q = qp_ref[0, hh]
            kn = kn_ref[0, hh]
            for j in range(n_new):
                qj = q[j:j + 1, :]
                s_sel = []
                for c in range(n_sel):
                    start = pl.multiple_of(sel_ref[vs, sel_index(hh, j, c)] * MOBA_BLOCK, MOBA_BLOCK)
                    kblk = kbuf[ks, hh, pl.ds(start, MOBA_BLOCK), :]
                    s_sel.append(jnp.sum(kblk * qj, axis=1, keepdims=True) * scale)
                s_new = jnp.sum(kn * qj, axis=1, keepdims=True) * scale
                s_new = jnp.where(new_row <= j, s_new, NEG_INF)
                m = jnp.max(s_new, axis=0, keepdims=True)
                for s in s_sel:
                    m = jnp.maximum(m, jnp.max(s, axis=0, keepdims=True))
                p_sel = [jnp.exp(s - m) for s in s_sel]
                p_new = jnp.exp(s_new - m)
                l = jnp.sum(p_new, axis=0, keepdims=True)
                for p in p_sel:
                    l = l + jnp.sum(p, axis=0, keepdims=True)
                probs.append((p_sel, p_new, l))
        for hh in range(hp):
            for j in range(n_new):
                for c in range(n_sel):
                    blk = sel_ref[vs, sel_index(hh, j, c)]
                    for pg in range(ppb):
                        v_copy(u, vs, hh, j, c, pg, blk).wait()
        for hh in range(hp):
            vn = vn_ref[0, hh]
            for j in range(n_new):
                p_sel, p_new, l = probs[hh * n_new + j]
                acc = jnp.sum(p_new * vn, axis=0, keepdims=True)
                for c in range(n_sel):
                    row0 = (j * n_sel + c) * MOBA_BLOCK
                    acc = acc + jnp.sum(p_sel[c] * vbuf[vs, hh, row0:row0 + MOBA_BLOCK, :], axis=0,
                                        keepdims=True)
                o_ref[0, hh, j:j + 1, :] = acc / l

    @pl.when(t == 0)
    def _():
        start_k(t)

    @pl.when(t + 1 < n_units)
    def _():
        start_k(t + 1)

    @pl.when(t < n_units)
    def _():
        select(t)

    @pl.when(t > 0)
    def _():
        attend(t - 1)


def _moba_sample(q, k_new, v_new, cache_k, cache_v, page_table):
    Bd, H, L, Dh = q.shape
    n_pages = page_table.shape[1]
    ppb = MOBA_BLOCK // PAGE_SIZE
    assert n_pages % ppb == 0, "past length must be a whole number of MoBA blocks"
    nblk = n_pages // ppb
    n_sel = min(MOBA_TOPK, nblk)
    assert n_sel > 0 and L <= V7X_SUBLANES
    hp = 2 if H % 2 == 0 else 1
    n_groups = H // hp
    n_units = Bd * n_groups

    def unit_spec(lag):
        def index_map(t, pt):
            u = jnp.clip(t - lag, 0, n_units - 1)
            return (u // n_groups, lax.rem(u, n_groups), 0, 0)
        return pl.BlockSpec((1, hp, L, Dh), index_map)

    any_spec = pl.BlockSpec(memory_space=pl.ANY)
    k_slots, v_slots = 3, 2
    kbuf_shape = (k_slots, hp, n_pages * PAGE_SIZE, Dh)
    vbuf_shape = (v_slots, hp, L * n_sel * MOBA_BLOCK, Dh)
    scratch_bytes = 4 * (k_slots * hp * n_pages * PAGE_SIZE * Dh + v_slots * hp * L * n_sel * MOBA_BLOCK * Dh)
    grid_spec = pltpu.PrefetchScalarGridSpec(
        num_scalar_prefetch=1,
        grid=(n_units + 1,),
        in_specs=[unit_spec(0), unit_spec(1), unit_spec(1), unit_spec(1), any_spec, any_spec],
        out_specs=unit_spec(1),
        scratch_shapes=[
            pltpu.VMEM(kbuf_shape, F32),
            pltpu.VMEM(vbuf_shape, F32),
            pltpu.VMEM((hp, nblk, Dh), F32),
            pltpu.VMEM((hp, V7X_SUBLANES, Dh), F32),
            pltpu.SMEM((v_slots, hp * L * n_sel), jnp.int32),
            pltpu.SemaphoreType.DMA((k_slots,)),
            pltpu.SemaphoreType.DMA((v_slots,)),
        ],
    )
    limit = int(min(scratch_bytes + VMEM_RESERVE_BYTES, V7X_VMEM_BYTES - VMEM_RESERVE_BYTES))
    return pl.pallas_call(
        functools.partial(_moba_sample_kernel, n_pages=n_pages, n_groups=n_groups, hp=hp, n_new=L,
                          n_units=n_units),
        grid_spec=grid_spec,
        out_shape=jax.ShapeDtypeStruct((Bd, H, L, Dh), F32),
        compiler_params=pltpu.CompilerParams(dimension_semantics=("arbitrary",), vmem_limit_bytes=limit),
        name="moba_sample",
    )(page_table, q, q, k_new, v_new, cache_k, cache_v)


def _outproj_kernel(pooled_ref, attn_ref, x_ref, w_ref, gpost_ref, gffn_ref, x1_ref, h2_ref, *, split):
    mix = _dot(pooled_ref[...], w_ref[0:split, :]) + _dot(attn_ref[...], w_ref[split:, :])
    x1 = x_ref[...] + _rms(mix, gpost_ref[...])
    x1_ref[...] = x1
    h2_ref[...] = _rms(x1, gffn_ref[...]).astype(BF16)


def _outproj(pooled, attn, x, w_bf, g_post, g_ffn, tm):
    rows, D = x.shape
    split = pooled.shape[1]
    wa = attn.shape[1]
    row_spec = lambda width: pl.BlockSpec((tm, width), lambda i: (i, 0))
    vec_spec = pl.BlockSpec((1, D), lambda i: (0, 0))
    block_bytes = 2 * tm * (split + wa) * 2 + 2 * tm * D * 4 + 2 * (split + wa) * D * 2 \
        + 2 * tm * D * 6 + 2 * tm * D * 4
    return pl.pallas_call(
        functools.partial(_outproj_kernel, split=split),
        grid=(rows // tm,),
        in_specs=[row_spec(split), row_spec(wa), row_spec(D),
                  pl.BlockSpec((split + wa, D), lambda i: (0, 0)), vec_spec, vec_spec],
        out_specs=[row_spec(D), row_spec(D)],
        out_shape=[jax.ShapeDtypeStruct((rows, D), F32), jax.ShapeDtypeStruct((rows, D), BF16)],
        compiler_params=_params(("arbitrary",), block_bytes),
        name="outproj",
    )(pooled, attn, x, w_bf, g_post.reshape(1, D), g_ffn.reshape(1, D))


def _gelu_tanh(c):
    return c * (0.5 * (1.0 + jnp.tanh(0.7978845608028654 * (c + 0.044715 * (c * c * c)))))


def _conv_gelu_gate(gt, prev8, up, cw_ref, cb_ref):
    pad = V7X_SUBLANES
    row8 = lax.broadcasted_iota(jnp.int32, prev8.shape, 0)

    def shifted(k):
        if k == 0:
            return gt
        rolled = pltpu.roll(gt, k, 0)
        head = jnp.where(row8 < k, pltpu.roll(prev8, k, 0), rolled[0:pad])
        return jnp.concatenate([head, rolled[pad:]], axis=0)

    c = cb_ref[...]
    for i in range(CONV_W):
        c = c + shifted(CONV_W - 1 - i) * cw_ref[i:i + 1, :]
    return _gelu_tanh(c) * up


def _ffn_up_kernel(h_ref, hs_ref, wg_ref, wu_ref, cw_ref, cb_ref, hist_ref,
                   f_ref, tail_ref, fs_ref, gts_ref, wgb_ref, wub_ref, halo_ref,
                   *, tm, sub, n_steps, n_seq):
    b = pl.program_id(1)
    s = pl.program_id(2)
    pad = V7X_SUBLANES

    @pl.when((b == 0) & (s == 0))
    def _():
        wgb_ref[...] = wg_ref[...].astype(BF16)
        wub_ref[...] = wu_ref[...].astype(BF16)
        hs = hs_ref[...]
        gt = _dot(hs, wgb_ref[...])
        up = _dot(hs, wub_ref[...])
        gts_ref[...] = gt
        ext = [hist_ref[i * n_seq:(i + 1) * n_seq, :] for i in range(CONV_W - 1)]
        ext += [gt[t * n_seq:(t + 1) * n_seq, :] for t in range(n_steps)]
        for t in range(n_steps):
            c = cb_ref[...]
            for i in range(CONV_W):
                c = c + ext[t + i] * cw_ref[i:i + 1, :]
            rows = slice(t * n_seq, (t + 1) * n_seq)
            fs_ref[rows, :] = (_gelu_tanh(c) * up[rows, :]).astype(BF16)

    @pl.when(s == 0)
    def _():
        halo_ref[...] = jnp.zeros(halo_ref.shape, F32)

    prev8 = halo_ref[...]
    for r in range(tm // sub):
        rows = slice(r * sub, (r + 1) * sub)
        hr = h_ref[0, rows, :]
        gt = _dot(hr, wgb_ref[...])
        up = _dot(hr, wub_ref[...])
        f_ref[0, rows, :] = _conv_gelu_gate(gt, prev8, up, cw_ref, cb_ref).astype(BF16)
        prev8 = gt[sub - pad:sub, :]
    halo_ref[...] = prev8
    tail_ref[0] = prev8


def _ffn_up(h2, h2_s, w_gate, w_up, conv_w, conv_b, hist_tm, n_steps, n_seq, tm, tn, sub):
    B, S, D = h2.shape
    rows_s = h2_s.shape[0]
    F = w_gate.shape[1]
    pad = V7X_SUBLANES
    w_spec = pl.BlockSpec((D, tn), lambda j, b, s: (0, j))
    col_spec = lambda r: pl.BlockSpec((r, tn), lambda j, b, s: (0, j))
    block_bytes = (2 * tm * D * 2 + 2 * rows_s * D * 2 + 2 * 2 * D * tn * 4 + 2 * D * tn * 2
                   + 2 * tm * tn * 2 + 2 * rows_s * tn * 6 + 8 * sub * tn * 4)
    return pl.pallas_call(
        functools.partial(_ffn_up_kernel, tm=tm, sub=sub, n_steps=n_steps, n_seq=n_seq),
        grid=(F // tn, B, S // tm),
        in_specs=[
            pl.BlockSpec((1, tm, D), lambda j, b, s: (b, s, 0)),
            pl.BlockSpec((rows_s, D), lambda j, b, s: (0, 0)),
            w_spec, w_spec,
            col_spec(CONV_W), col_spec(1), col_spec(hist_tm.shape[0]),
        ],
        out_specs=[pl.BlockSpec((1, tm, tn), lambda j, b, s: (b, s, j)),
                   pl.BlockSpec((1, pad, tn), lambda j, b, s: (b, 0, j)),
                   col_spec(rows_s), col_spec(rows_s)],
        out_shape=[jax.ShapeDtypeStruct((B, S, F), BF16), jax.ShapeDtypeStruct((B, pad, F), F32),
                   jax.ShapeDtypeStruct((rows_s, F), BF16), jax.ShapeDtypeStruct((rows_s, F), F32)],
        scratch_shapes=[pltpu.VMEM((D, tn), BF16), pltpu.VMEM((D, tn), BF16), pltpu.VMEM((pad, tn), F32)],
        compiler_params=_params(("arbitrary", "arbitrary", "arbitrary"), block_bytes),
        name="ffn_up",
    )(h2, h2_s, w_gate, w_up, conv_w, conv_b.reshape(1, F), hist_tm)


def _ffn_down_kernel(f_ref, w_ref, x1_ref, g_ref, y_ref):
    y_ref[...] = x1_ref[...] + _rms(_dot(f_ref[...], w_ref[...]), g_ref[...])


def _ffn_down(f, w_bf, x1, g, tm):
    rows, F = f.shape
    D = w_bf.shape[1]
    block_bytes = 2 * tm * F * 2 + F * D * 2 + 4 * tm * D * 4 + 2 * tm * D * 4
    return pl.pallas_call(
        _ffn_down_kernel,
        grid=(rows // tm,),
        in_specs=[pl.BlockSpec((tm, F), lambda i: (i, 0)),
                  pl.BlockSpec((F, D), lambda i: (0, 0), pipeline_mode=pl.Buffered(1)),
                  pl.BlockSpec((tm, D), lambda i: (i, 0)),
                  pl.BlockSpec((1, D), lambda i: (0, 0))],
        out_specs=pl.BlockSpec((tm, D), lambda i: (i, 0)),
        out_shape=jax.ShapeDtypeStruct((rows, D), F32),
        compiler_params=_resident_params(("arbitrary",), block_bytes),
        name="ffn_down",
    )(f, w_bf, x1, g.reshape(1, D))


def _pick_tile(n, target):
    t = min(n, target)
    while n % t:
        t //= 2
    return t


def _prompt_mixer(x, lp):
    B, S, D = x.shape
    tm = _pick_tile(S, 512)
    u, q, k, v = _inproj(x, lp["g_mix_pre"], lp["w_in"], jnp.arange(S), tm)
    pooled = _pool_prompt(u, lp["w_pool"], lp["pool_scale"], tm)
    attn = _moba_prompt(q, k, v)
    x1, h2 = _outproj(pooled.reshape(B * S, -1), attn.reshape(B * S, -1), x.reshape(B * S, D),
                      lp["w_out"], lp["g_mix_post"], lp["g_ffn_pre"], tm)
    return x1, h2, k, v, u


def _sample_mixer(x_tm, pool_hist, cache_k, cache_v, page_table, lp, n_seq, n_steps):
    rows = x_tm.shape[0]
    past_len = page_table.shape[1] * PAGE_SIZE
    pos = past_len + jnp.repeat(jnp.arange(n_steps), n_seq)
    u, q, k, v = _inproj(x_tm[None], lp["g_mix_pre"], lp["w_in"], pos, rows)

    def to_seq_major(t):
        H = t.shape[0]
        return t.reshape(H, n_steps, n_seq, HEAD_DIM).transpose(2, 0, 1, 3)

    qs, ks, vs = to_seq_major(q[0]), to_seq_major(k[0]), to_seq_major(v[0])
    pooled = _pool_sample(u[0], _seq_to_rows(pool_hist), lp["w_pool"], lp["pool_scale"],
                          n_steps, n_seq, past_len)
    attn = _moba_sample(qs, ks, vs, cache_k, cache_v, page_table)
    attn_tm = attn.transpose(2, 0, 1, 3).reshape(rows, -1).astype(BF16)
    x1, h2 = _outproj(pooled, attn_tm, x_tm, lp["w_out"], lp["g_mix_post"], lp["g_ffn_pre"], rows)
    return x1, h2, ks, vs, u[0]


def _rows_to_seq(t, n_steps, n_seq):
    return t.reshape(n_steps, n_seq, t.shape[-1]).transpose(1, 0, 2)


def _seq_to_rows(t):
    return t.transpose(1, 0, 2).reshape(-1, t.shape[-1])


def _layer(xp, xs_tm, pool_hist, conv_hist, cache_k, cache_v, page_table, lp, n_seq, n_steps):
    B, S, D = xp.shape
    F = lp["w_gate"].shape[1]
    x1p, h2p, kp, vp, up = _prompt_mixer(xp, lp)
    x1s, h2s, ks, vs, us = _sample_mixer(xs_tm, pool_hist, cache_k, cache_v, page_table, lp, n_seq, n_steps)
    tm = _pick_tile(S, 1024)
    fp, tail, fs, gts = _ffn_up(h2p.reshape(B, S, D), h2s, lp["w_gate"], lp["w_up"], lp["conv_w"],
                                lp["conv_b"], _seq_to_rows(conv_hist), n_steps, n_seq,
                                tm, _pick_tile(F, 512), _pick_tile(tm, 256))
    yp = _ffn_down(fp.reshape(B * S, F), lp["w_down"], x1p, lp["g_ffn_post"], _pick_tile(B * S, 256))
    ys = _ffn_down(fs, lp["w_down"], x1s, lp["g_ffn_post"], xs_tm.shape[0])
    pool_p = up[:, S - POOL_HIST:, :]
    conv_p = tail[:, V7X_SUBLANES - (CONV_W - 1):, :]
    pool_s = jnp.concatenate([pool_hist, _rows_to_seq(us, n_steps, n_seq)], axis=1)[:, n_steps:]
    conv_s = jnp.concatenate([conv_hist, _rows_to_seq(gts, n_steps, n_seq)], axis=1)[:, n_steps:]
    return yp.reshape(B, S, D), ys, (kp, vp, ks, vs, pool_p, pool_s, conv_p, conv_s)


def kernel(x_prompt, x_sample, cache_k, cache_v, state_pool, state_conv, page_table,
           w_in, w_pool, pool_scale, w_out, g_mix_pre, g_mix_post,
           w_gate, w_up, conv_w, conv_b, w_down, g_ffn_pre, g_ffn_post):
    depth = w_in.shape[0]
    n_seq, n_steps, D = x_sample.shape
    assert w_in.shape[2] == 4 * pool_scale.shape[1], "pooling and attention widths must match"
    yp = x_prompt
    ys = x_sample.transpose(1, 0, 2).reshape(n_steps * n_seq, D)
    outs = [[] for _ in range(8)]
    for l in range(depth):
        lp = {"w_in": w_in[l].astype(BF16), "w_pool": w_pool[l].astype(BF16), "pool_scale": pool_scale[l],
              "w_out": w_out[l].astype(BF16), "g_mix_pre": g_mix_pre[l], "g_mix_post": g_mix_post[l],
              "w_gate": w_gate[l], "w_up": w_up[l], "conv_w": conv_w[l],
              "conv_b": conv_b[l], "w_down": w_down[l].astype(BF16), "g_ffn_pre": g_ffn_pre[l],
              "g_ffn_post": g_ffn_post[l]}
        yp, ys, states = _layer(yp, ys, state_pool[l], state_conv[l], cache_k[l], cache_v[l],
                                page_table, lp, n_seq, n_steps)
        for lst, val in zip(outs, states):
            lst.append(val)
    y_sample = ys.reshape(n_steps, n_seq, D).transpose(1, 0, 2)
    return (yp, y_sample) + tuple(jnp.stack(o) for o in outs)
```

```python
import functools

import jax
import jax.numpy as jnp
from jax import lax
from jax.experimental import pallas as pl
from jax.experimental.pallas import tpu as pltpu

F32 = jnp.float32
BF16 = jnp.bfloat16

POOL_WINDOWS = (2, 4, 8, 16)
POOL_HIST = max(POOL_WINDOWS) - 1
HEAD_DIM = 128
ROT_DIM = HEAD_DIM // 4
ROPE_THETA = 500000.0
MOBA_BLOCK = 256
MOBA_TOPK = 3
QUERY_BLOCKS_PER_DOT = 2
PAGE_SIZE = 128
CONV_W = 3
EPS = 1e-6
NEG_INF = float("-inf")
LOG2E = 1.4426950408889634

V7X_SUBLANES = 8
V7X_VMEM_BYTES = 64 * 1024 * 1024
VMEM_RESERVE_BYTES = 4 * 1024 * 1024


def _vmem_limit(block_bytes):
    return int(min(2 * block_bytes, V7X_VMEM_BYTES - VMEM_RESERVE_BYTES))


def _params(semantics, block_bytes):
    return pltpu.CompilerParams(dimension_semantics=semantics,
                                vmem_limit_bytes=_vmem_limit(block_bytes))


def _resident_params(semantics, block_bytes):
    limit = int(min(block_bytes + VMEM_RESERVE_BYTES, V7X_VMEM_BYTES - VMEM_RESERVE_BYTES))
    return pltpu.CompilerParams(dimension_semantics=semantics, vmem_limit_bytes=limit)


def _rms(x, g):
    return x * lax.rsqrt(jnp.mean(x * x, axis=-1, keepdims=True) + EPS) * g


def _dot(a, b):
    return jnp.dot(a, b, preferred_element_type=F32)


def _dot_nt(a, b, precision=None):
    return lax.dot_general(a, b, (((1,), (1,)), ((), ())), precision=precision,
                           preferred_element_type=F32)


def _rope_tables(pos):
    half = ROT_DIM // 2
    inv = ROPE_THETA ** (-jnp.arange(half, dtype=F32) * (2.0 / ROT_DIM))
    ang = pos.astype(F32)[:, None] * inv[None, :]
    cos, sin = jnp.cos(ang), jnp.sin(ang)
    n = pos.shape[0]
    c = jnp.concatenate([cos, cos, jnp.ones((n, HEAD_DIM - ROT_DIM), F32)], axis=1)
    s_lo = jnp.concatenate([-sin, jnp.zeros((n, HEAD_DIM - half), F32)], axis=1)
    s_hi = jnp.concatenate([jnp.zeros((n, half), F32), sin,
                            jnp.zeros((n, HEAD_DIM - ROT_DIM), F32)], axis=1)
    return c, s_lo, s_hi


def _inproj_kernel(x_ref, g_ref, w_ref, c_ref, slo_ref, shi_ref,
                   u_ref, q_ref, k_ref, v_ref, hn_ref, *, n_heads, width):
    hn_ref[...] = _rms(x_ref[0], g_ref[...]).astype(BF16)
    pair = 2 * HEAD_DIM

    def rope(h):
        return (h * c_ref[...]
                + pltpu.roll(h, HEAD_DIM - ROT_DIM // 2, 1) * slo_ref[...]
                + pltpu.roll(h, ROT_DIM // 2, 1) * shi_ref[...])

    for p in range(width // pair):
        u_ref[0, :, p * pair:(p + 1) * pair] = _dot(hn_ref[...], w_ref[:, p * pair:(p + 1) * pair])
    for out_ref, part, rotary in ((q_ref, 1, True), (k_ref, 2, True), (v_ref, 3, False)):
        for p in range(n_heads // 2):
            col = part * width + p * pair
            res = _dot(hn_ref[...], w_ref[:, col:col + pair])
            for half in range(2):
                h = res[:, half * HEAD_DIM:(half + 1) * HEAD_DIM]
                out_ref[0, 2 * p + half] = rope(h) if rotary else h


def _inproj(x, g, w_bf, pos, tm):
    B, S, D = x.shape
    width = w_bf.shape[1] // 4
    n_heads = width // HEAD_DIM
    assert n_heads % 2 == 0
    c, s_lo, s_hi = _rope_tables(pos)
    tab_spec = pl.BlockSpec((tm, HEAD_DIM), lambda b, s: (s, 0))
    hm_spec = pl.BlockSpec((1, n_heads, tm, HEAD_DIM), lambda b, s: (b, 0, s, 0))
    hm_shape = jax.ShapeDtypeStruct((B, n_heads, S, HEAD_DIM), F32)
    block_bytes = 2 * tm * D * 4 + tm * D * 2 + D * 4 * width * 2 + 2 * 4 * tm * width * 4
    return pl.pallas_call(
        functools.partial(_inproj_kernel, n_heads=n_heads, width=width),
        grid=(B, S // tm),
        in_specs=[
            pl.BlockSpec((1, tm, D), lambda b, s: (b, s, 0)),
            pl.BlockSpec((1, D), lambda b, s: (0, 0)),
            pl.BlockSpec((D, 4 * width), lambda b, s: (0, 0), pipeline_mode=pl.Buffered(1)),
            tab_spec, tab_spec, tab_spec,
        ],
        out_specs=[pl.BlockSpec((1, tm, width), lambda b, s: (b, s, 0)), hm_spec, hm_spec, hm_spec],
        out_shape=[jax.ShapeDtypeStruct((B, S, width), F32), hm_shape, hm_shape, hm_shape],
        scratch_shapes=[pltpu.VMEM((tm, D), BF16)],
        compiler_params=_resident_params(("arbitrary", "arbitrary"), block_bytes),
        name="inproj",
    )(x, g.reshape(1, D), w_bf, c, s_lo, s_hi)


def _pool_kernel(u_ref, halo_ref, w_ref, scale_ref, o_ref, ext_ref, *, ts, group):
    s = pl.program_id(1)
    hist = POOL_HIST + 1
    u = u_ref[0]
    ext_ref[hist:, :] = u

    @pl.when(s == 0)
    def _():
        ext_ref[0:hist, :] = jnp.zeros((hist, ext_ref.shape[1]), F32)

    @pl.when(s > 0)
    def _():
        ext_ref[0:hist, :] = halo_ref[0]

    pos = s * ts + lax.broadcasted_iota(jnp.int32, (ts, 1), 0)
    for gi, w in enumerate(POOL_WINDOWS):
        cols = slice(gi * group, (gi + 1) * group)
        acc = u[:, cols]
        for back in range(1, w):
            acc = acc + ext_ref[hist - back:hist - back + ts, cols]
        cnt = jnp.minimum(pos + 1, w).astype(F32)
        d = (acc / cnt - u[:, cols]).astype(BF16)
        y = _dot(d, w_ref[gi]) * scale_ref[:, cols]
        o_ref[0, :, cols] = y.astype(BF16)


def _pool_prompt(u, w_pool_bf, scale, ts):
    B, S, C = u.shape
    n_groups, group, _ = w_pool_bf.shape
    hist = POOL_HIST + 1
    per = ts // hist
    block_bytes = 2 * ts * C * 4 + 2 * hist * C * 4 + 2 * n_groups * group * group * 2 \
        + 2 * ts * C * 2 + (ts + hist) * C * 4 + 4 * ts * group * 4
    return pl.pallas_call(
        functools.partial(_pool_kernel, ts=ts, group=group),
        grid=(B, S // ts),
        in_specs=[
            pl.BlockSpec((1, ts, C), lambda b, s: (b, s, 0)),
            pl.BlockSpec((1, hist, C), lambda b, s: (b, jnp.maximum(s * per - 1, 0), 0)),
            pl.BlockSpec((n_groups, group, group), lambda b, s: (0, 0, 0)),
            pl.BlockSpec((1, C), lambda b, s: (0, 0)),
        ],
        out_specs=pl.BlockSpec((1, ts, C), lambda b, s: (b, s, 0)),
        out_shape=jax.ShapeDtypeStruct((B, S, C), BF16),
        scratch_shapes=[pltpu.VMEM((ts + hist, C), F32)],
        compiler_params=_params(("arbitrary", "arbitrary"), block_bytes),
        name="pool_prompt",
    )(u, u, w_pool_bf, scale.reshape(1, C))


def _pool_sample_kernel(u_ref, hist_ref, w_ref, scale_ref, o_ref, *, n_steps, n_seq, group, past_len):
    def ext(i):
        if i < POOL_HIST:
            return hist_ref[i * n_seq:(i + 1) * n_seq, :]
        return u_ref[(i - POOL_HIST) * n_seq:(i - POOL_HIST + 1) * n_seq, :]

    for gi, w in enumerate(POOL_WINDOWS):
        cols = slice(gi * group, (gi + 1) * group)
        ds = []
        for t in range(n_steps):
            cur = ext(POOL_HIST + t)[:, cols]
            acc = cur
            for back in range(1, w):
                acc = acc + ext(POOL_HIST + t - back)[:, cols]
            cnt = float(min(past_len + t + 1, w))
            ds.append((acc / cnt - cur).astype(BF16))
        d = jnp.concatenate(ds, axis=0)
        y = _dot(d, w_ref[gi]) * scale_ref[:, cols]
        o_ref[:, cols] = y.astype(BF16)


def _pool_sample(u_tm, hist_tm, w_pool_bf, scale, n_steps, n_seq, past_len):
    rows, C = u_tm.shape
    n_groups, group, _ = w_pool_bf.shape
    block_bytes = 2 * (rows + hist_tm.shape[0]) * C * 4 + 2 * n_groups * group * group * 2 + 2 * rows * C * 2
    return pl.pallas_call(
        functools.partial(_pool_sample_kernel, n_steps=n_steps, n_seq=n_seq, group=group,
                          past_len=past_len),
        out_shape=jax.ShapeDtypeStruct((rows, C), BF16),
        compiler_params=pltpu.CompilerParams(vmem_limit_bytes=_vmem_limit(block_bytes)),
        name="pool_sample",
    )(u_tm, hist_tm, w_pool_bf, scale.reshape(1, C))


class _PerHead:
    def __init__(self, refs):
        self.refs = refs

    def __getitem__(self, idx):
        return self.refs[idx[0]][idx[1:]] if isinstance(idx, tuple) else self.refs[idx][...]

    def __setitem__(self, idx, value):
        self.refs[idx[0]][idx[1:]] = value


def _moba_prompt_kernel(q_ref, k_ref, v_ref, o_ref, *scratch, seq, hp):
    blk = MOBA_BLOCK
    nb = seq // blk
    n_sel = min(MOBA_TOPK, nb - 1)
    scale = HEAD_DIM ** -0.5
    kb_ref, vt_ref, km_ref, s_ref, p_ref = (_PerHead(scratch[kind * hp:(kind + 1) * hp]) for kind in range(5))

    for hh in range(hp):
        for n in range(nb):
            rows = slice(n * blk, (n + 1) * blk)
            kn = k_ref[0, hh, rows, :]
            kb_ref[hh, rows, :] = kn.astype(BF16)
            km_ref[hh, n:n + 1, :] = jnp.sum(kn, axis=0, keepdims=True) * (1.0 / blk)
            vt_ref[hh, :, rows] = v_ref[0, hh, rows, :].T.astype(BF16)

    blk_id = lax.broadcasted_iota(jnp.int32, (nb, blk), 0)
    key_i = lax.broadcasted_iota(jnp.int32, (blk, blk), 0)
    qry_i = lax.broadcasted_iota(jnp.int32, (blk, blk), 1)
    causal_bias = jnp.where(key_i <= qry_i, 0.0, NEG_INF).astype(F32)

    def scores(hh, qblocks):
        first, last = qblocks[0], qblocks[-1]
        nk = (last + 1) * blk
        width = len(qblocks) * blk
        qg = q_ref[0, hh, first * blk:nk, :]
        s_ref[hh, 0:nk, 0:width] = _dot_nt(kb_ref[hh, 0:nk, :], (qg * (scale * LOG2E)).astype(BF16))
        return _dot_nt(km_ref[hh], qg, precision=lax.Precision.HIGHEST) if last > 0 else None

    def softmax(hh, qblocks, gate):
        last = qblocks[-1]
        sums = []
        for a, i in enumerate(qblocks):
            cols = slice(a * blk, (a + 1) * blk)
            own = slice(i * blk, (i + 1) * blk)
            biases = []
            if i > 0:
                gate_a = gate[:, cols]
                valid = jnp.where(blk_id < i, 1.0, 0.0)
                for n in range(i):
                    gn = gate_a[n:n + 1, :]
                    beats = jnp.where(blk_id < n, jnp.where(gate_a >= gn, 1.0, 0.0),
                                      jnp.where(gate_a > gn, 1.0, 0.0))
                    rank = jnp.sum(beats * valid, axis=0, keepdims=True)
                    biases.append(jnp.where(rank < n_sel, 0.0, NEG_INF).astype(F32))

            s_own = s_ref[hh, own, cols] + causal_bias
            m = jnp.max(s_own, axis=0, keepdims=True)
            for n in range(i):
                m = jnp.maximum(m, jnp.max(s_ref[hh, n * blk:(n + 1) * blk, cols], axis=0, keepdims=True)
                                + biases[n])

            p = jnp.exp2(s_own - m)
            l = jnp.sum(p, axis=0, keepdims=True)
            p_ref[hh, own, cols] = p.astype(BF16)
            for n in range(i):
                rows = slice(n * blk, (n + 1) * blk)
                p = jnp.exp2(s_ref[hh, rows, cols] + (biases[n] - m))
                l = l + jnp.sum(p, axis=0, keepdims=True)
                p_ref[hh, rows, cols] = p.astype(BF16)
            for n in range(i + 1, last + 1):
                p_ref[hh, n * blk:(n + 1) * blk, cols] = jnp.zeros((blk, blk), BF16)
            sums.append(l)
        return sums

    def output(hh, qblocks, sums):
        nk = (qblocks[-1] + 1) * blk
        width = len(qblocks) * blk
        o_t = _dot(vt_ref[hh, :, 0:nk], p_ref[hh, 0:nk, 0:width])
        for a, i in enumerate(qblocks):
            o_a = o_t[:, a * blk:(a + 1) * blk] / sums[a]
            o_ref[0, i * blk:(i + 1) * blk, hh * HEAD_DIM:(hh + 1) * HEAD_DIM] = o_a.T.astype(BF16)

    items = [(hh, list(range(first, min(first + QUERY_BLOCKS_PER_DOT, nb))))
             for first in range(0, nb, QUERY_BLOCKS_PER_DOT) for hh in range(hp)]
    gate = scores(*items[0])
    for cur, nxt in zip(items, items[1:] + [None]):
        next_gate = scores(*nxt) if nxt is not None else None
        output(*cur, softmax(*cur, gate))
        gate = next_gate


def _moba_prompt(q, k, v):
    B, H, S, Dh = q.shape
    nb = S // MOBA_BLOCK
    hp = 2 if H % 2 == 0 else 1
    width = QUERY_BLOCKS_PER_DOT * MOBA_BLOCK
    in_spec = pl.BlockSpec((1, hp, S, Dh), lambda b, g: (b, g, 0, 0))
    block_bytes = hp * (2 * 3 * S * Dh * 4 + 2 * S * Dh * 2 + 2 * S * Dh * 2 + S * width * 6)
    return pl.pallas_call(
        functools.partial(_moba_prompt_kernel, seq=S, hp=hp),
        grid=(B, H // hp),
        in_specs=[in_spec, in_spec, in_spec],
        out_specs=pl.BlockSpec((1, S, hp * Dh), lambda b, g: (b, 0, g)),
        out_shape=jax.ShapeDtypeStruct((B, S, H * Dh), BF16),
        scratch_shapes=(
            [pltpu.VMEM((S, Dh), BF16)] * hp
            + [pltpu.VMEM((Dh, S), BF16)] * hp
            + [pltpu.VMEM((nb, Dh), F32)] * hp
            + [pltpu.VMEM((S, width), F32)] * hp
            + [pltpu.VMEM((S, width), BF16)] * hp
        ),
        compiler_params=_params(("arbitrary", "arbitrary"), block_bytes),
        name="moba_prompt",
    )(q, k, v)


class _PagedAttention:
    def __init__(self, pt_ref, qc_ref, qp_ref, kn_ref, vn_ref, ck_ref, cv_ref, o_ref,
                 kbuf, ksel, vbuf, km_ref, q8_ref, sel_ref, ksem, vsem, *, n_pages, n_groups, hp, n_new):
        self.__dict__.update(locals())
        self.ppb = MOBA_BLOCK // PAGE_SIZE
        self.nblk = n_pages // self.ppb
        self.n_sel = min(MOBA_TOPK, self.nblk)

    def unit_bh(self, u):
        return u // self.n_groups, lax.rem(u, self.n_groups) * self.hp

    def k_copy(self, u, page):
        bb, h0 = self.unit_bh(u)
        sl = lax.rem(u, 2)
        return pltpu.make_async_copy(self.ck_ref.at[self.pt_ref[bb, page], pl.ds(h0, self.hp)],
                                     self.kbuf.at[sl, :, pl.ds(page * PAGE_SIZE, PAGE_SIZE), :],
                                     self.ksem.at[sl])

    def start_k(self, u):
        def body(page, carry):
            self.k_copy(u, page).start()
            return carry
        lax.fori_loop(0, self.n_pages, body, 0, unroll=8)

    def wait_k(self, u):
        for page in range(self.n_pages):
            self.k_copy(u, page).wait()

    def sel_index(self, hh, j, c):
        return (hh * self.n_new + j) * self.n_sel + c

    def v_copies(self, u, heads):
        bb, h0 = self.unit_bh(u)
        sl = lax.rem(u, 2)
        for hh in heads:
            for j in range(self.n_new):
                for c in range(self.n_sel):
                    blk = self.sel_ref[sl, self.sel_index(hh, j, c)]
                    for pg in range(self.ppb):
                        page = self.pt_ref[bb, blk * self.ppb + pg]
                        dst = ((j * self.n_sel + c) * self.ppb + pg) * PAGE_SIZE
                        yield pltpu.make_async_copy(self.cv_ref.at[page, h0 + hh],
                                                    self.vbuf.at[sl, hh, pl.ds(dst, PAGE_SIZE), :],
                                                    self.vsem.at[sl])

    def start_v(self, u, heads):
        for cp in self.v_copies(u, heads):
            cp.start()

    def wait_v(self, u):
        for cp in self.v_copies(u, range(self.hp)):
            cp.wait()

    def block_means(self, u, heads):
        sl = lax.rem(u, 2)
        for hh in heads:
            for n in range(self.nblk):
                kn_blk = self.kbuf[sl, hh, n * MOBA_BLOCK:(n + 1) * MOBA_BLOCK, :]
                self.km_ref[hh, n:n + 1, :] = jnp.sum(kn_blk, axis=0, keepdims=True) * (1.0 / MOBA_BLOCK)

    def select(self, u, heads):
        sl = lax.rem(u, 2)
        for hh in heads:
            self.q8_ref[hh] = jnp.zeros(self.q8_ref.shape[1:], F32)
            self.q8_ref[hh, 0:self.n_new, :] = self.qc_ref[0, hh]
            gate = _dot_nt(self.q8_ref[hh], self.km_ref[hh], precision=lax.Precision.HIGHEST)
            lane = lax.broadcasted_iota(jnp.int32, gate.shape, 1)
            picks = [jnp.zeros((gate.shape[0], 1), F32) for _ in range(self.n_sel)]
            for n in range(self.nblk):
                gn = gate[:, n:n + 1]
                beats = jnp.where(lane < n, jnp.where(gate >= gn, 1.0, 0.0), jnp.where(gate > gn, 1.0, 0.0))
                rank = jnp.sum(beats, axis=1, keepdims=True)
                for c in range(self.n_sel):
                    picks[c] = jnp.where(rank == float(c), float(n), picks[c])
            for j in range(self.n_new):
                for c in range(self.n_sel):
                    blk = picks[c][j, 0].astype(jnp.int32)
                    self.sel_ref[sl, self.sel_index(hh, j, c)] = blk
                    row0 = (j * self.n_sel + c) * MOBA_BLOCK
                    start = pl.multiple_of(blk * MOBA_BLOCK, MOBA_BLOCK)
                    self.ksel[sl, hh, row0:row0 + MOBA_BLOCK, :] = self.kbuf[sl, hh, pl.ds(start, MOBA_BLOCK), :]

    def attend(self, u, heads):
        sl = lax.rem(u, 2)
        scale = HEAD_DIM ** -0.5
        new_row = lax.broadcasted_iota(jnp.int32, (self.n_new, 1), 0)
        for hh in heads:
            q = self.qp_ref[0, hh]
            kn = self.kn_ref[0, hh]
            vn = self.vn_ref[0, hh]
            for j in range(self.n_new):
                qj = q[j:j + 1, :]
                rows = [slice((j * self.n_sel + c) * MOBA_BLOCK, (j * self.n_sel + c + 1) * MOBA_BLOCK)
                        for c in range(self.n_sel)]
                s_sel = [jnp.sum(self.ksel[sl, hh, r, :] * qj, axis=1, keepdims=True) * scale for r in rows]
                s_new = jnp.sum(kn * qj, axis=1, keepdims=True) * scale
                s_new = jnp.where(new_row <= j, s_new, NEG_INF)
                m = jnp.max(s_new, axis=0, keepdims=True)
                for s in s_sel:
                    m = jnp.maximum(m, jnp.max(s, axis=0, keepdims=True))
                p_new = jnp.exp(s_new - m)
                l = jnp.sum(p_new, axis=0, keepdims=True)
                acc = jnp.sum(p_new * vn, axis=0, keepdims=True)
                for s, r in zip(s_sel, rows):
                    p = jnp.exp(s - m)
                    l = l + jnp.sum(p, axis=0, keepdims=True)
                    acc = acc + jnp.sum(p * self.vbuf[sl, hh, r, :], axis=0, keepdims=True)
                self.o_ref[0, hh, j:j + 1, :] = acc / l


def _paged_attention_scratch(n_pages, hp, n_new):
    nblk = n_pages // (MOBA_BLOCK // PAGE_SIZE)
    n_sel = min(MOBA_TOPK, nblk)
    picked_rows = n_new * n_sel * MOBA_BLOCK
    shapes = [
        pltpu.VMEM((2, hp, n_pages * PAGE_SIZE, HEAD_DIM), F32),
        pltpu.VMEM((2, hp, picked_rows, HEAD_DIM), F32),
        pltpu.VMEM((2, hp, picked_rows, HEAD_DIM), F32),
        pltpu.VMEM((hp, nblk, HEAD_DIM), F32),
        pltpu.VMEM((hp, V7X_SUBLANES, HEAD_DIM), F32),
        pltpu.SMEM((2, hp * n_new * n_sel), jnp.int32),
        pltpu.SemaphoreType.DMA((2,)),
        pltpu.SemaphoreType.DMA((2,)),
    ]
    n_bytes = 4 * HEAD_DIM * 2 * hp * (n_pages * PAGE_SIZE + 2 * picked_rows)
    return shapes, n_bytes


def _outproj_kernel(pooled_ref, attn_ref, x_ref, w_ref, gpost_ref, gffn_ref, x1_ref, h2_ref, *, split):
    mix = _dot(pooled_ref[...], w_ref[0:split, :]) + _dot(attn_ref[...], w_ref[split:, :])
    x1 = x_ref[...] + _rms(mix, gpost_ref[...])
    x1_ref[...] = x1
    h2_ref[...] = _rms(x1, gffn_ref[...]).astype(BF16)


def _outproj(pooled, attn, x, w_bf, g_post, g_ffn, tm):
    rows, D = x.shape
    split = pooled.shape[1]
    wa = attn.shape[1]
    row_spec = lambda width: pl.BlockSpec((tm, width), lambda i: (i, 0))
    vec_spec = pl.BlockSpec((1, D), lambda i: (0, 0))
    block_bytes = 2 * tm * (split + wa) * 2 + 2 * tm * D * 4 + 2 * (split + wa) * D * 2 \
        + 2 * tm * D * 6 + 2 * tm * D * 4
    return pl.pallas_call(
        functools.partial(_outproj_kernel, split=split),
        grid=(rows // tm,),
        in_specs=[row_spec(split), row_spec(wa), row_spec(D),
                  pl.BlockSpec((split + wa, D), lambda i: (0, 0)), vec_spec, vec_spec],
        out_specs=[row_spec(D), row_spec(D)],
        out_shape=[jax.ShapeDtypeStruct((rows, D), F32), jax.ShapeDtypeStruct((rows, D), BF16)],
        compiler_params=_params(("arbitrary",), block_bytes),
        name="outproj",
    )(pooled, attn, x, w_bf, g_post.reshape(1, D), g_ffn.reshape(1, D))


def _gelu_tanh(c):
    return c * (0.5 * (1.0 + jnp.tanh(0.7978845608028654 * (c + 0.044715 * (c * c * c)))))


def _conv_gelu_gate(gt, prev8, up, cw_ref, cb_ref):
    pad = V7X_SUBLANES
    row8 = lax.broadcasted_iota(jnp.int32, prev8.shape, 0)

    def shifted(k):
        if k == 0:
            return gt
        rolled = pltpu.roll(gt, k, 0)
        head = jnp.where(row8 < k, pltpu.roll(prev8, k, 0), rolled[0:pad])
        return jnp.concatenate([head, rolled[pad:]], axis=0)

    c = cb_ref[...]
    for i in range(CONV_W):
        c = c + shifted(CONV_W - 1 - i) * cw_ref[i:i + 1, :]
    return _gelu_tanh(c) * up


def _ffn_up_kernel(pt_ref, h_ref, wg_ref, wu_ref, cw_ref, cb_ref, qc_ref, qp_ref, kn_ref, vn_ref, ck_ref, cv_ref,
                   f_ref, tail_ref, o_ref, wgb_ref, wub_ref, halo_ref, *attn_scratch,
                   tm, sub, n_b, n_s, n_units, attn_params):
    j, b, s = pl.program_id(0), pl.program_id(1), pl.program_id(2)
    t = (j * n_b + b) * n_s + s
    pad = V7X_SUBLANES
    last_sub = tm // sub - 1
    attn = _PagedAttention(pt_ref, qc_ref, qp_ref, kn_ref, vn_ref, ck_ref, cv_ref, o_ref,
                           *attn_scratch, **attn_params)

    @pl.when((b == 0) & (s == 0))
    def _():
        wgb_ref[...] = wg_ref[...].astype(BF16)
        wub_ref[...] = wu_ref[...].astype(BF16)

    @pl.when(s == 0)
    def _():
        halo_ref[...] = jnp.zeros(halo_ref.shape, F32)

    @pl.when(t == 0)
    def _():
        attn.start_k(t)

    @pl.when(t < n_units)
    def _():
        attn.wait_k(t)

    @pl.when((t >= 1) & (t <= n_units))
    def _():
        attn.wait_v(t - 1)

    @pl.when(t + 1 < n_units)
    def _():
        attn.start_k(t + 1)

    def gate_up(before, after):
        prev8 = halo_ref[...]
        for r in range(tm // sub):
            rows = slice(r * sub, (r + 1) * sub)
            for stage in before.get(r, ()):
                stage()
            hr = h_ref[0, rows, :]
            gt = _dot(hr, wgb_ref[...])
            up = _dot(hr, wub_ref[...])
            for stage in after.get(r, ()):
                stage()
            f_ref[0, rows, :] = _conv_gelu_gate(gt, prev8, up, cw_ref, cb_ref).astype(BF16)
            prev8 = gt[sub - pad:sub, :]
        halo_ref[...] = prev8
        tail_ref[0] = prev8

    heads = tuple(range(attn.hp))

    def means():
        attn.block_means(t, heads)

    def select_and_fetch():
        attn.select(t, heads)
        attn.start_v(t, heads)

    def attend_previous():
        attn.attend(t - 1, heads)

    @pl.when(t == 0)
    def _():
        gate_up({0: [means]}, {0: [select_and_fetch]})

    @pl.when((t >= 1) & (t < n_units))
    def _():
        before = {0: [means]}
        before.setdefault(last_sub, []).append(attend_previous)
        gate_up(before, {0: [select_and_fetch]})

    @pl.when(t == n_units)
    def _():
        gate_up({last_sub: [attend_previous]}, {})

    @pl.when(t > n_units)
    def _():
        gate_up({}, {})


def _ffn_up(h2, w_gate, w_up, conv_w, conv_b, q, k_new, v_new, cache_k, cache_v, page_table, tm, tn, sub):
    B, S, D = h2.shape
    F = w_gate.shape[1]
    Bd, H, L, Dh = q.shape
    n_pages = page_table.shape[1]
    ppb = MOBA_BLOCK // PAGE_SIZE
    assert n_pages % ppb == 0, "past length must be a whole number of MoBA blocks"
    assert n_pages >= ppb and L <= V7X_SUBLANES
    hp = 2 if H % 2 == 0 else 1
    n_groups = H // hp
    n_units = Bd * n_groups
    n_b, n_s = B, S // tm
    assert (F // tn) * n_b * n_s > n_units, "not enough grid steps to host the sample attention"
    pad = V7X_SUBLANES

    def unit_spec(lag):
        def index_map(j, b, s, pt):
            u = jnp.clip((j * n_b + b) * n_s + s - lag, 0, n_units - 1)
            return (u // n_groups, lax.rem(u, n_groups), 0, 0)
        return pl.BlockSpec((1, hp, L, Dh), index_map)

    w_spec = pl.BlockSpec((D, tn), lambda j, b, s, pt: (0, j))
    any_spec = pl.BlockSpec(memory_space=pl.ANY)
    attn_scratch, attn_bytes = _paged_attention_scratch(n_pages, hp, L)
    block_bytes = (2 * tm * D * 2 + 2 * 2 * D * tn * 4 + 2 * D * tn * 2 + 2 * tm * tn * 2 + 8 * sub * tn * 4
                   + attn_bytes)
    grid_spec = pltpu.PrefetchScalarGridSpec(
        num_scalar_prefetch=1,
        grid=(F // tn, n_b, n_s),
        in_specs=[
            pl.BlockSpec((1, tm, D), lambda j, b, s, pt: (b, s, 0)),
            w_spec, w_spec,
            pl.BlockSpec((CONV_W, tn), lambda j, b, s, pt: (0, j)),
            pl.BlockSpec((1, tn), lambda j, b, s, pt: (0, j)),
            unit_spec(0), unit_spec(1), unit_spec(1), unit_spec(1), any_spec, any_spec,
        ],
        out_specs=[pl.BlockSpec((1, tm, tn), lambda j, b, s, pt: (b, s, j)),
                   pl.BlockSpec((1, pad, tn), lambda j, b, s, pt: (b, 0, j)),
                   unit_spec(1)],
        scratch_shapes=[pltpu.VMEM((D, tn), BF16), pltpu.VMEM((D, tn), BF16), pltpu.VMEM((pad, tn), F32)]
        + attn_scratch,
    )
    return pl.pallas_call(
        functools.partial(_ffn_up_kernel, tm=tm, sub=sub, n_b=n_b, n_s=n_s, n_units=n_units,
                          attn_params=dict(n_pages=n_pages, n_groups=n_groups, hp=hp, n_new=L)),
        grid_spec=grid_spec,
        out_shape=[jax.ShapeDtypeStruct((B, S, F), BF16), jax.ShapeDtypeStruct((B, pad, F), F32),
                   jax.ShapeDtypeStruct((Bd, H, L, Dh), F32)],
        compiler_params=_resident_params(("arbitrary", "arbitrary", "arbitrary"), block_bytes),
        name="ffn_up",
    )(page_table, h2, w_gate, w_up, conv_w, conv_b.reshape(1, F), q, q, k_new, v_new, cache_k, cache_v)


def _ffn_up_sample_kernel(h_ref, wg_ref, wu_ref, cw_ref, cb_ref, hist_ref, f_ref, gt_ref, *, n_steps, n_seq):
    h = h_ref[...]
    gt = _dot(h, wg_ref[...].astype(BF16))
    up = _dot(h, wu_ref[...].astype(BF16))
    gt_ref[...] = gt
    ext = [hist_ref[i * n_seq:(i + 1) * n_seq, :] for i in range(CONV_W - 1)]
    ext += [gt[t * n_seq:(t + 1) * n_seq, :] for t in range(n_steps)]
    for t in range(n_steps):
        c = cb_ref[...]
        for i in range(CONV_W):
            c = c + ext[t + i] * cw_ref[i:i + 1, :]
        rows = slice(t * n_seq, (t + 1) * n_seq)
        f_ref[rows, :] = (_gelu_tanh(c) * up[rows, :]).astype(BF16)


def _ffn_up_sample(h2, w_gate, w_up, conv_w, conv_b, hist_tm, n_steps, n_seq, tn):
    rows, D = h2.shape
    F = w_gate.shape[1]
    col_spec = lambda r: pl.BlockSpec((r, tn), lambda j: (0, j))
    block_bytes = 2 * rows * D * 2 + 2 * 2 * D * tn * 4 + 2 * D * tn * 2 + 2 * rows * tn * 6 \
        + 2 * hist_tm.shape[0] * tn * 4 + 4 * rows * tn * 4
    return pl.pallas_call(
        functools.partial(_ffn_up_sample_kernel, n_steps=n_steps, n_seq=n_seq),
        grid=(F // tn,),
        in_specs=[pl.BlockSpec((rows, D), lambda j: (0, 0)), col_spec(D), col_spec(D),
                  col_spec(CONV_W), col_spec(1), col_spec(hist_tm.shape[0])],
        out_specs=[col_spec(rows), col_spec(rows)],
        out_shape=[jax.ShapeDtypeStruct((rows, F), BF16), jax.ShapeDtypeStruct((rows, F), F32)],
        compiler_params=_params(("arbitrary",), block_bytes),
        name="ffn_up_sample",
    )(h2, w_gate, w_up, conv_w, conv_b.reshape(1, F), hist_tm)


def _ffn_down_kernel(f_ref, w_ref, x1_ref, g_ref, y_ref):
    y_ref[...] = x1_ref[...] + _rms(_dot(f_ref[...], w_ref[...]), g_ref[...])


def _ffn_down(f, w_bf, x1, g, tm):
    rows, F = f.shape
    D = w_bf.shape[1]
    block_bytes = 2 * tm * F * 2 + F * D * 2 + 4 * tm * D * 4 + 2 * tm * D * 4
    return pl.pallas_call(
        _ffn_down_kernel,
        grid=(rows // tm,),
        in_specs=[pl.BlockSpec((tm, F), lambda i: (i, 0)),
                  pl.BlockSpec((F, D), lambda i: (0, 0), pipeline_mode=pl.Buffered(1)),
                  pl.BlockSpec((tm, D), lambda i: (i, 0)),
                  pl.BlockSpec((1, D), lambda i: (0, 0))],
        out_specs=pl.BlockSpec((tm, D), lambda i: (i, 0)),
        out_shape=jax.ShapeDtypeStruct((rows, D), F32),
        compiler_params=_resident_params(("arbitrary",), block_bytes),
        name="ffn_down",
    )(f, w_bf, x1, g.reshape(1, D))


def _pick_tile(n, target):
    t = min(n, target)
    while n % t:
        t //= 2
    return t


def _prompt_mixer(x, lp):
    B, S, D = x.shape
    tm = _pick_tile(S, 512)
    u, q, k, v = _inproj(x, lp["g_mix_pre"], lp["w_in"], jnp.arange(S), tm)
    pooled = _pool_prompt(u, lp["w_pool"], lp["pool_scale"], tm)
    attn = _moba_prompt(q, k, v)
    x1, h2 = _outproj(pooled.reshape(B * S, -1), attn.reshape(B * S, -1), x.reshape(B * S, D),
                      lp["w_out"], lp["g_mix_post"], lp["g_ffn_pre"], tm)
    return x1, h2, k, v, u


def _sample_projections(x_tm, pool_hist, page_table, lp, n_seq, n_steps):
    rows = x_tm.shape[0]
    past_len = page_table.shape[1] * PAGE_SIZE
    pos = past_len + jnp.repeat(jnp.arange(n_steps), n_seq)
    u, q, k, v = _inproj(x_tm[None], lp["g_mix_pre"], lp["w_in"], pos, rows)

    def to_seq_major(t):
        H = t.shape[0]
        return t.reshape(H, n_steps, n_seq, HEAD_DIM).transpose(2, 0, 1, 3)

    pooled = _pool_sample(u[0], _seq_to_rows(pool_hist), lp["w_pool"], lp["pool_scale"],
                          n_steps, n_seq, past_len)
    return to_seq_major(q[0]), to_seq_major(k[0]), to_seq_major(v[0]), pooled, u[0]


def _rows_to_seq(t, n_steps, n_seq):
    return t.reshape(n_steps, n_seq, t.shape[-1]).transpose(1, 0, 2)


def _seq_to_rows(t):
    return t.transpose(1, 0, 2).reshape(-1, t.shape[-1])


def _layer(xp, xs_tm, pool_hist, conv_hist, cache_k, cache_v, page_table, lp, n_seq, n_steps):
    B, S, D = xp.shape
    F = lp["w_gate"].shape[1]
    rows_s = xs_tm.shape[0]
    x1p, h2p, kp, vp, up = _prompt_mixer(xp, lp)
    qs, ks, vs, pooled_s, us = _sample_projections(xs_tm, pool_hist, page_table, lp, n_seq, n_steps)
    tm = _pick_tile(S, 512)
    tn = _pick_tile(F, 512)
    fp, tail, attn_s = _ffn_up(h2p.reshape(B, S, D), lp["w_gate"], lp["w_up"], lp["conv_w"], lp["conv_b"],
                               qs, ks, vs, cache_k, cache_v, page_table, tm, tn, _pick_tile(tm, 256))
    yp = _ffn_down(fp.reshape(B * S, F), lp["w_down"], x1p, lp["g_ffn_post"], _pick_tile(B * S, 256))
    attn_tm = attn_s.transpose(2, 0, 1, 3).reshape(rows_s, -1).astype(BF16)
    x1s, h2s = _outproj(pooled_s, attn_tm, xs_tm, lp["w_out"], lp["g_mix_post"], lp["g_ffn_pre"], rows_s)
    fs, gts = _ffn_up_sample(h2s, lp["w_gate"], lp["w_up"], lp["conv_w"], lp["conv_b"],
                             _seq_to_rows(conv_hist), n_steps, n_seq, tn)
    ys = _ffn_down(fs, lp["w_down"], x1s, lp["g_ffn_post"], rows_s)
    pool_p = up[:, S - POOL_HIST:, :]
    conv_p = tail[:, V7X_SUBLANES - (CONV_W - 1):, :]
    pool_s = jnp.concatenate([pool_hist, _rows_to_seq(us, n_steps, n_seq)], axis=1)[:, n_steps:]
    conv_s = jnp.concatenate([conv_hist, _rows_to_seq(gts, n_steps, n_seq)], axis=1)[:, n_steps:]
    return yp.reshape(B, S, D), ys, (kp, vp, ks, vs, pool_p, pool_s, conv_p, conv_s)


def kernel(x_prompt, x_sample, cache_k, cache_v, state_pool, state_conv, page_table,
           w_in, w_pool, pool_scale, w_out, g_mix_pre, g_mix_post,
           w_gate, w_up, conv_w, conv_b, w_down, g_ffn_pre, g_ffn_post):
    depth = w_in.shape[0]
    n_seq, n_steps, D = x_sample.shape
    assert w_in.shape[2] == 4 * pool_scale.shape[1], "pooling and attention widths must match"
    yp = x_prompt
    ys = x_sample.transpose(1, 0, 2).reshape(n_steps * n_seq, D)
    outs = [[] for _ in range(8)]
    for l in range(depth):
        lp = {"w_in": w_in[l].astype(BF16), "w_pool": w_pool[l].astype(BF16), "pool_scale": pool_scale[l],
              "w_out": w_out[l].astype(BF16), "g_mix_pre": g_mix_pre[l], "g_mix_post": g_mix_post[l],
              "w_gate": w_gate[l], "w_up": w_up[l], "conv_w": conv_w[l],
              "conv_b": conv_b[l], "w_down": w_down[l].astype(BF16), "g_ffn_pre": g_ffn_pre[l],
              "g_ffn_post": g_ffn_post[l]}
        yp, ys, states = _layer(yp, ys, state_pool[l], state_conv[l], cache_k[l], cache_v[l],
                                page_table, lp, n_seq, n_steps)
        for lst, val in zip(outs, states):
            lst.append(val)
    y_sample = ys.reshape(n_steps, n_seq, D).transpose(1, 0, 2)
    return (yp, y_sample) + tuple(jnp.stack(o) for o in outs)
```

```python
import functools

import jax
import jax.numpy as jnp
from jax import lax
from jax.experimental import pallas as pl
from jax.experimental.pallas import tpu as pltpu

F32 = jnp.float32
BF16 = jnp.bfloat16

POOL_WINDOWS = (2, 4, 8, 16)
POOL_HIST = max(POOL_WINDOWS) - 1
HEAD_DIM = 128
ROT_DIM = HEAD_DIM // 4
ROPE_THETA = 500000.0
MOBA_BLOCK = 256
MOBA_TOPK = 3
QUERY_BLOCKS_PER_DOT = 2
PAGE_SIZE = 128
CONV_W = 3
EPS = 1e-6
NEG_INF = float("-inf")
LOG2E = 1.4426950408889634

V7X_SUBLANES = 8
V7X_VMEM_BYTES = 64 * 1024 * 1024
VMEM_RESERVE_BYTES = 4 * 1024 * 1024


def _vmem_limit(block_bytes):
    return int(min(2 * block_bytes, V7X_VMEM_BYTES - VMEM_RESERVE_BYTES))


def _params(semantics, block_bytes):
    return pltpu.CompilerParams(dimension_semantics=semantics,
                                vmem_limit_bytes=_vmem_limit(block_bytes))


def _resident_params(semantics, block_bytes):
    limit = int(min(block_bytes + VMEM_RESERVE_BYTES, V7X_VMEM_BYTES - VMEM_RESERVE_BYTES))
    return pltpu.CompilerParams(dimension_semantics=semantics, vmem_limit_bytes=limit)


def _rms(x, g):
    return x * lax.rsqrt(jnp.mean(x * x, axis=-1, keepdims=True) + EPS) * g


def _dot(a, b):
    return jnp.dot(a, b, preferred_element_type=F32)


def _dot_nt(a, b, precision=None):
    return lax.dot_general(a, b, (((1,), (1,)), ((), ())), precision=precision,
                           preferred_element_type=F32)


def _rope_tables(pos):
    half = ROT_DIM // 2
    inv = ROPE_THETA ** (-jnp.arange(half, dtype=F32) * (2.0 / ROT_DIM))
    ang = pos.astype(F32)[:, None] * inv[None, :]
    cos, sin = jnp.cos(ang), jnp.sin(ang)
    n = pos.shape[0]
    c = jnp.concatenate([cos, cos, jnp.ones((n, HEAD_DIM - ROT_DIM), F32)], axis=1)
    s_lo = jnp.concatenate([-sin, jnp.zeros((n, HEAD_DIM - half), F32)], axis=1)
    s_hi = jnp.concatenate([jnp.zeros((n, half), F32), sin,
                            jnp.zeros((n, HEAD_DIM - ROT_DIM), F32)], axis=1)
    return c, s_lo, s_hi


def _inproj_kernel(x_ref, g_ref, w_ref, c_ref, slo_ref, shi_ref,
                   u_ref, q_ref, k_ref, v_ref, hn_ref, *, n_heads, width):
    hn_ref[...] = _rms(x_ref[0], g_ref[...]).astype(BF16)
    pair = 2 * HEAD_DIM

    def rope(h):
        return (h * c_ref[...]
                + pltpu.roll(h, HEAD_DIM - ROT_DIM // 2, 1) * slo_ref[...]
                + pltpu.roll(h, ROT_DIM // 2, 1) * shi_ref[...])

    for p in range(width // pair):
        u_ref[0, :, p * pair:(p + 1) * pair] = _dot(hn_ref[...], w_ref[:, p * pair:(p + 1) * pair])
    for out_ref, part, rotary in ((q_ref, 1, True), (k_ref, 2, True), (v_ref, 3, False)):
        for p in range(n_heads // 2):
            col = part * width + p * pair
            res = _dot(hn_ref[...], w_ref[:, col:col + pair])
            for half in range(2):
                h = res[:, half * HEAD_DIM:(half + 1) * HEAD_DIM]
                out_ref[0, 2 * p + half] = rope(h) if rotary else h


def _inproj(x, g, w_bf, pos, tm):
    B, S, D = x.shape
    width = w_bf.shape[1] // 4
    n_heads = width // HEAD_DIM
    assert n_heads % 2 == 0
    c, s_lo, s_hi = _rope_tables(pos)
    tab_spec = pl.BlockSpec((tm, HEAD_DIM), lambda b, s: (s, 0))
    hm_spec = pl.BlockSpec((1, n_heads, tm, HEAD_DIM), lambda b, s: (b, 0, s, 0))
    hm_shape = jax.ShapeDtypeStruct((B, n_heads, S, HEAD_DIM), F32)
    block_bytes = 2 * tm * D * 4 + tm * D * 2 + D * 4 * width * 2 + 2 * 4 * tm * width * 4
    return pl.pallas_call(
        functools.partial(_inproj_kernel, n_heads=n_heads, width=width),
        grid=(B, S // tm),
        in_specs=[
            pl.BlockSpec((1, tm, D), lambda b, s: (b, s, 0)),
            pl.BlockSpec((1, D), lambda b, s: (0, 0)),
            pl.BlockSpec((D, 4 * width), lambda b, s: (0, 0), pipeline_mode=pl.Buffered(1)),
            tab_spec, tab_spec, tab_spec,
        ],
        out_specs=[pl.BlockSpec((1, tm, width), lambda b, s: (b, s, 0)), hm_spec, hm_spec, hm_spec],
        out_shape=[jax.ShapeDtypeStruct((B, S, width), F32), hm_shape, hm_shape, hm_shape],
        scratch_shapes=[pltpu.VMEM((tm, D), BF16)],
        compiler_params=_resident_params(("arbitrary", "arbitrary"), block_bytes),
        name="inproj",
    )(x, g.reshape(1, D), w_bf, c, s_lo, s_hi)


def _shift_rows(x, prev8, k):
    pad = V7X_SUBLANES
    rolled = pltpu.roll(x, k, 0)
    row8 = lax.broadcasted_iota(jnp.int32, prev8.shape, 0)
    head = jnp.where(row8 < k, pltpu.roll(prev8, k % pad, 0) if k % pad else prev8, rolled[0:pad])
    return jnp.concatenate([head, rolled[pad:]], axis=0)


def _pool_kernel(u_ref, w_ref, scale_ref, o_ref, halo_ref, *, ts, group):
    s = pl.program_id(1)
    pad = V7X_SUBLANES

    @pl.when(s == 0)
    def _():
        halo_ref[...] = jnp.zeros(halo_ref.shape, F32)

    pos = s * ts + lax.broadcasted_iota(jnp.int32, (ts, 1), 0)
    slot = 0
    for gi, w in enumerate(POOL_WINDOWS):
        cols = slice(gi * group, (gi + 1) * group)
        x = u_ref[0, :, cols]
        acc = x
        span = 1
        while span < w:
            prev8 = halo_ref[slot]
            halo_ref[slot] = acc[ts - pad:ts, :]
            acc = acc + _shift_rows(acc, prev8, span)
            slot += 1
            span *= 2
        cnt = jnp.minimum(pos + 1, w).astype(F32)
        d = (acc / cnt - x).astype(BF16)
        y = _dot(d, w_ref[gi]) * scale_ref[:, cols]
        o_ref[0, :, cols] = y.astype(BF16)


def _pool_prompt(u, w_pool_bf, scale, ts):
    B, S, C = u.shape
    n_groups, group, _ = w_pool_bf.shape
    assert all(w & (w - 1) == 0 and w <= 2 * V7X_SUBLANES for w in POOL_WINDOWS)
    n_levels = sum(w.bit_length() - 1 for w in POOL_WINDOWS)
    block_bytes = 2 * ts * C * 4 + 2 * n_groups * group * group * 2 + 2 * ts * C * 2 + 6 * ts * group * 4
    return pl.pallas_call(
        functools.partial(_pool_kernel, ts=ts, group=group),
        grid=(B, S // ts),
        in_specs=[
            pl.BlockSpec((1, ts, C), lambda b, s: (b, s, 0)),
            pl.BlockSpec((n_groups, group, group), lambda b, s: (0, 0, 0)),
            pl.BlockSpec((1, C), lambda b, s: (0, 0)),
        ],
        out_specs=pl.BlockSpec((1, ts, C), lambda b, s: (b, s, 0)),
        out_shape=jax.ShapeDtypeStruct((B, S, C), BF16),
        scratch_shapes=[pltpu.VMEM((n_levels, V7X_SUBLANES, group), F32)],
        compiler_params=_params(("arbitrary", "arbitrary"), block_bytes),
        name="pool_prompt",
    )(u, w_pool_bf, scale.reshape(1, C))


def _pool_sample_kernel(u_ref, hist_ref, w_ref, scale_ref, o_ref, *, n_steps, n_seq, group, past_len):
    def ext(i):
        if i < POOL_HIST:
            return hist_ref[i * n_seq:(i + 1) * n_seq, :]
        return u_ref[(i - POOL_HIST) * n_seq:(i - POOL_HIST + 1) * n_seq, :]

    for gi, w in enumerate(POOL_WINDOWS):
        cols = slice(gi * group, (gi + 1) * group)
        ds = []
        for t in range(n_steps):
            cur = ext(POOL_HIST + t)[:, cols]
            acc = cur
            for back in range(1, w):
                acc = acc + ext(POOL_HIST + t - back)[:, cols]
            cnt = float(min(past_len + t + 1, w))
            ds.append((acc / cnt - cur).astype(BF16))
        d = jnp.concatenate(ds, axis=0)
        y = _dot(d, w_ref[gi]) * scale_ref[:, cols]
        o_ref[:, cols] = y.astype(BF16)


def _pool_sample(u_tm, hist_tm, w_pool_bf, scale, n_steps, n_seq, past_len):
    rows, C = u_tm.shape
    n_groups, group, _ = w_pool_bf.shape
    block_bytes = 2 * (rows + hist_tm.shape[0]) * C * 4 + 2 * n_groups * group * group * 2 + 2 * rows * C * 2
    return pl.pallas_call(
        functools.partial(_pool_sample_kernel, n_steps=n_steps, n_seq=n_seq, group=group,
                          past_len=past_len),
        out_shape=jax.ShapeDtypeStruct((rows, C), BF16),
        compiler_params=pltpu.CompilerParams(vmem_limit_bytes=_vmem_limit(block_bytes)),
        name="pool_sample",
    )(u_tm, hist_tm, w_pool_bf, scale.reshape(1, C))


class _PerHead:
    def __init__(self, refs):
        self.refs = refs

    def __getitem__(self, idx):
        return self.refs[idx[0]][idx[1:]] if isinstance(idx, tuple) else self.refs[idx][...]

    def __setitem__(self, idx, value):
        self.refs[idx[0]][idx[1:]] = value


def _moba_prompt_kernel(q_ref, k_ref, v_ref, o_ref, *scratch, seq, hp):
    blk = MOBA_BLOCK
    nb = seq // blk
    n_sel = min(MOBA_TOPK, nb - 1)
    scale = HEAD_DIM ** -0.5
    kb_ref, vt_ref, km_ref, s_ref, p_ref = (_PerHead(scratch[kind * hp:(kind + 1) * hp]) for kind in range(5))

    for hh in range(hp):
        for n in range(nb):
            rows = slice(n * blk, (n + 1) * blk)
            kn = k_ref[0, hh, rows, :]
            kb_ref[hh, rows, :] = kn.astype(BF16)
            km_ref[hh, n:n + 1, :] = jnp.sum(kn, axis=0, keepdims=True) * (1.0 / blk)
            vt_ref[hh, :, rows] = v_ref[0, hh, rows, :].T.astype(BF16)

    blk_id = lax.broadcasted_iota(jnp.int32, (nb, blk), 0)
    key_i = lax.broadcasted_iota(jnp.int32, (blk, blk), 0)
    qry_i = lax.broadcasted_iota(jnp.int32, (blk, blk), 1)
    causal_bias = jnp.where(key_i <= qry_i, 0.0, NEG_INF).astype(F32)

    def scores(hh, qblocks):
        first, last = qblocks[0], qblocks[-1]
        nk = (last + 1) * blk
        width = len(qblocks) * blk
        qg = q_ref[0, hh, first * blk:nk, :]
        s_ref[hh, 0:nk, 0:width] = _dot_nt(kb_ref[hh, 0:nk, :], (qg * (scale * LOG2E)).astype(BF16))
        return _dot_nt(km_ref[hh], qg, precision=lax.Precision.HIGHEST) if last > 0 else None

    def softmax(hh, qblocks, gate):
        last = qblocks[-1]
        sums = []
        for a, i in enumerate(qblocks):
            cols = slice(a * blk, (a + 1) * blk)
            own = slice(i * blk, (i + 1) * blk)
            biases = []
            if i > 0:
                gate_a = gate[:, cols]
                valid = jnp.where(blk_id < i, 1.0, 0.0)
                for n in range(i):
                    gn = gate_a[n:n + 1, :]
                    beats = jnp.where(blk_id < n, jnp.where(gate_a >= gn, 1.0, 0.0),
                                      jnp.where(gate_a > gn, 1.0, 0.0))
                    rank = jnp.sum(beats * valid, axis=0, keepdims=True)
                    biases.append(jnp.where(rank < n_sel, 0.0, NEG_INF).astype(F32))

            s_own = s_ref[hh, own, cols] + causal_bias
            m = jnp.max(s_own, axis=0, keepdims=True)
            for n in range(i):
                m = jnp.maximum(m, jnp.max(s_ref[hh, n * blk:(n + 1) * blk, cols], axis=0, keepdims=True)
                                + biases[n])

            p = jnp.exp2(s_own - m)
            l = jnp.sum(p, axis=0, keepdims=True)
            p_ref[hh, own, cols] = p.astype(BF16)
            for n in range(i):
                rows = slice(n * blk, (n + 1) * blk)
                p = jnp.exp2(s_ref[hh, rows, cols] + (biases[n] - m))
                l = l + jnp.sum(p, axis=0, keepdims=True)
                p_ref[hh, rows, cols] = p.astype(BF16)
            for n in range(i + 1, last + 1):
                p_ref[hh, n * blk:(n + 1) * blk, cols] = jnp.zeros((blk, blk), BF16)
            sums.append(l)
        return sums

    def output(hh, qblocks, sums):
        nk = (qblocks[-1] + 1) * blk
        width = len(qblocks) * blk
        o_t = _dot(vt_ref[hh, :, 0:nk], p_ref[hh, 0:nk, 0:width])
        for a, i in enumerate(qblocks):
            o_a = o_t[:, a * blk:(a + 1) * blk] / sums[a]
            o_ref[0, i * blk:(i + 1) * blk, hh * HEAD_DIM:(hh + 1) * HEAD_DIM] = o_a.T.astype(BF16)

    items = [(hh, list(range(first, min(first + QUERY_BLOCKS_PER_DOT, nb))))
             for first in range(0, nb, QUERY_BLOCKS_PER_DOT) for hh in range(hp)]
    gate = scores(*items[0])
    for cur, nxt in zip(items, items[1:] + [None]):
        next_gate = scores(*nxt) if nxt is not None else None
        output(*cur, softmax(*cur, gate))
        gate = next_gate


def _moba_prompt(q, k, v):
    B, H, S, Dh = q.shape
    nb = S // MOBA_BLOCK
    hp = 2 if H % 2 == 0 else 1
    width = QUERY_BLOCKS_PER_DOT * MOBA_BLOCK
    in_spec = pl.BlockSpec((1, hp, S, Dh), lambda b, g: (b, g, 0, 0))
    block_bytes = hp * (2 * 3 * S * Dh * 4 + 2 * S * Dh * 2 + 2 * S * Dh * 2 + S * width * 6)
    return pl.pallas_call(
        functools.partial(_moba_prompt_kernel, seq=S, hp=hp),
        grid=(B, H // hp),
        in_specs=[in_spec, in_spec, in_spec],
        out_specs=pl.BlockSpec((1, S, hp * Dh), lambda b, g: (b, 0, g)),
        out_shape=jax.ShapeDtypeStruct((B, S, H * Dh), BF16),
        scratch_shapes=(
            [pltpu.VMEM((S, Dh), BF16)] * hp
            + [pltpu.VMEM((Dh, S), BF16)] * hp
            + [pltpu.VMEM((nb, Dh), F32)] * hp
            + [pltpu.VMEM((S, width), F32)] * hp
            + [pltpu.VMEM((S, width), BF16)] * hp
        ),
        compiler_params=_params(("arbitrary", "arbitrary"), block_bytes),
        name="moba_prompt",
    )(q, k, v)


class _PagedAttention:
    def __init__(self, pt_ref, qc_ref, qp_ref, kn_ref, vn_ref, ck_ref, cv_ref, o_ref,
                 kbuf, ksel, vbuf, km_ref, q8_ref, sel_ref, ksem, vsem, *, n_pages, n_groups, hp, n_new):
        self.__dict__.update(locals())
        self.ppb = MOBA_BLOCK // PAGE_SIZE
        self.nblk = n_pages // self.ppb
        self.n_sel = min(MOBA_TOPK, self.nblk)

    def unit_bh(self, u):
        return u // self.n_groups, lax.rem(u, self.n_groups) * self.hp

    def k_copy(self, u, page):
        bb, h0 = self.unit_bh(u)
        sl = lax.rem(u, 2)
        return pltpu.make_async_copy(self.ck_ref.at[self.pt_ref[bb, page], pl.ds(h0, self.hp)],
                                     self.kbuf.at[sl, :, pl.ds(page * PAGE_SIZE, PAGE_SIZE), :],
                                     self.ksem.at[sl])

    def start_k(self, u):
        def body(page, carry):
            self.k_copy(u, page).start()
            return carry
        lax.fori_loop(0, self.n_pages, body, 0, unroll=8)

    def wait_k(self, u):
        for page in range(self.n_pages):
            self.k_copy(u, page).wait()

    def sel_index(self, hh, j, c):
        return (hh * self.n_new + j) * self.n_sel + c

    def v_copies(self, u, heads):
        bb, h0 = self.unit_bh(u)
        sl = lax.rem(u, 2)
        for hh in heads:
            for j in range(self.n_new):
                for c in range(self.n_sel):
                    blk = self.sel_ref[sl, self.sel_index(hh, j, c)]
                    for pg in range(self.ppb):
                        page = self.pt_ref[bb, blk * self.ppb + pg]
                        dst = ((j * self.n_sel + c) * self.ppb + pg) * PAGE_SIZE
                        yield pltpu.make_async_copy(self.cv_ref.at[page, h0 + hh],
                                                    self.vbuf.at[sl, hh, pl.ds(dst, PAGE_SIZE), :],
                                                    self.vsem.at[sl])

    def start_v(self, u, heads):
        for cp in self.v_copies(u, heads):
            cp.start()

    def wait_v(self, u):
        for cp in self.v_copies(u, range(self.hp)):
            cp.wait()

    def block_means(self, u, heads):
        sl = lax.rem(u, 2)
        for hh in heads:
            for n in range(self.nblk):
                kn_blk = self.kbuf[sl, hh, n * MOBA_BLOCK:(n + 1) * MOBA_BLOCK, :]
                self.km_ref[hh, n:n + 1, :] = jnp.sum(kn_blk, axis=0, keepdims=True) * (1.0 / MOBA_BLOCK)

    def select(self, u, heads):
        sl = lax.rem(u, 2)
        for hh in heads:
            self.q8_ref[hh] = jnp.zeros(self.q8_ref.shape[1:], F32)
            self.q8_ref[hh, 0:self.n_new, :] = self.qc_ref[0, hh]
            gate = _dot_nt(self.q8_ref[hh], self.km_ref[hh], precision=lax.Precision.HIGHEST)
            lane = lax.broadcasted_iota(jnp.int32, gate.shape, 1)
            picks = [jnp.zeros((gate.shape[0], 1), F32) for _ in range(self.n_sel)]
            for n in range(self.nblk):
                gn = gate[:, n:n + 1]
                beats = jnp.where(lane < n, jnp.where(gate >= gn, 1.0, 0.0), jnp.where(gate > gn, 1.0, 0.0))
                rank = jnp.sum(beats, axis=1, keepdims=True)
                for c in range(self.n_sel):
                    picks[c] = jnp.where(rank == float(c), float(n), picks[c])
            for j in range(self.n_new):
                for c in range(self.n_sel):
                    blk = picks[c][j, 0].astype(jnp.int32)
                    self.sel_ref[sl, self.sel_index(hh, j, c)] = blk
                    row0 = (j * self.n_sel + c) * MOBA_BLOCK
                    start = pl.multiple_of(blk * MOBA_BLOCK, MOBA_BLOCK)
                    self.ksel[sl, hh, row0:row0 + MOBA_BLOCK, :] = self.kbuf[sl, hh, pl.ds(start, MOBA_BLOCK), :]

    def attend(self, u, heads):
        sl = lax.rem(u, 2)
        scale = HEAD_DIM ** -0.5
        new_row = lax.broadcasted_iota(jnp.int32, (self.n_new, 1), 0)
        for hh in heads:
            q = self.qp_ref[0, hh]
            kn = self.kn_ref[0, hh]
            vn = self.vn_ref[0, hh]
            for j in range(self.n_new):
                qj = q[j:j + 1, :] * (scale * LOG2E)
                rows = [slice((j * self.n_sel + c) * MOBA_BLOCK, (j * self.n_sel + c + 1) * MOBA_BLOCK)
                        for c in range(self.n_sel)]
                s_sel = [jnp.sum(self.ksel[sl, hh, r, :] * qj, axis=1, keepdims=True) for r in rows]
                s_new = jnp.sum(kn * qj, axis=1, keepdims=True)
                s_new = jnp.where(new_row <= j, s_new, NEG_INF)
                m = jnp.max(s_new, axis=0, keepdims=True)
                for s in s_sel:
                    m = jnp.maximum(m, jnp.max(s, axis=0, keepdims=True))
                p_new = jnp.exp2(s_new - m)
                l = jnp.sum(p_new, axis=0, keepdims=True)
                acc = jnp.sum(p_new * vn, axis=0, keepdims=True)
                for s, r in zip(s_sel, rows):
                    p = jnp.exp2(s - m)
                    l = l + jnp.sum(p, axis=0, keepdims=True)
                    acc = acc + jnp.sum(p * self.vbuf[sl, hh, r, :], axis=0, keepdims=True)
                self.o_ref[0, hh, j:j + 1, :] = acc / l


def _paged_attention_scratch(n_pages, hp, n_new):
    nblk = n_pages // (MOBA_BLOCK // PAGE_SIZE)
    n_sel = min(MOBA_TOPK, nblk)
    picked_rows = n_new * n_sel * MOBA_BLOCK
    shapes = [
        pltpu.VMEM((2, hp, n_pages * PAGE_SIZE, HEAD_DIM), F32),
        pltpu.VMEM((2, hp, picked_rows, HEAD_DIM), F32),
        pltpu.VMEM((2, hp, picked_rows, HEAD_DIM), F32),
        pltpu.VMEM((hp, nblk, HEAD_DIM), F32),
        pltpu.VMEM((hp, V7X_SUBLANES, HEAD_DIM), F32),
        pltpu.SMEM((2, hp * n_new * n_sel), jnp.int32),
        pltpu.SemaphoreType.DMA((2,)),
        pltpu.SemaphoreType.DMA((2,)),
    ]
    n_bytes = 4 * HEAD_DIM * 2 * hp * (n_pages * PAGE_SIZE + 2 * picked_rows)
    return shapes, n_bytes


def _outproj_kernel(pooled_ref, attn_ref, x_ref, w_ref, gpost_ref, gffn_ref, x1_ref, h2_ref, *, split):
    mix = _dot(pooled_ref[...], w_ref[0:split, :]) + _dot(attn_ref[...], w_ref[split:, :])
    x1 = x_ref[...] + _rms(mix, gpost_ref[...])
    x1_ref[...] = x1
    h2_ref[...] = _rms(x1, gffn_ref[...]).astype(BF16)


def _outproj(pooled, attn, x, w_bf, g_post, g_ffn, tm):
    rows, D = x.shape
    split = pooled.shape[1]
    wa = attn.shape[1]
    row_spec = lambda width: pl.BlockSpec((tm, width), lambda i: (i, 0))
    vec_spec = pl.BlockSpec((1, D), lambda i: (0, 0))
    block_bytes = 2 * tm * (split + wa) * 2 + 2 * tm * D * 4 + 2 * (split + wa) * D * 2 \
        + 2 * tm * D * 6 + 2 * tm * D * 4
    return pl.pallas_call(
        functools.partial(_outproj_kernel, split=split),
        grid=(rows // tm,),
        in_specs=[row_spec(split), row_spec(wa), row_spec(D),
                  pl.BlockSpec((split + wa, D), lambda i: (0, 0)), vec_spec, vec_spec],
        out_specs=[row_spec(D), row_spec(D)],
        out_shape=[jax.ShapeDtypeStruct((rows, D), F32), jax.ShapeDtypeStruct((rows, D), BF16)],
        compiler_params=_params(("arbitrary",), block_bytes),
        name="outproj",
    )(pooled, attn, x, w_bf, g_post.reshape(1, D), g_ffn.reshape(1, D))


def _gelu_tanh(c):
    return c * (0.5 * (1.0 + jnp.tanh(0.7978845608028654 * (c + 0.044715 * (c * c * c)))))


def _conv_gelu_gate(gt, prev8, up, cw_ref, cb_ref):
    c = cb_ref[...]
    for i in range(CONV_W):
        back = CONV_W - 1 - i
        c = c + (_shift_rows(gt, prev8, back) if back else gt) * cw_ref[i:i + 1, :]
    return _gelu_tanh(c) * up


def _ffn_up_kernel(pt_ref, h_ref, wg_ref, wu_ref, cw_ref, cb_ref, qc_ref, qp_ref, kn_ref, vn_ref, ck_ref, cv_ref,
                   f_ref, tail_ref, o_ref, wgb_ref, wub_ref, halo_ref, *attn_scratch,
                   tm, sub, n_b, n_s, n_units, attn_params):
    j, b, s = pl.program_id(0), pl.program_id(1), pl.program_id(2)
    t = (j * n_b + b) * n_s + s
    pad = V7X_SUBLANES
    last_sub = tm // sub - 1
    attn = _PagedAttention(pt_ref, qc_ref, qp_ref, kn_ref, vn_ref, ck_ref, cv_ref, o_ref,
                           *attn_scratch, **attn_params)

    @pl.when((b == 0) & (s == 0))
    def _():
        wgb_ref[...] = wg_ref[...].astype(BF16)
        wub_ref[...] = wu_ref[...].astype(BF16)

    @pl.when(s == 0)
    def _():
        halo_ref[...] = jnp.zeros(halo_ref.shape, F32)

    @pl.when(t == 0)
    def _():
        attn.start_k(t)

    @pl.when(t < n_units)
    def _():
        attn.wait_k(t)

    @pl.when((t >= 1) & (t <= n_units))
    def _():
        attn.wait_v(t - 1)

    @pl.when(t + 1 < n_units)
    def _():
        attn.start_k(t + 1)

    def gate_up(before, middle):
        prev8 = halo_ref[...]
        half = sub // 2
        for r in range(tm // sub):
            for stage in before.get(r, ()):
                stage()
            row0 = r * sub
            if r in middle:
                lo = h_ref[0, row0:row0 + half, :]
                gt_lo, up_lo = _dot(lo, wgb_ref[...]), _dot(lo, wub_ref[...])
                for stage in middle[r]:
                    stage()
                hi = h_ref[0, row0 + half:row0 + sub, :]
                gt = jnp.concatenate([gt_lo, _dot(hi, wgb_ref[...])], axis=0)
                up = jnp.concatenate([up_lo, _dot(hi, wub_ref[...])], axis=0)
            else:
                hr = h_ref[0, row0:row0 + sub, :]
                gt, up = _dot(hr, wgb_ref[...]), _dot(hr, wub_ref[...])
            f_ref[0, row0:row0 + sub, :] = _conv_gelu_gate(gt, prev8, up, cw_ref, cb_ref).astype(BF16)
            prev8 = gt[sub - pad:sub, :]
        halo_ref[...] = prev8
        tail_ref[0] = prev8

    heads = tuple(range(attn.hp))

    def means():
        attn.block_means(t, heads)

    def select_and_fetch():
        attn.select(t, heads)
        attn.start_v(t, heads)

    def attend_previous():
        attn.attend(t - 1, heads)

    @pl.when(t == 0)
    def _():
        gate_up({0: [means]}, {0: [select_and_fetch]})

    @pl.when((t >= 1) & (t < n_units))
    def _():
        before = {0: [means]}
        before.setdefault(last_sub, []).append(attend_previous)
        gate_up(before, {0: [select_and_fetch]})

    @pl.when(t == n_units)
    def _():
        gate_up({last_sub: [attend_previous]}, {})

    @pl.when(t > n_units)
    def _():
        gate_up({}, {})


def _ffn_up(h2, w_gate, w_up, conv_w, conv_b, q, k_new, v_new, cache_k, cache_v, page_table, tm, tn, sub):
    B, S, D = h2.shape
    F = w_gate.shape[1]
    Bd, H, L, Dh = q.shape
    n_pages = page_table.shape[1]
    ppb = MOBA_BLOCK // PAGE_SIZE
    assert n_pages % ppb == 0, "past length must be a whole number of MoBA blocks"
    assert n_pages >= ppb and L <= V7X_SUBLANES
    hp = 2 if H % 2 == 0 else 1
    n_groups = H // hp
    n_units = Bd * n_groups
    n_b, n_s = B, S // tm
    assert (F // tn) * n_b * n_s > n_units, "not enough grid steps to host the sample attention"
    pad = V7X_SUBLANES

    def unit_spec(lag):
        def index_map(j, b, s, pt):
            u = jnp.clip((j * n_b + b) * n_s + s - lag, 0, n_units - 1)
            return (u // n_groups, lax.rem(u, n_groups), 0, 0)
        return pl.BlockSpec((1, hp, L, Dh), index_map)

    w_spec = pl.BlockSpec((D, tn), lambda j, b, s, pt: (0, j))
    any_spec = pl.BlockSpec(memory_space=pl.ANY)
    attn_scratch, attn_bytes = _paged_attention_scratch(n_pages, hp, L)
    block_bytes = (2 * tm * D * 2 + 2 * 2 * D * tn * 4 + 2 * D * tn * 2 + 2 * tm * tn * 2 + 8 * sub * tn * 4
                   + attn_bytes)
    grid_spec = pltpu.PrefetchScalarGridSpec(
        num_scalar_prefetch=1,
        grid=(F // tn, n_b, n_s),
        in_specs=[
            pl.BlockSpec((1, tm, D), lambda j, b, s, pt: (b, s, 0)),
            w_spec, w_spec,
            pl.BlockSpec((CONV_W, tn), lambda j, b, s, pt: (0, j)),
            pl.BlockSpec((1, tn), lambda j, b, s, pt: (0, j)),
            unit_spec(0), unit_spec(1), unit_spec(1), unit_spec(1), any_spec, any_spec,
        ],
        out_specs=[pl.BlockSpec((1, tm, tn), lambda j, b, s, pt: (b, s, j)),
                   pl.BlockSpec((1, pad, tn), lambda j, b, s, pt: (b, 0, j)),
                   unit_spec(1)],
        scratch_shapes=[pltpu.VMEM((D, tn), BF16), pltpu.VMEM((D, tn), BF16), pltpu.VMEM((pad, tn), F32)]
        + attn_scratch,
    )
    return pl.pallas_call(
        functools.partial(_ffn_up_kernel, tm=tm, sub=sub, n_b=n_b, n_s=n_s, n_units=n_units,
                          attn_params=dict(n_pages=n_pages, n_groups=n_groups, hp=hp, n_new=L)),
        grid_spec=grid_spec,
        out_shape=[jax.ShapeDtypeStruct((B, S, F), BF16), jax.ShapeDtypeStruct((B, pad, F), F32),
                   jax.ShapeDtypeStruct((Bd, H, L, Dh), F32)],
        compiler_params=_resident_params(("arbitrary", "arbitrary", "arbitrary"), block_bytes),
        name="ffn_up",
    )(page_table, h2, w_gate, w_up, conv_w, conv_b.reshape(1, F), q, q, k_new, v_new, cache_k, cache_v)


def _ffn_up_sample_kernel(h_ref, wg_ref, wu_ref, cw_ref, cb_ref, hist_ref, f_ref, gt_ref, *, n_steps, n_seq):
    h = h_ref[...]
    gt = _dot(h, wg_ref[...].astype(BF16))
    up = _dot(h, wu_ref[...].astype(BF16))
    gt_ref[...] = gt
    ext = [hist_ref[i * n_seq:(i + 1) * n_seq, :] for i in range(CONV_W - 1)]
    ext += [gt[t * n_seq:(t + 1) * n_seq, :] for t in range(n_steps)]
    for t in range(n_steps):
        c = cb_ref[...]
        for i in range(CONV_W):
            c = c + ext[t + i] * cw_ref[i:i + 1, :]
        rows = slice(t * n_seq, (t + 1) * n_seq)
        f_ref[rows, :] = (_gelu_tanh(c) * up[rows, :]).astype(BF16)


def _ffn_up_sample(h2, w_gate, w_up, conv_w, conv_b, hist_tm, n_steps, n_seq, tn):
    rows, D = h2.shape
    F = w_gate.shape[1]
    col_spec = lambda r: pl.BlockSpec((r, tn), lambda j: (0, j))
    block_bytes = 2 * rows * D * 2 + 2 * 2 * D * tn * 4 + 2 * D * tn * 2 + 2 * rows * tn * 6 \
        + 2 * hist_tm.shape[0] * tn * 4 + 4 * rows * tn * 4
    return pl.pallas_call(
        functools.partial(_ffn_up_sample_kernel, n_steps=n_steps, n_seq=n_seq),
        grid=(F // tn,),
        in_specs=[pl.BlockSpec((rows, D), lambda j: (0, 0)), col_spec(D), col_spec(D),
                  col_spec(CONV_W), col_spec(1), col_spec(hist_tm.shape[0])],
        out_specs=[col_spec(rows), col_spec(rows)],
        out_shape=[jax.ShapeDtypeStruct((rows, F), BF16), jax.ShapeDtypeStruct((rows, F), F32)],
        compiler_params=_params(("arbitrary",), block_bytes),
        name="ffn_up_sample",
    )(h2, w_gate, w_up, conv_w, conv_b.reshape(1, F), hist_tm)


def _ffn_down_kernel(f_ref, fs_ref, w_hbm, x1_ref, x1s_ref, g_ref, y_ref, ys_ref, wb_ref, stage_ref, sem,
                     *, n_tiles, chunk):
    i = pl.program_id(0)
    n_chunks = wb_ref.shape[0] // chunk

    def w_copy(c):
        return pltpu.make_async_copy(w_hbm.at[pl.ds(c * chunk, chunk), :], stage_ref.at[c % 2], sem.at[c % 2])

    @pl.when(i == 0)
    def _():
        w_copy(0).start()
        for c in range(n_chunks):
            if c + 1 < n_chunks:
                w_copy(c + 1).start()
            w_copy(c).wait()
            wb_ref[c * chunk:(c + 1) * chunk, :] = stage_ref[c % 2].astype(BF16)

    @pl.when(i < n_tiles)
    def _():
        y_ref[...] = x1_ref[...] + _rms(_dot(f_ref[...], wb_ref[...]), g_ref[...])

    @pl.when(i == n_tiles)
    def _():
        ys_ref[...] = x1s_ref[...] + _rms(_dot(fs_ref[...], wb_ref[...]), g_ref[...])


def _ffn_down(f, f_s, w_down, x1, x1_s, g, tm):
    rows, F = f.shape
    rows_s = f_s.shape[0]
    D = w_down.shape[1]
    n_tiles = rows // tm
    chunk = _pick_tile(F, 512)
    while (F // chunk) % 2 == 0 and chunk * D * 4 > 4 * 1024 * 1024:
        chunk //= 2
    assert chunk % 16 == 0
    tile = lambda r: pl.BlockSpec((tm, r), lambda i: (jnp.minimum(i, n_tiles - 1), 0))
    whole = lambda r, c: pl.BlockSpec((r, c), lambda i: (0, 0))
    block_bytes = (2 * tm * F * 2 + 2 * rows_s * F * 2 + F * D * 2 + 2 * chunk * D * 4 + 4 * tm * D * 4
                   + 4 * rows_s * D * 4 + 2 * tm * D * 4)
    return pl.pallas_call(
        functools.partial(_ffn_down_kernel, n_tiles=n_tiles, chunk=chunk),
        grid=(n_tiles + 1,),
        in_specs=[tile(F), whole(rows_s, F), pl.BlockSpec(memory_space=pl.ANY), tile(D), whole(rows_s, D),
                  whole(1, D)],
        out_specs=[tile(D), whole(rows_s, D)],
        out_shape=[jax.ShapeDtypeStruct((rows, D), F32), jax.ShapeDtypeStruct((rows_s, D), F32)],
        scratch_shapes=[pltpu.VMEM((F, D), BF16), pltpu.VMEM((2, chunk, D), F32), pltpu.SemaphoreType.DMA((2,))],
        compiler_params=_resident_params(("arbitrary",), block_bytes),
        name="ffn_down",
    )(f, f_s, w_down, x1, x1_s, g.reshape(1, D))


def _pick_tile(n, target):
    t = min(n, target)
    while n % t:
        t //= 2
    return t


def _prompt_mixer(x, lp):
    B, S, D = x.shape
    tm = _pick_tile(S, 512)
    u, q, k, v = _inproj(x, lp["g_mix_pre"], lp["w_in"], jnp.arange(S), tm)
    pooled = _pool_prompt(u, lp["w_pool"], lp["pool_scale"], tm)
    attn = _moba_prompt(q, k, v)
    x1, h2 = _outproj(pooled.reshape(B * S, -1), attn.reshape(B * S, -1), x.reshape(B * S, D),
                      lp["w_out"], lp["g_mix_post"], lp["g_ffn_pre"], tm)
    return x1, h2, k, v, u


def _sample_projections(x_tm, pool_hist, page_table, lp, n_seq, n_steps):
    rows = x_tm.shape[0]
    past_len = page_table.shape[1] * PAGE_SIZE
    pos = past_len + jnp.repeat(jnp.arange(n_steps), n_seq)
    u, q, k, v = _inproj(x_tm[None], lp["g_mix_pre"], lp["w_in"], pos, rows)

    def to_seq_major(t):
        H = t.shape[0]
        return t.reshape(H, n_steps, n_seq, HEAD_DIM).transpose(2, 0, 1, 3)

    pooled = _pool_sample(u[0], _seq_to_rows(pool_hist), lp["w_pool"], lp["pool_scale"],
                          n_steps, n_seq, past_len)
    return to_seq_major(q[0]), to_seq_major(k[0]), to_seq_major(v[0]), pooled, u[0]


def _rows_to_seq(t, n_steps, n_seq):
    return t.reshape(n_steps, n_seq, t.shape[-1]).transpose(1, 0, 2)


def _seq_to_rows(t):
    return t.transpose(1, 0, 2).reshape(-1, t.shape[-1])


def _layer(xp, xs_tm, pool_hist, conv_hist, cache_k, cache_v, page_table, lp, n_seq, n_steps):
    B, S, D = xp.shape
    F = lp["w_gate"].shape[1]
    rows_s = xs_tm.shape[0]
    x1p, h2p, kp, vp, up = _prompt_mixer(xp, lp)
    qs, ks, vs, pooled_s, us = _sample_projections(xs_tm, pool_hist, page_table, lp, n_seq, n_steps)
    tm = _pick_tile(S, 512)
    tn = _pick_tile(F, 512)
    fp, tail, attn_s = _ffn_up(h2p.reshape(B, S, D), lp["w_gate"], lp["w_up"], lp["conv_w"], lp["conv_b"],
                               qs, ks, vs, cache_k, cache_v, page_table, tm, tn, _pick_tile(tm, 256))
    attn_tm = attn_s.transpose(2, 0, 1, 3).reshape(rows_s, -1).astype(BF16)
    x1s, h2s = _outproj(pooled_s, attn_tm, xs_tm, lp["w_out"], lp["g_mix_post"], lp["g_ffn_pre"], rows_s)
    fs, gts = _ffn_up_sample(h2s, lp["w_gate"], lp["w_up"], lp["conv_w"], lp["conv_b"],
                             _seq_to_rows(conv_hist), n_steps, n_seq, tn)
    yp, ys = _ffn_down(fp.reshape(B * S, F), fs, lp["w_down"], x1p, x1s, lp["g_ffn_post"],
                       _pick_tile(B * S, 256))
    pool_p = up[:, S - POOL_HIST:, :]
    conv_p = tail[:, V7X_SUBLANES - (CONV_W - 1):, :]
    pool_s = jnp.concatenate([pool_hist, _rows_to_seq(us, n_steps, n_seq)], axis=1)[:, n_steps:]
    conv_s = jnp.concatenate([conv_hist, _rows_to_seq(gts, n_steps, n_seq)], axis=1)[:, n_steps:]
    return yp.reshape(B, S, D), ys, (kp, vp, ks, vs, pool_p, pool_s, conv_p, conv_s)


def kernel(x_prompt, x_sample, cache_k, cache_v, state_pool, state_conv, page_table,
           w_in, w_pool, pool_scale, w_out, g_mix_pre, g_mix_post,
           w_gate, w_up, conv_w, conv_b, w_down, g_ffn_pre, g_ffn_post):
    depth = w_in.shape[0]
    n_seq, n_steps, D = x_sample.shape
    assert w_in.shape[2] == 4 * pool_scale.shape[1], "pooling and attention widths must match"
    yp = x_prompt
    ys = x_sample.transpose(1, 0, 2).reshape(n_steps * n_seq, D)
    outs = [[] for _ in range(8)]
    for l in range(depth):
        lp = {"w_in": w_in[l].astype(BF16), "w_pool": w_pool[l].astype(BF16), "pool_scale": pool_scale[l],
              "w_out": w_out[l].astype(BF16), "g_mix_pre": g_mix_pre[l], "g_mix_post": g_mix_post[l],
              "w_gate": w_gate[l], "w_up": w_up[l], "conv_w": conv_w[l],
              "conv_b": conv_b[l], "w_down": w_down[l], "g_ffn_pre": g_ffn_pre[l],
              "g_ffn_post": g_ffn_post[l]}
        yp, ys, states = _layer(yp, ys, state_pool[l], state_conv[l], cache_k[l], cache_v[l],
                                page_table, lp, n_seq, n_steps)
        for lst, val in zip(outs, states):
            lst.append(val)
    y_sample = ys.reshape(n_steps, n_seq, D).transpose(1, 0, 2)
    return (yp, y_sample) + tuple(jnp.stack(o) for o in outs)
```

```python
import functools

import jax
import jax.numpy as jnp
from jax import lax
from jax.experimental import pallas as pl
from jax.experimental.pallas import tpu as pltpu

F32 = jnp.float32
BF16 = jnp.bfloat16

POOL_WINDOWS = (2, 4, 8, 16)
POOL_HIST = max(POOL_WINDOWS) - 1
HEAD_DIM = 128
ROT_DIM = HEAD_DIM // 4
ROPE_THETA = 500000.0
MOBA_BLOCK = 256
MOBA_TOPK = 3
QUERY_BLOCKS_PER_DOT = 2
PAGE_SIZE = 128
CONV_W = 3
EPS = 1e-6
NEG_INF = float("-inf")
LOG2E = 1.4426950408889634

V7X_SUBLANES = 8
V7X_VMEM_BYTES = 64 * 1024 * 1024
VMEM_RESERVE_BYTES = 4 * 1024 * 1024


def _vmem_limit(block_bytes):
    return int(min(2 * block_bytes, V7X_VMEM_BYTES - VMEM_RESERVE_BYTES))


def _params(semantics, block_bytes):
    return pltpu.CompilerParams(dimension_semantics=semantics,
                                vmem_limit_bytes=_vmem_limit(block_bytes))


def _resident_params(semantics, block_bytes):
    limit = int(min(block_bytes + VMEM_RESERVE_BYTES, V7X_VMEM_BYTES - VMEM_RESERVE_BYTES))
    return pltpu.CompilerParams(dimension_semantics=semantics, vmem_limit_bytes=limit)


def _rms(x, g):
    return x * lax.rsqrt(jnp.mean(x * x, axis=-1, keepdims=True) + EPS) * g


def _dot(a, b):
    return jnp.dot(a, b, preferred_element_type=F32)


def _dot_nt(a, b, precision=None):
    return lax.dot_general(a, b, (((1,), (1,)), ((), ())), precision=precision,
                           preferred_element_type=F32)


def _rope_tables(pos):
    half = ROT_DIM // 2
    inv = ROPE_THETA ** (-jnp.arange(half, dtype=F32) * (2.0 / ROT_DIM))
    ang = pos.astype(F32)[:, None] * inv[None, :]
    cos, sin = jnp.cos(ang), jnp.sin(ang)
    n = pos.shape[0]
    c = jnp.concatenate([cos, cos, jnp.ones((n, HEAD_DIM - ROT_DIM), F32)], axis=1)
    s_lo = jnp.concatenate([-sin, jnp.zeros((n, HEAD_DIM - half), F32)], axis=1)
    s_hi = jnp.concatenate([jnp.zeros((n, half), F32), sin,
                            jnp.zeros((n, HEAD_DIM - ROT_DIM), F32)], axis=1)
    return c, s_lo, s_hi


def _round_weight_once(w_hbm, wb_ref, stage_ref, sem, chunk):
    n_chunks = wb_ref.shape[0] // chunk

    def w_copy(c):
        return pltpu.make_async_copy(w_hbm.at[pl.ds(c * chunk, chunk), :], stage_ref.at[c % 2], sem.at[c % 2])

    w_copy(0).start()
    for c in range(n_chunks):
        if c + 1 < n_chunks:
            w_copy(c + 1).start()
        w_copy(c).wait()
        wb_ref[c * chunk:(c + 1) * chunk, :] = stage_ref[c % 2].astype(BF16)


def _inproj_kernel(x_ref, xs_ref, g_ref, w_hbm, c_ref, slo_ref, shi_ref, cs_ref, slos_ref, shis_ref,
                   u_ref, q_ref, k_ref, v_ref, us_ref, qs_ref, ks_ref, vs_ref,
                   wb_ref, hn_ref, stage_ref, sem, *, n_heads, width, n_tiles, chunk):
    i = pl.program_id(0)
    pair = 2 * HEAD_DIM

    @pl.when(i == 0)
    def _():
        _round_weight_once(w_hbm, wb_ref, stage_ref, sem, chunk)

    def project(x, tables, outs):
        c, s_lo, s_hi = (t[...] for t in tables)
        u_out, q_out, k_out, v_out = outs
        rows = x.shape[0]
        hn_ref[0:rows, :] = _rms(x, g_ref[...]).astype(BF16)

        def rope(h):
            return (h * c + pltpu.roll(h, HEAD_DIM - ROT_DIM // 2, 1) * s_lo
                    + pltpu.roll(h, ROT_DIM // 2, 1) * s_hi)

        for p in range(width // pair):
            u_out[0, :, p * pair:(p + 1) * pair] = _dot(hn_ref[0:rows, :], wb_ref[:, p * pair:(p + 1) * pair])
        for out_ref, part, rotary in ((q_out, 1, True), (k_out, 2, True), (v_out, 3, False)):
            for p in range(n_heads // 2):
                col = part * width + p * pair
                res = _dot(hn_ref[0:rows, :], wb_ref[:, col:col + pair])
                for half in range(2):
                    h = res[:, half * HEAD_DIM:(half + 1) * HEAD_DIM]
                    out_ref[0, 2 * p + half] = rope(h) if rotary else h

    @pl.when(i < n_tiles)
    def _():
        project(x_ref[0], (c_ref, slo_ref, shi_ref), (u_ref, q_ref, k_ref, v_ref))

    @pl.when(i == n_tiles)
    def _():
        project(xs_ref[0], (cs_ref, slos_ref, shis_ref), (us_ref, qs_ref, ks_ref, vs_ref))


def _inproj(x, x_s, g, w_in, pos, pos_s, tm):
    B, S, D = x.shape
    rows_s = x_s.shape[1]
    width = w_in.shape[1] // 4
    n_heads = width // HEAD_DIM
    assert n_heads % 2 == 0 and rows_s <= tm
    n_s = S // tm
    n_tiles = B * n_s
    chunk = _pick_tile(D, 256)

    def tile(i):
        ii = jnp.minimum(i, n_tiles - 1)
        return ii // n_s, lax.rem(ii, n_s)

    tab_spec = pl.BlockSpec((tm, HEAD_DIM), lambda i: (tile(i)[1], 0))
    hm_spec = pl.BlockSpec((1, n_heads, tm, HEAD_DIM), lambda i: (tile(i)[0], 0, tile(i)[1], 0))
    hm_shape = jax.ShapeDtypeStruct((B, n_heads, S, HEAD_DIM), F32)
    whole = lambda shape: pl.BlockSpec(shape, lambda i: (0,) * len(shape))
    stab_spec = whole((rows_s, HEAD_DIM))
    shm_spec = whole((1, n_heads, rows_s, HEAD_DIM))
    shm_shape = jax.ShapeDtypeStruct((1, n_heads, rows_s, HEAD_DIM), F32)
    block_bytes = (2 * tm * D * 4 + tm * D * 2 + D * 4 * width * 2 + 2 * chunk * 4 * width * 4
                   + 2 * 4 * tm * width * 4 + 2 * rows_s * (D + 4 * width) * 4)
    return pl.pallas_call(
        functools.partial(_inproj_kernel, n_heads=n_heads, width=width, n_tiles=n_tiles, chunk=chunk),
        grid=(n_tiles + 1,),
        in_specs=[
            pl.BlockSpec((1, tm, D), lambda i: (tile(i)[0], tile(i)[1], 0)),
            whole((1, rows_s, D)),
            whole((1, D)),
            pl.BlockSpec(memory_space=pl.ANY),
            tab_spec, tab_spec, tab_spec, stab_spec, stab_spec, stab_spec,
        ],
        out_specs=[pl.BlockSpec((1, tm, width), lambda i: (tile(i)[0], tile(i)[1], 0)), hm_spec, hm_spec, hm_spec,
                   whole((1, rows_s, width)), shm_spec, shm_spec, shm_spec],
        out_shape=[jax.ShapeDtypeStruct((B, S, width), F32), hm_shape, hm_shape, hm_shape,
                   jax.ShapeDtypeStruct((1, rows_s, width), F32), shm_shape, shm_shape, shm_shape],
        scratch_shapes=[pltpu.VMEM((D, 4 * width), BF16), pltpu.VMEM((tm, D), BF16),
                        pltpu.VMEM((2, chunk, 4 * width), F32), pltpu.SemaphoreType.DMA((2,))],
        compiler_params=_resident_params(("arbitrary",), block_bytes),
        name="inproj",
    )(x, x_s, g.reshape(1, D), w_in, *_rope_tables(pos), *_rope_tables(pos_s))


def _shift_rows(x, prev8, k):
    pad = V7X_SUBLANES
    rolled = pltpu.roll(x, k, 0)
    row8 = lax.broadcasted_iota(jnp.int32, prev8.shape, 0)
    head = jnp.where(row8 < k, pltpu.roll(prev8, k % pad, 0) if k % pad else prev8, rolled[0:pad])
    return jnp.concatenate([head, rolled[pad:]], axis=0)


def _pool_kernel(u_ref, w_ref, scale_ref, o_ref, halo_ref, *, ts, group):
    s = pl.program_id(1)
    pad = V7X_SUBLANES

    @pl.when(s == 0)
    def _():
        halo_ref[...] = jnp.zeros(halo_ref.shape, F32)

    pos = s * ts + lax.broadcasted_iota(jnp.int32, (ts, 1), 0)
    slot = 0
    for gi, w in enumerate(POOL_WINDOWS):
        cols = slice(gi * group, (gi + 1) * group)
        x = u_ref[0, :, cols]
        acc = x
        span = 1
        while span < w:
            prev8 = halo_ref[slot]
            halo_ref[slot] = acc[ts - pad:ts, :]
            acc = acc + _shift_rows(acc, prev8, span)
            slot += 1
            span *= 2
        cnt = jnp.minimum(pos + 1, w).astype(F32)
        d = (acc / cnt - x).astype(BF16)
        y = _dot(d, w_ref[gi]) * scale_ref[:, cols]
        o_ref[0, :, cols] = y.astype(BF16)


def _pool_prompt(u, w_pool_bf, scale, ts):
    B, S, C = u.shape
    n_groups, group, _ = w_pool_bf.shape
    assert all(w & (w - 1) == 0 and w <= 2 * V7X_SUBLANES for w in POOL_WINDOWS)
    n_levels = sum(w.bit_length() - 1 for w in POOL_WINDOWS)
    block_bytes = 2 * ts * C * 4 + 2 * n_groups * group * group * 2 + 2 * ts * C * 2 + 6 * ts * group * 4
    return pl.pallas_call(
        functools.partial(_pool_kernel, ts=ts, group=group),
        grid=(B, S // ts),
        in_specs=[
            pl.BlockSpec((1, ts, C), lambda b, s: (b, s, 0)),
            pl.BlockSpec((n_groups, group, group), lambda b, s: (0, 0, 0)),
            pl.BlockSpec((1, C), lambda b, s: (0, 0)),
        ],
        out_specs=pl.BlockSpec((1, ts, C), lambda b, s: (b, s, 0)),
        out_shape=jax.ShapeDtypeStruct((B, S, C), BF16),
        scratch_shapes=[pltpu.VMEM((n_levels, V7X_SUBLANES, group), F32)],
        compiler_params=_params(("arbitrary", "arbitrary"), block_bytes),
        name="pool_prompt",
    )(u, w_pool_bf, scale.reshape(1, C))


def _pool_sample_kernel(u_ref, hist_ref, w_ref, scale_ref, o_ref, *, n_steps, n_seq, group, past_len):
    def ext(i):
        if i < POOL_HIST:
            return hist_ref[i * n_seq:(i + 1) * n_seq, :]
        return u_ref[(i - POOL_HIST) * n_seq:(i - POOL_HIST + 1) * n_seq, :]

    for gi, w in enumerate(POOL_WINDOWS):
        cols = slice(gi * group, (gi + 1) * group)
        ds = []
        for t in range(n_steps):
            cur = ext(POOL_HIST + t)[:, cols]
            acc = cur
            for back in range(1, w):
                acc = acc + ext(POOL_HIST + t - back)[:, cols]
            cnt = float(min(past_len + t + 1, w))
            ds.append((acc / cnt - cur).astype(BF16))
        d = jnp.concatenate(ds, axis=0)
        y = _dot(d, w_ref[gi]) * scale_ref[:, cols]
        o_ref[:, cols] = y.astype(BF16)


def _pool_sample(u_tm, hist_tm, w_pool_bf, scale, n_steps, n_seq, past_len):
    rows, C = u_tm.shape
    n_groups, group, _ = w_pool_bf.shape
    block_bytes = 2 * (rows + hist_tm.shape[0]) * C * 4 + 2 * n_groups * group * group * 2 + 2 * rows * C * 2
    return pl.pallas_call(
        functools.partial(_pool_sample_kernel, n_steps=n_steps, n_seq=n_seq, group=group,
                          past_len=past_len),
        out_shape=jax.ShapeDtypeStruct((rows, C), BF16),
        compiler_params=pltpu.CompilerParams(vmem_limit_bytes=_vmem_limit(block_bytes)),
        name="pool_sample",
    )(u_tm, hist_tm, w_pool_bf, scale.reshape(1, C))


class _PerHead:
    def __init__(self, refs):
        self.refs = refs

    def __getitem__(self, idx):
        return self.refs[idx[0]][idx[1:]] if isinstance(idx, tuple) else self.refs[idx][...]

    def __setitem__(self, idx, value):
        self.refs[idx[0]][idx[1:]] = value


def _moba_prompt_kernel(q_ref, k_ref, v_ref, o_ref, *scratch, seq, hp):
    blk = MOBA_BLOCK
    nb = seq // blk
    n_sel = min(MOBA_TOPK, nb - 1)
    scale = HEAD_DIM ** -0.5
    kb_ref, vt_ref, km_ref, s_ref, p_ref = (_PerHead(scratch[kind * hp:(kind + 1) * hp]) for kind in range(5))

    for hh in range(hp):
        for n in range(nb):
            rows = slice(n * blk, (n + 1) * blk)
            kn = k_ref[0, hh, rows, :]
            kb_ref[hh, rows, :] = kn.astype(BF16)
            km_ref[hh, n:n + 1, :] = jnp.sum(kn, axis=0, keepdims=True) * (1.0 / blk)
            vt_ref[hh, :, rows] = v_ref[0, hh, rows, :].T.astype(BF16)

    blk_id = lax.broadcasted_iota(jnp.int32, (nb, blk), 0)
    key_i = lax.broadcasted_iota(jnp.int32, (blk, blk), 0)
    qry_i = lax.broadcasted_iota(jnp.int32, (blk, blk), 1)
    causal_bias = jnp.where(key_i <= qry_i, 0.0, NEG_INF).astype(F32)

    def scores(hh, qblocks):
        first, last = qblocks[0], qblocks[-1]
        nk = (last + 1) * blk
        width = len(qblocks) * blk
        qg = q_ref[0, hh, first * blk:nk, :]
        s_ref[hh, 0:nk, 0:width] = _dot_nt(kb_ref[hh, 0:nk, :], (qg * (scale * LOG2E)).astype(BF16))
        return _dot_nt(km_ref[hh], qg, precision=lax.Precision.HIGHEST) if last > 0 else None

    def softmax(hh, qblocks, gate):
        last = qblocks[-1]
        sums = []
        for a, i in enumerate(qblocks):
            cols = slice(a * blk, (a + 1) * blk)
            own = slice(i * blk, (i + 1) * blk)
            biases = []
            if i > 0:
                gate_a = gate[:, cols]
                valid = jnp.where(blk_id < i, 1.0, 0.0)
                for n in range(i):
                    gn = gate_a[n:n + 1, :]
                    beats = jnp.where(blk_id < n, jnp.where(gate_a >= gn, 1.0, 0.0),
                                      jnp.where(gate_a > gn, 1.0, 0.0))
                    rank = jnp.sum(beats * valid, axis=0, keepdims=True)
                    biases.append(jnp.where(rank < n_sel, 0.0, NEG_INF).astype(F32))

            s_own = s_ref[hh, own, cols] + causal_bias
            m = jnp.max(s_own, axis=0, keepdims=True)
            for n in range(i):
                m = jnp.maximum(m, jnp.max(s_ref[hh, n * blk:(n + 1) * blk, cols], axis=0, keepdims=True)
                                + biases[n])

            p = jnp.exp2(s_own - m)
            l = jnp.sum(p, axis=0, keepdims=True)
            p_ref[hh, own, cols] = p.astype(BF16)
            for n in range(i):
                rows = slice(n * blk, (n + 1) * blk)
                p = jnp.exp2(s_ref[hh, rows, cols] + (biases[n] - m))
                l = l + jnp.sum(p, axis=0, keepdims=True)
                p_ref[hh, rows, cols] = p.astype(BF16)
            for n in range(i + 1, last + 1):
                p_ref[hh, n * blk:(n + 1) * blk, cols] = jnp.zeros((blk, blk), BF16)
            sums.append(l)
        return sums

    def output(hh, qblocks, sums):
        nk = (qblocks[-1] + 1) * blk
        width = len(qblocks) * blk
        o_t = _dot(vt_ref[hh, :, 0:nk], p_ref[hh, 0:nk, 0:width])
        for a, i in enumerate(qblocks):
            o_a = o_t[:, a * blk:(a + 1) * blk] / sums[a]
            o_ref[0, i * blk:(i + 1) * blk, hh * HEAD_DIM:(hh + 1) * HEAD_DIM] = o_a.T.astype(BF16)

    items = [(hh, list(range(first, min(first + QUERY_BLOCKS_PER_DOT, nb))))
             for first in range(0, nb, QUERY_BLOCKS_PER_DOT) for hh in range(hp)]
    gate = scores(*items[0])
    for cur, nxt in zip(items, items[1:] + [None]):
        next_gate = scores(*nxt) if nxt is not None else None
        output(*cur, softmax(*cur, gate))
        gate = next_gate


def _moba_prompt(q, k, v):
    B, H, S, Dh = q.shape
    nb = S // MOBA_BLOCK
    hp = 2 if H % 2 == 0 else 1
    width = QUERY_BLOCKS_PER_DOT * MOBA_BLOCK
    in_spec = pl.BlockSpec((1, hp, S, Dh), lambda b, g: (b, g, 0, 0))
    block_bytes = hp * (2 * 3 * S * Dh * 4 + 2 * S * Dh * 2 + 2 * S * Dh * 2 + S * width * 6)
    return pl.pallas_call(
        functools.partial(_moba_prompt_kernel, seq=S, hp=hp),
        grid=(B, H // hp),
        in_specs=[in_spec, in_spec, in_spec],
        out_specs=pl.BlockSpec((1, S, hp * Dh), lambda b, g: (b, 0, g)),
        out_shape=jax.ShapeDtypeStruct((B, S, H * Dh), BF16),
        scratch_shapes=(
            [pltpu.VMEM((S, Dh), BF16)] * hp
            + [pltpu.VMEM((Dh, S), BF16)] * hp
            + [pltpu.VMEM((nb, Dh), F32)] * hp
            + [pltpu.VMEM((S, width), F32)] * hp
            + [pltpu.VMEM((S, width), BF16)] * hp
        ),
        compiler_params=_params(("arbitrary", "arbitrary"), block_bytes),
        name="moba_prompt",
    )(q, k, v)


class _PagedAttention:
    def __init__(self, pt_ref, qc_ref, qp_ref, kn_ref, vn_ref, ck_ref, cv_ref, o_ref,
                 kbuf, ksel, vbuf, km_ref, q8_ref, sel_ref, ksem, vsem, *, n_pages, n_groups, hp, n_new):
        self.__dict__.update(locals())
        self.ppb = MOBA_BLOCK // PAGE_SIZE
        self.nblk = n_pages // self.ppb
        self.n_sel = min(MOBA_TOPK, self.nblk)

    def unit_bh(self, u):
        return u // self.n_groups, lax.rem(u, self.n_groups) * self.hp

    def k_copy(self, u, page):
        bb, h0 = self.unit_bh(u)
        sl = lax.rem(u, 2)
        return pltpu.make_async_copy(self.ck_ref.at[self.pt_ref[bb, page], pl.ds(h0, self.hp)],
                                     self.kbuf.at[sl, :, pl.ds(page * PAGE_SIZE, PAGE_SIZE), :],
                                     self.ksem.at[sl])

    def start_k(self, u, part=0, n_parts=1):
        per = self.n_pages // n_parts
        if n_parts == 1:
            def body(page, carry):
                self.k_copy(u, page).start()
                return carry
            lax.fori_loop(0, self.n_pages, body, 0, unroll=8)
        else:
            for page in range(part * per, (part + 1) * per if part + 1 < n_parts else self.n_pages):
                self.k_copy(u, page).start()

    def wait_k(self, u):
        for page in range(self.n_pages):
            self.k_copy(u, page).wait()

    def sel_index(self, hh, j, c):
        return (hh * self.n_new + j) * self.n_sel + c

    def v_copies(self, u, heads):
        bb, h0 = self.unit_bh(u)
        sl = lax.rem(u, 2)
        for hh in heads:
            for j in range(self.n_new):
                for c in range(self.n_sel):
                    blk = self.sel_ref[sl, self.sel_index(hh, j, c)]
                    for pg in range(self.ppb):
                        page = self.pt_ref[bb, blk * self.ppb + pg]
                        dst = ((j * self.n_sel + c) * self.ppb + pg) * PAGE_SIZE
                        yield pltpu.make_async_copy(self.cv_ref.at[page, h0 + hh],
                                                    self.vbuf.at[sl, hh, pl.ds(dst, PAGE_SIZE), :],
                                                    self.vsem.at[sl])

    def start_v(self, u, heads):
        for cp in self.v_copies(u, heads):
            cp.start()

    def wait_v(self, u):
        for cp in self.v_copies(u, range(self.hp)):
            cp.wait()

    def block_means(self, u, heads):
        sl = lax.rem(u, 2)
        for hh in heads:
            for n in range(self.nblk):
                kn_blk = self.kbuf[sl, hh, n * MOBA_BLOCK:(n + 1) * MOBA_BLOCK, :]
                self.km_ref[hh, n:n + 1, :] = jnp.sum(kn_blk, axis=0, keepdims=True) * (1.0 / MOBA_BLOCK)

    def select(self, u, heads):
        sl = lax.rem(u, 2)
        for hh in heads:
            self.q8_ref[hh] = jnp.zeros(self.q8_ref.shape[1:], F32)
            self.q8_ref[hh, 0:self.n_new, :] = self.qc_ref[0, hh]
            gate = _dot_nt(self.q8_ref[hh], self.km_ref[hh], precision=lax.Precision.HIGHEST)
            lane = lax.broadcasted_iota(jnp.int32, gate.shape, 1)
            picks = [jnp.zeros((gate.shape[0], 1), F32) for _ in range(self.n_sel)]
            for n in range(self.nblk):
                gn = gate[:, n:n + 1]
                beats = jnp.where(lane < n, jnp.where(gate >= gn, 1.0, 0.0), jnp.where(gate > gn, 1.0, 0.0))
                rank = jnp.sum(beats, axis=1, keepdims=True)
                for c in range(self.n_sel):
                    picks[c] = jnp.where(rank == float(c), float(n), picks[c])
            for j in range(self.n_new):
                for c in range(self.n_sel):
                    blk = picks[c][j, 0].astype(jnp.int32)
                    self.sel_ref[sl, self.sel_index(hh, j, c)] = blk
                    row0 = (j * self.n_sel + c) * MOBA_BLOCK
                    start = pl.multiple_of(blk * MOBA_BLOCK, MOBA_BLOCK)
                    self.ksel[sl, hh, row0:row0 + MOBA_BLOCK, :] = self.kbuf[sl, hh, pl.ds(start, MOBA_BLOCK), :]

    def attend(self, u, heads):
        sl = lax.rem(u, 2)
        scale = HEAD_DIM ** -0.5
        new_row = lax.broadcasted_iota(jnp.int32, (self.n_new, 1), 0)
        for hh in heads:
            q = self.qp_ref[0, hh]
            kn = self.kn_ref[0, hh]
            vn = self.vn_ref[0, hh]
            for j in range(self.n_new):
                qj = q[j:j + 1, :] * (scale * LOG2E)
                rows = [slice((j * self.n_sel + c) * MOBA_BLOCK, (j * self.n_sel + c + 1) * MOBA_BLOCK)
                        for c in range(self.n_sel)]
                s_sel = [jnp.sum(self.ksel[sl, hh, r, :] * qj, axis=1, keepdims=True) for r in rows]
                s_new = jnp.sum(kn * qj, axis=1, keepdims=True)
                s_new = jnp.where(new_row <= j, s_new, NEG_INF)
                m = jnp.max(s_new, axis=0, keepdims=True)
                for s in s_sel:
                    m = jnp.maximum(m, jnp.max(s, axis=0, keepdims=True))
                p_new = jnp.exp2(s_new - m)
                l = jnp.sum(p_new, axis=0, keepdims=True)
                acc = jnp.sum(p_new * vn, axis=0, keepdims=True)
                for s, r in zip(s_sel, rows):
                    p = jnp.exp2(s - m)
                    l = l + jnp.sum(p, axis=0, keepdims=True)
                    acc = acc + jnp.sum(p * self.vbuf[sl, hh, r, :], axis=0, keepdims=True)
                self.o_ref[0, hh, j:j + 1, :] = acc / l


def _paged_attention_scratch(n_pages, hp, n_new):
    nblk = n_pages // (MOBA_BLOCK // PAGE_SIZE)
    n_sel = min(MOBA_TOPK, nblk)
    picked_rows = n_new * n_sel * MOBA_BLOCK
    shapes = [
        pltpu.VMEM((2, hp, n_pages * PAGE_SIZE, HEAD_DIM), F32),
        pltpu.VMEM((2, hp, picked_rows, HEAD_DIM), F32),
        pltpu.VMEM((2, hp, picked_rows, HEAD_DIM), F32),
        pltpu.VMEM((hp, nblk, HEAD_DIM), F32),
        pltpu.VMEM((hp, V7X_SUBLANES, HEAD_DIM), F32),
        pltpu.SMEM((2, hp * n_new * n_sel), jnp.int32),
        pltpu.SemaphoreType.DMA((2,)),
        pltpu.SemaphoreType.DMA((2,)),
    ]
    n_bytes = 4 * HEAD_DIM * 2 * hp * (n_pages * PAGE_SIZE + 2 * picked_rows)
    return shapes, n_bytes


def _outproj_kernel(pooled_ref, attn_ref, x_ref, w_ref, gpost_ref, gffn_ref, x1_ref, h2_ref, *, split):
    mix = _dot(pooled_ref[...], w_ref[0:split, :]) + _dot(attn_ref[...], w_ref[split:, :])
    x1 = x_ref[...] + _rms(mix, gpost_ref[...])
    x1_ref[...] = x1
    h2_ref[...] = _rms(x1, gffn_ref[...]).astype(BF16)


def _outproj(pooled, attn, x, w_bf, g_post, g_ffn, tm):
    rows, D = x.shape
    split = pooled.shape[1]
    wa = attn.shape[1]
    row_spec = lambda width: pl.BlockSpec((tm, width), lambda i: (i, 0))
    vec_spec = pl.BlockSpec((1, D), lambda i: (0, 0))
    block_bytes = 2 * tm * (split + wa) * 2 + 2 * tm * D * 4 + 2 * (split + wa) * D * 2 \
        + 2 * tm * D * 6 + 2 * tm * D * 4
    return pl.pallas_call(
        functools.partial(_outproj_kernel, split=split),
        grid=(rows // tm,),
        in_specs=[row_spec(split), row_spec(wa), row_spec(D),
                  pl.BlockSpec((split + wa, D), lambda i: (0, 0)), vec_spec, vec_spec],
        out_specs=[row_spec(D), row_spec(D)],
        out_shape=[jax.ShapeDtypeStruct((rows, D), F32), jax.ShapeDtypeStruct((rows, D), BF16)],
        compiler_params=_params(("arbitrary",), block_bytes),
        name="outproj",
    )(pooled, attn, x, w_bf, g_post.reshape(1, D), g_ffn.reshape(1, D))


def _gelu_tanh(c):
    return c * (0.5 * (1.0 + jnp.tanh(0.7978845608028654 * (c + 0.044715 * (c * c * c)))))


def _conv_gelu_gate(gt, prev8, up, cw_ref, cb_ref):
    c = cb_ref[...]
    for i in range(CONV_W):
        back = CONV_W - 1 - i
        c = c + (_shift_rows(gt, prev8, back) if back else gt) * cw_ref[i:i + 1, :]
    return _gelu_tanh(c) * up


def _ffn_up_kernel(pt_ref, h_ref, wg_ref, wu_ref, cw_ref, cb_ref, qc_ref, qp_ref, kn_ref, vn_ref, ck_ref, cv_ref,
                   f_ref, tail_ref, o_ref, wgb_ref, wub_ref, halo_ref, *attn_scratch,
                   tm, sub, n_b, n_s, n_units, attn_params):
    j, b, s = pl.program_id(0), pl.program_id(1), pl.program_id(2)
    t = (j * n_b + b) * n_s + s
    pad = V7X_SUBLANES
    last_sub = tm // sub - 1
    attn = _PagedAttention(pt_ref, qc_ref, qp_ref, kn_ref, vn_ref, ck_ref, cv_ref, o_ref,
                           *attn_scratch, **attn_params)

    @pl.when((b == 0) & (s == 0))
    def _():
        wgb_ref[...] = wg_ref[...].astype(BF16)
        wub_ref[...] = wu_ref[...].astype(BF16)

    @pl.when(s == 0)
    def _():
        halo_ref[...] = jnp.zeros(halo_ref.shape, F32)

    @pl.when(t == 0)
    def _():
        attn.start_k(t)

    @pl.when(t < n_units)
    def _():
        attn.wait_k(t)

    @pl.when((t >= 1) & (t <= n_units))
    def _():
        attn.wait_v(t - 1)

    def gate_up(before=None, middle=None, after=None):
        before, middle, after = before or {}, middle or {}, after or {}
        prev8 = halo_ref[...]
        half = sub // 2
        for r in range(tm // sub):
            for stage in before.get(r, ()):
                stage()
            row0 = r * sub
            if r in middle:
                lo = h_ref[0, row0:row0 + half, :]
                gt_lo, up_lo = _dot(lo, wgb_ref[...]), _dot(lo, wub_ref[...])
                for stage in middle[r]:
                    stage()
                hi = h_ref[0, row0 + half:row0 + sub, :]
                gt = jnp.concatenate([gt_lo, _dot(hi, wgb_ref[...])], axis=0)
                up = jnp.concatenate([up_lo, _dot(hi, wub_ref[...])], axis=0)
            else:
                hr = h_ref[0, row0:row0 + sub, :]
                gt, up = _dot(hr, wgb_ref[...]), _dot(hr, wub_ref[...])
            for stage in after.get(r, ()):
                stage()
            f_ref[0, row0:row0 + sub, :] = _conv_gelu_gate(gt, prev8, up, cw_ref, cb_ref).astype(BF16)
            prev8 = gt[sub - pad:sub, :]
        halo_ref[...] = prev8
        tail_ref[0] = prev8

    heads = tuple(range(attn.hp))
    k_parts = 4

    def next_k(part):
        return lambda: attn.start_k(t + 1, part, k_parts)

    def means():
        attn.block_means(t, heads)

    def select_and_fetch():
        attn.select(t, heads)
        attn.start_v(t, heads)

    def attend_previous():
        attn.attend(t - 1, heads)

    def stages(select, attend, fetch_next):
        before, middle, after = {}, {}, {}
        if fetch_next:
            before.setdefault(0, []).append(next_k(0))
        if select:
            before.setdefault(0, []).append(means)
            middle.setdefault(0, []).append(select_and_fetch)
        if fetch_next:
            middle.setdefault(0, []).append(next_k(1))
            before.setdefault(last_sub, []).append(next_k(2))
        if attend:
            before.setdefault(last_sub, []).append(attend_previous)
        if fetch_next:
            after.setdefault(last_sub, []).append(next_k(3))
        return before, middle, after

    @pl.when(t == 0)
    def _():
        gate_up(*stages(select=True, attend=False, fetch_next=True))

    @pl.when((t >= 1) & (t + 1 < n_units))
    def _():
        gate_up(*stages(select=True, attend=True, fetch_next=True))

    @pl.when(t + 1 == n_units)
    def _():
        gate_up(*stages(select=True, attend=True, fetch_next=False))

    @pl.when(t == n_units)
    def _():
        gate_up(*stages(select=False, attend=True, fetch_next=False))

    @pl.when(t > n_units)
    def _():
        gate_up()


def _ffn_up(h2, w_gate, w_up, conv_w, conv_b, q, k_new, v_new, cache_k, cache_v, page_table, tm, tn, sub):
    B, S, D = h2.shape
    F = w_gate.shape[1]
    Bd, H, L, Dh = q.shape
    n_pages = page_table.shape[1]
    ppb = MOBA_BLOCK // PAGE_SIZE
    assert n_pages % ppb == 0, "past length must be a whole number of MoBA blocks"
    assert n_pages >= ppb and L <= V7X_SUBLANES
    hp = 2 if H % 2 == 0 else 1
    n_groups = H // hp
    n_units = Bd * n_groups
    n_b, n_s = B, S // tm
    assert (F // tn) * n_b * n_s > n_units >= 2, "not enough grid steps to host the sample attention"
    pad = V7X_SUBLANES

    def unit_spec(lag):
        def index_map(j, b, s, pt):
            u = jnp.clip((j * n_b + b) * n_s + s - lag, 0, n_units - 1)
            return (u // n_groups, lax.rem(u, n_groups), 0, 0)
        return pl.BlockSpec((1, hp, L, Dh), index_map)

    w_spec = pl.BlockSpec((D, tn), lambda j, b, s, pt: (0, j))
    any_spec = pl.BlockSpec(memory_space=pl.ANY)
    attn_scratch, attn_bytes = _paged_attention_scratch(n_pages, hp, L)
    block_bytes = (2 * tm * D * 2 + 2 * 2 * D * tn * 4 + 2 * D * tn * 2 + 2 * tm * tn * 2 + 8 * sub * tn * 4
                   + attn_bytes)
    grid_spec = pltpu.PrefetchScalarGridSpec(
        num_scalar_prefetch=1,
        grid=(F // tn, n_b, n_s),
        in_specs=[
            pl.BlockSpec((1, tm, D), lambda j, b, s, pt: (b, s, 0)),
            w_spec, w_spec,
            pl.BlockSpec((CONV_W, tn), lambda j, b, s, pt: (0, j)),
            pl.BlockSpec((1, tn), lambda j, b, s, pt: (0, j)),
            unit_spec(0), unit_spec(1), unit_spec(1), unit_spec(1), any_spec, any_spec,
        ],
        out_specs=[pl.BlockSpec((1, tm, tn), lambda j, b, s, pt: (b, s, j)),
                   pl.BlockSpec((1, pad, tn), lambda j, b, s, pt: (b, 0, j)),
                   unit_spec(1)],
        scratch_shapes=[pltpu.VMEM((D, tn), BF16), pltpu.VMEM((D, tn), BF16), pltpu.VMEM((pad, tn), F32)]
        + attn_scratch,
    )
    return pl.pallas_call(
        functools.partial(_ffn_up_kernel, tm=tm, sub=sub, n_b=n_b, n_s=n_s, n_units=n_units,
                          attn_params=dict(n_pages=n_pages, n_groups=n_groups, hp=hp, n_new=L)),
        grid_spec=grid_spec,
        out_shape=[jax.ShapeDtypeStruct((B, S, F), BF16), jax.ShapeDtypeStruct((B, pad, F), F32),
                   jax.ShapeDtypeStruct((Bd, H, L, Dh), F32)],
        compiler_params=_resident_params(("arbitrary", "arbitrary", "arbitrary"), block_bytes),
        name="ffn_up",
    )(page_table, h2, w_gate, w_up, conv_w, conv_b.reshape(1, F), q, q, k_new, v_new, cache_k, cache_v)


def _ffn_up_sample_kernel(h_ref, wg_ref, wu_ref, cw_ref, cb_ref, hist_ref, f_ref, gt_ref, *, n_steps, n_seq):
    h = h_ref[...]
    gt = _dot(h, wg_ref[...].astype(BF16))
    up = _dot(h, wu_ref[...].astype(BF16))
    gt_ref[...] = gt
    ext = [hist_ref[i * n_seq:(i + 1) * n_seq, :] for i in range(CONV_W - 1)]
    ext += [gt[t * n_seq:(t + 1) * n_seq, :] for t in range(n_steps)]
    for t in range(n_steps):
        c = cb_ref[...]
        for i in range(CONV_W):
            c = c + ext[t + i] * cw_ref[i:i + 1, :]
        rows = slice(t * n_seq, (t + 1) * n_seq)
        f_ref[rows, :] = (_gelu_tanh(c) * up[rows, :]).astype(BF16)


def _ffn_up_sample(h2, w_gate, w_up, conv_w, conv_b, hist_tm, n_steps, n_seq, tn):
    rows, D = h2.shape
    F = w_gate.shape[1]
    col_spec = lambda r: pl.BlockSpec((r, tn), lambda j: (0, j))
    block_bytes = 2 * rows * D * 2 + 2 * 2 * D * tn * 4 + 2 * D * tn * 2 + 2 * rows * tn * 6 \
        + 2 * hist_tm.shape[0] * tn * 4 + 4 * rows * tn * 4
    return pl.pallas_call(
        functools.partial(_ffn_up_sample_kernel, n_steps=n_steps, n_seq=n_seq),
        grid=(F // tn,),
        in_specs=[pl.BlockSpec((rows, D), lambda j: (0, 0)), col_spec(D), col_spec(D),
                  col_spec(CONV_W), col_spec(1), col_spec(hist_tm.shape[0])],
        out_specs=[col_spec(rows), col_spec(rows)],
        out_shape=[jax.ShapeDtypeStruct((rows, F), BF16), jax.ShapeDtypeStruct((rows, F), F32)],
        compiler_params=_params(("arbitrary",), block_bytes),
        name="ffn_up_sample",
    )(h2, w_gate, w_up, conv_w, conv_b.reshape(1, F), hist_tm)


def _ffn_down_kernel(f_ref, fs_ref, w_hbm, x1_ref, x1s_ref, g_ref, y_ref, ys_ref, wb_ref, stage_ref, sem,
                     *, n_tiles, chunk):
    i = pl.program_id(0)

    @pl.when(i == 0)
    def _():
        _round_weight_once(w_hbm, wb_ref, stage_ref, sem, chunk)

    @pl.when(i < n_tiles)
    def _():
        y_ref[...] = x1_ref[...] + _rms(_dot(f_ref[...], wb_ref[...]), g_ref[...])

    @pl.when(i == n_tiles)
    def _():
        ys_ref[...] = x1s_ref[...] + _rms(_dot(fs_ref[...], wb_ref[...]), g_ref[...])


def _ffn_down(f, f_s, w_down, x1, x1_s, g, tm):
    rows, F = f.shape
    rows_s = f_s.shape[0]
    D = w_down.shape[1]
    n_tiles = rows // tm
    chunk = _pick_tile(F, 512)
    while (F // chunk) % 2 == 0 and chunk * D * 4 > 4 * 1024 * 1024:
        chunk //= 2
    assert chunk % 16 == 0
    tile = lambda r: pl.BlockSpec((tm, r), lambda i: (jnp.minimum(i, n_tiles - 1), 0))
    whole = lambda r, c: pl.BlockSpec((r, c), lambda i: (0, 0))
    block_bytes = (2 * tm * F * 2 + 2 * rows_s * F * 2 + F * D * 2 + 2 * chunk * D * 4 + 4 * tm * D * 4
                   + 4 * rows_s * D * 4 + 2 * tm * D * 4)
    return pl.pallas_call(
        functools.partial(_ffn_down_kernel, n_tiles=n_tiles, chunk=chunk),
        grid=(n_tiles + 1,),
        in_specs=[tile(F), whole(rows_s, F), pl.BlockSpec(memory_space=pl.ANY), tile(D), whole(rows_s, D),
                  whole(1, D)],
        out_specs=[tile(D), whole(rows_s, D)],
        out_shape=[jax.ShapeDtypeStruct((rows, D), F32), jax.ShapeDtypeStruct((rows_s, D), F32)],
        scratch_shapes=[pltpu.VMEM((F, D), BF16), pltpu.VMEM((2, chunk, D), F32), pltpu.SemaphoreType.DMA((2,))],
        compiler_params=_resident_params(("arbitrary",), block_bytes),
        name="ffn_down",
    )(f, f_s, w_down, x1, x1_s, g.reshape(1, D))


def _pick_tile(n, target):
    t = min(n, target)
    while n % t:
        t //= 2
    return t


def _mixer_inputs(x, x_tm, pool_hist, page_table, lp, n_seq, n_steps):
    B, S, D = x.shape
    tm = _pick_tile(S, 512)
    past_len = page_table.shape[1] * PAGE_SIZE
    pos_s = past_len + jnp.repeat(jnp.arange(n_steps), n_seq)
    u, q, k, v, u_s, q_s, k_s, v_s = _inproj(x, x_tm[None], lp["g_mix_pre"], lp["w_in"],
                                             jnp.arange(S), pos_s, tm)
    pooled = _pool_prompt(u, lp["w_pool"], lp["pool_scale"], tm)
    attn = _moba_prompt(q, k, v)
    x1, h2 = _outproj(pooled.reshape(B * S, -1), attn.reshape(B * S, -1), x.reshape(B * S, D),
                      lp["w_out"], lp["g_mix_post"], lp["g_ffn_pre"], tm)

    def to_seq_major(t):
        H = t.shape[0]
        return t.reshape(H, n_steps, n_seq, HEAD_DIM).transpose(2, 0, 1, 3)

    pooled_s = _pool_sample(u_s[0], _seq_to_rows(pool_hist), lp["w_pool"], lp["pool_scale"],
                            n_steps, n_seq, past_len)
    sample = (to_seq_major(q_s[0]), to_seq_major(k_s[0]), to_seq_major(v_s[0]), pooled_s, u_s[0])
    return (x1, h2, k, v, u), sample


def _rows_to_seq(t, n_steps, n_seq):
    return t.reshape(n_steps, n_seq, t.shape[-1]).transpose(1, 0, 2)


def _seq_to_rows(t):
    return t.transpose(1, 0, 2).reshape(-1, t.shape[-1])


def _layer(xp, xs_tm, pool_hist, conv_hist, cache_k, cache_v, page_table, lp, n_seq, n_steps):
    B, S, D = xp.shape
    F = lp["w_gate"].shape[1]
    rows_s = xs_tm.shape[0]
    (x1p, h2p, kp, vp, up), (qs, ks, vs, pooled_s, us) = _mixer_inputs(xp, xs_tm, pool_hist, page_table, lp,
                                                                       n_seq, n_steps)
    tm = _pick_tile(S, 512)
    tn = _pick_tile(F, 512)
    fp, tail, attn_s = _ffn_up(h2p.reshape(B, S, D), lp["w_gate"], lp["w_up"], lp["conv_w"], lp["conv_b"],
                               qs, ks, vs, cache_k, cache_v, page_table, tm, tn, _pick_tile(tm, 256))
    attn_tm = attn_s.transpose(2, 0, 1, 3).reshape(rows_s, -1).astype(BF16)
    x1s, h2s = _outproj(pooled_s, attn_tm, xs_tm, lp["w_out"], lp["g_mix_post"], lp["g_ffn_pre"], rows_s)
    fs, gts = _ffn_up_sample(h2s, lp["w_gate"], lp["w_up"], lp["conv_w"], lp["conv_b"],
                             _seq_to_rows(conv_hist), n_steps, n_seq, tn)
    yp, ys = _ffn_down(fp.reshape(B * S, F), fs, lp["w_down"], x1p, x1s, lp["g_ffn_post"],
                       _pick_tile(B * S, 256))
    pool_p = up[:, S - POOL_HIST:, :]
    conv_p = tail[:, V7X_SUBLANES - (CONV_W - 1):, :]
    pool_s = jnp.concatenate([pool_hist, _rows_to_seq(us, n_steps, n_seq)], axis=1)[:, n_steps:]
    conv_s = jnp.concatenate([conv_hist, _rows_to_seq(gts, n_steps, n_seq)], axis=1)[:, n_steps:]
    return yp.reshape(B, S, D), ys, (kp, vp, ks, vs, pool_p, pool_s, conv_p, conv_s)


def kernel(x_prompt, x_sample, cache_k, cache_v, state_pool, state_conv, page_table,
           w_in, w_pool, pool_scale, w_out, g_mix_pre, g_mix_post,
           w_gate, w_up, conv_w, conv_b, w_down, g_ffn_pre, g_ffn_post):
    depth = w_in.shape[0]
    n_seq, n_steps, D = x_sample.shape
    assert w_in.shape[2] == 4 * pool_scale.shape[1], "pooling and attention widths must match"
    yp = x_prompt
    ys = x_sample.transpose(1, 0, 2).reshape(n_steps * n_seq, D)
    outs = [[] for _ in range(8)]
    for l in range(depth):
        lp = {"w_in": w_in[l], "w_pool": w_pool[l].astype(BF16), "pool_scale": pool_scale[l],
              "w_out": w_out[l].astype(BF16), "g_mix_pre": g_mix_pre[l], "g_mix_post": g_mix_post[l],
              "w_gate": w_gate[l], "w_up": w_up[l], "conv_w": conv_w[l],
              "conv_b": conv_b[l], "w_down": w_down[l], "g_ffn_pre": g_ffn_pre[l],
              "g_ffn_post": g_ffn_post[l]}
        yp, ys, states = _layer(yp, ys, state_pool[l], state_conv[l], cache_k[l], cache_v[l],
                                page_table, lp, n_seq, n_steps)
        for lst, val in zip(outs, states):
            lst.append(val)
    y_sample = ys.reshape(n_steps, n_seq, D).transpose(1, 0, 2)
    return (yp, y_sample) + tuple(jnp.stack(o) for o in outs)
```

```python
import functools

import jax
import jax.numpy as jnp
from jax import lax
from jax.experimental import pallas as pl
from jax.experimental.pallas import tpu as pltpu

F32 = jnp.float32
BF16 = jnp.bfloat16

POOL_WINDOWS = (2, 4, 8, 16)
POOL_HIST = max(POOL_WINDOWS) - 1
HEAD_DIM = 128
ROT_DIM = HEAD_DIM // 4
ROPE_THETA = 500000.0
MOBA_BLOCK = 256
MOBA_TOPK = 3
QUERY_BLOCKS_PER_DOT = 2
PAGE_SIZE = 128
CONV_W = 3
EPS = 1e-6
NEG_INF = float("-inf")
LOG2E = 1.4426950408889634

V7X_SUBLANES = 8
V7X_VMEM_BYTES = 64 * 1024 * 1024
VMEM_RESERVE_BYTES = 4 * 1024 * 1024


def _vmem_limit(block_bytes):
    return int(min(2 * block_bytes, V7X_VMEM_BYTES - VMEM_RESERVE_BYTES))


def _params(semantics, block_bytes):
    return pltpu.CompilerParams(dimension_semantics=semantics,
                                vmem_limit_bytes=_vmem_limit(block_bytes))


def _resident_params(semantics, block_bytes):
    limit = int(min(block_bytes + VMEM_RESERVE_BYTES, V7X_VMEM_BYTES - VMEM_RESERVE_BYTES))
    return pltpu.CompilerParams(dimension_semantics=semantics, vmem_limit_bytes=limit)


def _rms(x, g):
    return x * lax.rsqrt(jnp.mean(x * x, axis=-1, keepdims=True) + EPS) * g


def _dot(a, b):
    return jnp.dot(a, b, preferred_element_type=F32)


def _dot_nt(a, b, precision=None):
    return lax.dot_general(a, b, (((1,), (1,)), ((), ())), precision=precision,
                           preferred_element_type=F32)


def _rope_tables(pos):
    half = ROT_DIM // 2
    inv = ROPE_THETA ** (-jnp.arange(half, dtype=F32) * (2.0 / ROT_DIM))
    ang = pos.astype(F32)[:, None] * inv[None, :]
    cos, sin = jnp.cos(ang), jnp.sin(ang)
    n = pos.shape[0]
    c = jnp.concatenate([cos, cos, jnp.ones((n, HEAD_DIM - ROT_DIM), F32)], axis=1)
    s_lo = jnp.concatenate([-sin, jnp.zeros((n, HEAD_DIM - half), F32)], axis=1)
    s_hi = jnp.concatenate([jnp.zeros((n, half), F32), sin,
                            jnp.zeros((n, HEAD_DIM - ROT_DIM), F32)], axis=1)
    return c, s_lo, s_hi


def _round_weight_once(w_hbm, wb_ref, stage_ref, sem, chunk):
    n_chunks = wb_ref.shape[0] // chunk

    def w_copy(c):
        return pltpu.make_async_copy(w_hbm.at[pl.ds(c * chunk, chunk), :], stage_ref.at[c % 2], sem.at[c % 2])

    w_copy(0).start()
    for c in range(n_chunks):
        if c + 1 < n_chunks:
            w_copy(c + 1).start()
        w_copy(c).wait()
        wb_ref[c * chunk:(c + 1) * chunk, :] = stage_ref[c % 2].astype(BF16)


def _inproj_kernel(x_ref, xs_ref, g_ref, w_hbm, c_ref, slo_ref, shi_ref, cs_ref, slos_ref, shis_ref,
                   u_ref, q_ref, k_ref, v_ref, us_ref, qs_ref, ks_ref, vs_ref,
                   wb_ref, hn_ref, stage_ref, sem, *, n_heads, width, n_tiles, chunk):
    i = pl.program_id(0)
    pair = 2 * HEAD_DIM

    @pl.when(i == 0)
    def _():
        _round_weight_once(w_hbm, wb_ref, stage_ref, sem, chunk)

    def project(x, tables, outs):
        c, s_lo, s_hi = (t[...] for t in tables)
        u_out, q_out, k_out, v_out = outs
        rows = x.shape[0]
        hn_ref[0:rows, :] = _rms(x, g_ref[...]).astype(BF16)

        def rope(h):
            return (h * c + pltpu.roll(h, HEAD_DIM - ROT_DIM // 2, 1) * s_lo
                    + pltpu.roll(h, ROT_DIM // 2, 1) * s_hi)

        for p in range(width // pair):
            u_out[0, :, p * pair:(p + 1) * pair] = _dot(hn_ref[0:rows, :], wb_ref[:, p * pair:(p + 1) * pair])
        for out_ref, part, rotary in ((q_out, 1, True), (k_out, 2, True), (v_out, 3, False)):
            for p in range(n_heads // 2):
                col = part * width + p * pair
                res = _dot(hn_ref[0:rows, :], wb_ref[:, col:col + pair])
                for half in range(2):
                    h = res[:, half * HEAD_DIM:(half + 1) * HEAD_DIM]
                    out_ref[0, 2 * p + half] = rope(h) if rotary else h

    @pl.when(i < n_tiles)
    def _():
        project(x_ref[0], (c_ref, slo_ref, shi_ref), (u_ref, q_ref, k_ref, v_ref))

    @pl.when(i == n_tiles)
    def _():
        project(xs_ref[0], (cs_ref, slos_ref, shis_ref), (us_ref, qs_ref, ks_ref, vs_ref))


def _inproj(x, x_s, g, w_in, pos, pos_s, tm):
    B, S, D = x.shape
    rows_s = x_s.shape[1]
    width = w_in.shape[1] // 4
    n_heads = width // HEAD_DIM
    assert n_heads % 2 == 0 and rows_s <= tm
    n_s = S // tm
    n_tiles = B * n_s
    chunk = _pick_tile(D, 256)

    def tile(i):
        ii = jnp.minimum(i, n_tiles - 1)
        return ii // n_s, lax.rem(ii, n_s)

    tab_spec = pl.BlockSpec((tm, HEAD_DIM), lambda i: (tile(i)[1], 0))
    hm_spec = pl.BlockSpec((1, n_heads, tm, HEAD_DIM), lambda i: (tile(i)[0], 0, tile(i)[1], 0))
    hm_shape = jax.ShapeDtypeStruct((B, n_heads, S, HEAD_DIM), F32)
    whole = lambda shape: pl.BlockSpec(shape, lambda i: (0,) * len(shape))
    stab_spec = whole((rows_s, HEAD_DIM))
    shm_spec = whole((1, n_heads, rows_s, HEAD_DIM))
    shm_shape = jax.ShapeDtypeStruct((1, n_heads, rows_s, HEAD_DIM), F32)
    block_bytes = (2 * tm * D * 4 + tm * D * 2 + D * 4 * width * 2 + 2 * chunk * 4 * width * 4
                   + 2 * 4 * tm * width * 4 + 2 * rows_s * (D + 4 * width) * 4)
    return pl.pallas_call(
        functools.partial(_inproj_kernel, n_heads=n_heads, width=width, n_tiles=n_tiles, chunk=chunk),
        grid=(n_tiles + 1,),
        in_specs=[
            pl.BlockSpec((1, tm, D), lambda i: (tile(i)[0], tile(i)[1], 0)),
            whole((1, rows_s, D)),
            whole((1, D)),
            pl.BlockSpec(memory_space=pl.ANY),
            tab_spec, tab_spec, tab_spec, stab_spec, stab_spec, stab_spec,
        ],
        out_specs=[pl.BlockSpec((1, tm, width), lambda i: (tile(i)[0], tile(i)[1], 0)), hm_spec, hm_spec, hm_spec,
                   whole((1, rows_s, width)), shm_spec, shm_spec, shm_spec],
        out_shape=[jax.ShapeDtypeStruct((B, S, width), F32), hm_shape, hm_shape, hm_shape,
                   jax.ShapeDtypeStruct((1, rows_s, width), F32), shm_shape, shm_shape, shm_shape],
        scratch_shapes=[pltpu.VMEM((D, 4 * width), BF16), pltpu.VMEM((tm, D), BF16),
                        pltpu.VMEM((2, chunk, 4 * width), F32), pltpu.SemaphoreType.DMA((2,))],
        compiler_params=_resident_params(("arbitrary",), block_bytes),
        name="inproj",
    )(x, x_s, g.reshape(1, D), w_in, *_rope_tables(pos), *_rope_tables(pos_s))


def _shift_rows(x, prev8, k):
    pad = V7X_SUBLANES
    rolled = pltpu.roll(x, k, 0)
    row8 = lax.broadcasted_iota(jnp.int32, prev8.shape, 0)
    head = jnp.where(row8 < k, pltpu.roll(prev8, k % pad, 0) if k % pad else prev8, rolled[0:pad])
    return jnp.concatenate([head, rolled[pad:]], axis=0)


def _pool_kernel(u_ref, w_ref, scale_ref, o_ref, halo_ref, *, ts, group):
    s = pl.program_id(1)
    pad = V7X_SUBLANES

    @pl.when(s == 0)
    def _():
        halo_ref[...] = jnp.zeros(halo_ref.shape, F32)

    pos = s * ts + lax.broadcasted_iota(jnp.int32, (ts, 1), 0)
    slot = 0
    for gi, w in enumerate(POOL_WINDOWS):
        cols = slice(gi * group, (gi + 1) * group)
        x = u_ref[0, :, cols]
        acc = x
        span = 1
        while span < w:
            prev8 = halo_ref[slot]
            halo_ref[slot] = acc[ts - pad:ts, :]
            acc = acc + _shift_rows(acc, prev8, span)
            slot += 1
            span *= 2
        cnt = jnp.minimum(pos + 1, w).astype(F32)
        d = (acc / cnt - x).astype(BF16)
        y = _dot(d, w_ref[gi]) * scale_ref[:, cols]
        o_ref[0, :, cols] = y.astype(BF16)


def _pool_prompt(u, w_pool_bf, scale, ts):
    B, S, C = u.shape
    n_groups, group, _ = w_pool_bf.shape
    assert all(w & (w - 1) == 0 and w <= 2 * V7X_SUBLANES for w in POOL_WINDOWS)
    n_levels = sum(w.bit_length() - 1 for w in POOL_WINDOWS)
    block_bytes = 2 * ts * C * 4 + 2 * n_groups * group * group * 2 + 2 * ts * C * 2 + 6 * ts * group * 4
    return pl.pallas_call(
        functools.partial(_pool_kernel, ts=ts, group=group),
        grid=(B, S // ts),
        in_specs=[
            pl.BlockSpec((1, ts, C), lambda b, s: (b, s, 0)),
            pl.BlockSpec((n_groups, group, group), lambda b, s: (0, 0, 0)),
            pl.BlockSpec((1, C), lambda b, s: (0, 0)),
        ],
        out_specs=pl.BlockSpec((1, ts, C), lambda b, s: (b, s, 0)),
        out_shape=jax.ShapeDtypeStruct((B, S, C), BF16),
        scratch_shapes=[pltpu.VMEM((n_levels, V7X_SUBLANES, group), F32)],
        compiler_params=_params(("arbitrary", "arbitrary"), block_bytes),
        name="pool_prompt",
    )(u, w_pool_bf, scale.reshape(1, C))


def _pool_sample_kernel(u_ref, hist_ref, w_ref, scale_ref, o_ref, *, n_steps, n_seq, group, past_len):
    def ext(i):
        if i < POOL_HIST:
            return hist_ref[i * n_seq:(i + 1) * n_seq, :]
        return u_ref[(i - POOL_HIST) * n_seq:(i - POOL_HIST + 1) * n_seq, :]

    for gi, w in enumerate(POOL_WINDOWS):
        cols = slice(gi * group, (gi + 1) * group)
        ds = []
        for t in range(n_steps):
            cur = ext(POOL_HIST + t)[:, cols]
            acc = cur
            for back in range(1, w):
                acc = acc + ext(POOL_HIST + t - back)[:, cols]
            cnt = float(min(past_len + t + 1, w))
            ds.append((acc / cnt - cur).astype(BF16))
        d = jnp.concatenate(ds, axis=0)
        y = _dot(d, w_ref[gi]) * scale_ref[:, cols]
        o_ref[:, cols] = y.astype(BF16)


def _pool_sample(u_tm, hist_tm, w_pool_bf, scale, n_steps, n_seq, past_len):
    rows, C = u_tm.shape
    n_groups, group, _ = w_pool_bf.shape
    block_bytes = 2 * (rows + hist_tm.shape[0]) * C * 4 + 2 * n_groups * group * group * 2 + 2 * rows * C * 2
    return pl.pallas_call(
        functools.partial(_pool_sample_kernel, n_steps=n_steps, n_seq=n_seq, group=group,
                          past_len=past_len),
        out_shape=jax.ShapeDtypeStruct((rows, C), BF16),
        compiler_params=pltpu.CompilerParams(vmem_limit_bytes=_vmem_limit(block_bytes)),
        name="pool_sample",
    )(u_tm, hist_tm, w_pool_bf, scale.reshape(1, C))


class _PerHead:
    def __init__(self, refs):
        self.refs = refs

    def __getitem__(self, idx):
        return self.refs[idx[0]][idx[1:]] if isinstance(idx, tuple) else self.refs[idx][...]

    def __setitem__(self, idx, value):
        self.refs[idx[0]][idx[1:]] = value


def _moba_prompt_kernel(q_ref, k_ref, v_ref, o_ref, *scratch, seq, hp):
    blk = MOBA_BLOCK
    nb = seq // blk
    n_sel = min(MOBA_TOPK, nb - 1)
    scale = HEAD_DIM ** -0.5
    kb_ref, vt_ref, km_ref, s_ref, p_ref = (_PerHead(scratch[kind * hp:(kind + 1) * hp]) for kind in range(5))

    for hh in range(hp):
        for n in range(nb):
            rows = slice(n * blk, (n + 1) * blk)
            kn = k_ref[0, hh, rows, :]
            kb_ref[hh, rows, :] = kn.astype(BF16)
            km_ref[hh, n:n + 1, :] = jnp.sum(kn, axis=0, keepdims=True) * (1.0 / blk)
            vt_ref[hh, :, rows] = v_ref[0, hh, rows, :].T.astype(BF16)

    blk_id = lax.broadcasted_iota(jnp.int32, (nb, blk), 0)
    key_i = lax.broadcasted_iota(jnp.int32, (blk, blk), 0)
    qry_i = lax.broadcasted_iota(jnp.int32, (blk, blk), 1)
    causal_bias = jnp.where(key_i <= qry_i, 0.0, NEG_INF).astype(F32)

    def scores(hh, qblocks):
        first, last = qblocks[0], qblocks[-1]
        nk = (last + 1) * blk
        width = len(qblocks) * blk
        qg = q_ref[0, hh, first * blk:nk, :]
        s_ref[hh, 0:nk, 0:width] = _dot_nt(kb_ref[hh, 0:nk, :], (qg * (scale * LOG2E)).astype(BF16))
        return _dot_nt(km_ref[hh], qg, precision=lax.Precision.HIGHEST) if last > 0 else None

    def softmax(hh, qblocks, gate):
        last = qblocks[-1]
        sums = []
        for a, i in enumerate(qblocks):
            cols = slice(a * blk, (a + 1) * blk)
            own = slice(i * blk, (i + 1) * blk)
            biases = []
            if i > 0:
                gate_a = gate[:, cols]
                valid = jnp.where(blk_id < i, 1.0, 0.0)
                for n in range(i):
                    gn = gate_a[n:n + 1, :]
                    beats = jnp.where(blk_id < n, jnp.where(gate_a >= gn, 1.0, 0.0),
                                      jnp.where(gate_a > gn, 1.0, 0.0))
                    rank = jnp.sum(beats * valid, axis=0, keepdims=True)
                    biases.append(jnp.where(rank < n_sel, 0.0, NEG_INF).astype(F32))

            s_own = s_ref[hh, own, cols] + causal_bias
            m = jnp.max(s_own, axis=0, keepdims=True)
            for n in range(i):
                m = jnp.maximum(m, jnp.max(s_ref[hh, n * blk:(n + 1) * blk, cols], axis=0, keepdims=True)
                                + biases[n])

            p = jnp.exp2(s_own - m)
            l = jnp.sum(p, axis=0, keepdims=True)
            p_ref[hh, own, cols] = p.astype(BF16)
            for n in range(i):
                rows = slice(n * blk, (n + 1) * blk)
                p = jnp.exp2(s_ref[hh, rows, cols] + (biases[n] - m))
                l = l + jnp.sum(p, axis=0, keepdims=True)
                p_ref[hh, rows, cols] = p.astype(BF16)
            for n in range(i + 1, last + 1):
                p_ref[hh, n * blk:(n + 1) * blk, cols] = jnp.zeros((blk, blk), BF16)
            sums.append(l)
        return sums

    def output(hh, qblocks, sums):
        nk = (qblocks[-1] + 1) * blk
        width = len(qblocks) * blk
        o_t = _dot(vt_ref[hh, :, 0:nk], p_ref[hh, 0:nk, 0:width])
        for a, i in enumerate(qblocks):
            o_a = o_t[:, a * blk:(a + 1) * blk] / sums[a]
            o_ref[0, i * blk:(i + 1) * blk, hh * HEAD_DIM:(hh + 1) * HEAD_DIM] = o_a.T.astype(BF16)

    items = [(hh, list(range(first, min(first + QUERY_BLOCKS_PER_DOT, nb))))
             for first in range(0, nb, QUERY_BLOCKS_PER_DOT) for hh in range(hp)]
    gate = scores(*items[0])
    for cur, nxt in zip(items, items[1:] + [None]):
        next_gate = scores(*nxt) if nxt is not None else None
        output(*cur, softmax(*cur, gate))
        gate = next_gate


def _moba_prompt(q, k, v):
    B, H, S, Dh = q.shape
    nb = S // MOBA_BLOCK
    hp = 2 if H % 2 == 0 else 1
    width = QUERY_BLOCKS_PER_DOT * MOBA_BLOCK
    in_spec = pl.BlockSpec((1, hp, S, Dh), lambda b, g: (b, g, 0, 0))
    block_bytes = hp * (2 * 3 * S * Dh * 4 + 2 * S * Dh * 2 + 2 * S * Dh * 2 + S * width * 6)
    return pl.pallas_call(
        functools.partial(_moba_prompt_kernel, seq=S, hp=hp),
        grid=(B, H // hp),
        in_specs=[in_spec, in_spec, in_spec],
        out_specs=pl.BlockSpec((1, S, hp * Dh), lambda b, g: (b, 0, g)),
        out_shape=jax.ShapeDtypeStruct((B, S, H * Dh), BF16),
        scratch_shapes=(
            [pltpu.VMEM((S, Dh), BF16)] * hp
            + [pltpu.VMEM((Dh, S), BF16)] * hp
            + [pltpu.VMEM((nb, Dh), F32)] * hp
            + [pltpu.VMEM((S, width), F32)] * hp
            + [pltpu.VMEM((S, width), BF16)] * hp
        ),
        compiler_params=_params(("arbitrary", "arbitrary"), block_bytes),
        name="moba_prompt",
    )(q, k, v)


class _PagedAttention:
    def __init__(self, pt_ref, qc_ref, qp_ref, kn_ref, vn_ref, ck_ref, cv_ref, o_ref,
                 kbuf, ksel, vbuf, km_ref, q8_ref, sel_ref, ksem, vsem, *, n_pages, n_groups, hp, n_new):
        self.__dict__.update(locals())
        self.ppb = MOBA_BLOCK // PAGE_SIZE
        self.nblk = n_pages // self.ppb
        self.n_sel = min(MOBA_TOPK, self.nblk)

    def unit_bh(self, u):
        return u // self.n_groups, lax.rem(u, self.n_groups) * self.hp

    def k_copy(self, u, page):
        bb, h0 = self.unit_bh(u)
        sl = lax.rem(u, 2)
        return pltpu.make_async_copy(self.ck_ref.at[self.pt_ref[bb, page], pl.ds(h0, self.hp)],
                                     self.kbuf.at[sl, :, pl.ds(page * PAGE_SIZE, PAGE_SIZE), :],
                                     self.ksem.at[sl])

    def start_k(self, u):
        def body(page, carry):
            self.k_copy(u, page).start()
            return carry
        lax.fori_loop(0, self.n_pages, body, 0, unroll=8)

    def wait_k(self, u):
        for page in range(self.n_pages):
            self.k_copy(u, page).wait()

    def sel_index(self, hh, j, c):
        return (hh * self.n_new + j) * self.n_sel + c

    def v_copies(self, u, heads):
        bb, h0 = self.unit_bh(u)
        sl = lax.rem(u, 2)
        for hh in heads:
            for j in range(self.n_new):
                for c in range(self.n_sel):
                    blk = self.sel_ref[sl, self.sel_index(hh, j, c)]
                    for pg in range(self.ppb):
                        page = self.pt_ref[bb, blk * self.ppb + pg]
                        dst = ((j * self.n_sel + c) * self.ppb + pg) * PAGE_SIZE
                        yield pltpu.make_async_copy(self.cv_ref.at[page, h0 + hh],
                                                    self.vbuf.at[sl, hh, pl.ds(dst, PAGE_SIZE), :],
                                                    self.vsem.at[sl])

    def start_v(self, u, heads):
        for cp in self.v_copies(u, heads):
            cp.start()

    def wait_v(self, u):
        for cp in self.v_copies(u, range(self.hp)):
            cp.wait()

    def block_means(self, u, heads):
        sl = lax.rem(u, 2)
        for hh in heads:
            for n in range(self.nblk):
                kn_blk = self.kbuf[sl, hh, n * MOBA_BLOCK:(n + 1) * MOBA_BLOCK, :]
                self.km_ref[hh, n:n + 1, :] = jnp.sum(kn_blk, axis=0, keepdims=True) * (1.0 / MOBA_BLOCK)

    def select(self, u, heads):
        sl = lax.rem(u, 2)
        for hh in heads:
            self.q8_ref[hh] = jnp.zeros(self.q8_ref.shape[1:], F32)
            self.q8_ref[hh, 0:self.n_new, :] = self.qc_ref[0, hh]
            gate = _dot_nt(self.q8_ref[hh], self.km_ref[hh], precision=lax.Precision.HIGHEST)
            lane = lax.broadcasted_iota(jnp.int32, gate.shape, 1)
            picks = [jnp.zeros((gate.shape[0], 1), F32) for _ in range(self.n_sel)]
            for n in range(self.nblk):
                gn = gate[:, n:n + 1]
                beats = jnp.where(lane < n, jnp.where(gate >= gn, 1.0, 0.0), jnp.where(gate > gn, 1.0, 0.0))
                rank = jnp.sum(beats, axis=1, keepdims=True)
                for c in range(self.n_sel):
                    picks[c] = jnp.where(rank == float(c), float(n), picks[c])
            for j in range(self.n_new):
                for c in range(self.n_sel):
                    blk = picks[c][j, 0].astype(jnp.int32)
                    self.sel_ref[sl, self.sel_index(hh, j, c)] = blk
                    row0 = (j * self.n_sel + c) * MOBA_BLOCK
                    start = pl.multiple_of(blk * MOBA_BLOCK, MOBA_BLOCK)
                    self.ksel[sl, hh, row0:row0 + MOBA_BLOCK, :] = self.kbuf[sl, hh, pl.ds(start, MOBA_BLOCK), :]

    def attend(self, u, heads):
        sl = lax.rem(u, 2)
        scale = HEAD_DIM ** -0.5
        new_row = lax.broadcasted_iota(jnp.int32, (self.n_new, 1), 0)
        for hh in heads:
            q = self.qp_ref[0, hh]
            kn = self.kn_ref[0, hh]
            vn = self.vn_ref[0, hh]
            for j in range(self.n_new):
                qj = q[j:j + 1, :] * (scale * LOG2E)
                rows = [slice((j * self.n_sel + c) * MOBA_BLOCK, (j * self.n_sel + c + 1) * MOBA_BLOCK)
                        for c in range(self.n_sel)]
                s_sel = [jnp.sum(self.ksel[sl, hh, r, :] * qj, axis=1, keepdims=True) for r in rows]
                s_new = jnp.sum(kn * qj, axis=1, keepdims=True)
                s_new = jnp.where(new_row <= j, s_new, NEG_INF)
                m = jnp.max(s_new, axis=0, keepdims=True)
                for s in s_sel:
                    m = jnp.maximum(m, jnp.max(s, axis=0, keepdims=True))
                p_new = jnp.exp2(s_new - m)
                l = jnp.sum(p_new, axis=0, keepdims=True)
                acc = jnp.sum(p_new * vn, axis=0, keepdims=True)
                for s, r in zip(s_sel, rows):
                    p = jnp.exp2(s - m)
                    l = l + jnp.sum(p, axis=0, keepdims=True)
                    acc = acc + jnp.sum(p * self.vbuf[sl, hh, r, :], axis=0, keepdims=True)
                self.o_ref[0, hh, j:j + 1, :] = acc / l


def _paged_attention_scratch(n_pages, hp, n_new):
    nblk = n_pages // (MOBA_BLOCK // PAGE_SIZE)
    n_sel = min(MOBA_TOPK, nblk)
    picked_rows = n_new * n_sel * MOBA_BLOCK
    shapes = [
        pltpu.VMEM((2, hp, n_pages * PAGE_SIZE, HEAD_DIM), F32),
        pltpu.VMEM((2, hp, picked_rows, HEAD_DIM), F32),
        pltpu.VMEM((2, hp, picked_rows, HEAD_DIM), F32),
        pltpu.VMEM((hp, nblk, HEAD_DIM), F32),
        pltpu.VMEM((hp, V7X_SUBLANES, HEAD_DIM), F32),
        pltpu.SMEM((2, hp * n_new * n_sel), jnp.int32),
        pltpu.SemaphoreType.DMA((2,)),
        pltpu.SemaphoreType.DMA((2,)),
    ]
    n_bytes = 4 * HEAD_DIM * 2 * hp * (n_pages * PAGE_SIZE + 2 * picked_rows)
    return shapes, n_bytes


def _outproj_kernel(pooled_ref, attn_ref, x_ref, w_ref, gpost_ref, gffn_ref, x1_ref, h2_ref, *, split):
    mix = _dot(pooled_ref[...], w_ref[0:split, :]) + _dot(attn_ref[...], w_ref[split:, :])
    x1 = x_ref[...] + _rms(mix, gpost_ref[...])
    x1_ref[...] = x1
    h2_ref[...] = _rms(x1, gffn_ref[...]).astype(BF16)


def _outproj(pooled, attn, x, w_bf, g_post, g_ffn, tm):
    rows, D = x.shape
    split = pooled.shape[1]
    wa = attn.shape[1]
    row_spec = lambda width: pl.BlockSpec((tm, width), lambda i: (i, 0))
    vec_spec = pl.BlockSpec((1, D), lambda i: (0, 0))
    block_bytes = 2 * tm * (split + wa) * 2 + 2 * tm * D * 4 + 2 * (split + wa) * D * 2 \
        + 2 * tm * D * 6 + 2 * tm * D * 4
    return pl.pallas_call(
        functools.partial(_outproj_kernel, split=split),
        grid=(rows // tm,),
        in_specs=[row_spec(split), row_spec(wa), row_spec(D),
                  pl.BlockSpec((split + wa, D), lambda i: (0, 0)), vec_spec, vec_spec],
        out_specs=[row_spec(D), row_spec(D)],
        out_shape=[jax.ShapeDtypeStruct((rows, D), F32), jax.ShapeDtypeStruct((rows, D), BF16)],
        compiler_params=_params(("arbitrary",), block_bytes),
        name="outproj",
    )(pooled, attn, x, w_bf, g_post.reshape(1, D), g_ffn.reshape(1, D))


def _gelu_tanh(c):
    return c * (0.5 * (1.0 + jnp.tanh(0.7978845608028654 * (c + 0.044715 * (c * c * c)))))


def _conv_gelu_gate(gt, prev8, up, cw_ref, cb_ref):
    c = cb_ref[...]
    for i in range(CONV_W):
        back = CONV_W - 1 - i
        c = c + (_shift_rows(gt, prev8, back) if back else gt) * cw_ref[i:i + 1, :]
    return _gelu_tanh(c) * up


def _ffn_up_kernel(pt_ref, h_ref, wg_ref, wu_ref, cw_ref, cb_ref, qc_ref, qp_ref, kn_ref, vn_ref, ck_ref, cv_ref,
                   f_ref, tail_ref, o_ref, wgb_ref, wub_ref, halo_ref, *attn_scratch,
                   tm, sub, n_b, n_s, n_units, attn_params):
    j, b, s = pl.program_id(0), pl.program_id(1), pl.program_id(2)
    t = (j * n_b + b) * n_s + s
    pad = V7X_SUBLANES
    last_sub = tm // sub - 1
    attn = _PagedAttention(pt_ref, qc_ref, qp_ref, kn_ref, vn_ref, ck_ref, cv_ref, o_ref,
                           *attn_scratch, **attn_params)

    @pl.when((b == 0) & (s == 0))
    def _():
        wgb_ref[...] = wg_ref[...].astype(BF16)
        wub_ref[...] = wu_ref[...].astype(BF16)

    @pl.when(s == 0)
    def _():
        halo_ref[...] = jnp.zeros(halo_ref.shape, F32)

    @pl.when(t == 0)
    def _():
        attn.start_k(t)

    @pl.when(t < n_units)
    def _():
        attn.wait_k(t)

    @pl.when(t + 1 < n_units)
    def _():
        attn.start_k(t + 1)

    def gate_up(before=None, middle=None, end=()):
        before, middle = before or {}, middle or {}
        prev8 = halo_ref[...]
        half = sub // 2
        for r in range(tm // sub):
            for stage in before.get(r, ()):
                stage()
            row0 = r * sub
            if r in middle:
                lo = h_ref[0, row0:row0 + half, :]
                gt_lo, up_lo = _dot(lo, wgb_ref[...]), _dot(lo, wub_ref[...])
                for stage in middle[r]:
                    stage()
                hi = h_ref[0, row0 + half:row0 + sub, :]
                gt = jnp.concatenate([gt_lo, _dot(hi, wgb_ref[...])], axis=0)
                up = jnp.concatenate([up_lo, _dot(hi, wub_ref[...])], axis=0)
            else:
                hr = h_ref[0, row0:row0 + sub, :]
                gt, up = _dot(hr, wgb_ref[...]), _dot(hr, wub_ref[...])
            f_ref[0, row0:row0 + sub, :] = _conv_gelu_gate(gt, prev8, up, cw_ref, cb_ref).astype(BF16)
            prev8 = gt[sub - pad:sub, :]
        halo_ref[...] = prev8
        tail_ref[0] = prev8
        for stage in end:
            stage()

    heads = tuple(range(attn.hp))

    def means():
        attn.block_means(t, heads)

    def select():
        attn.select(t, heads)

    def attend_previous():
        attn.wait_v(t - 1)
        attn.attend(t - 1, heads)

    def fetch_values():
        attn.start_v(t, heads)

    @pl.when(t == 0)
    def _():
        gate_up(before={0: [means]}, middle={0: [select]}, end=[fetch_values])

    @pl.when((t >= 1) & (t < n_units))
    def _():
        before = {0: [means]}
        before.setdefault(last_sub, []).append(attend_previous)
        gate_up(before=before, middle={0: [select]}, end=[fetch_values])

    @pl.when(t == n_units)
    def _():
        gate_up(before={last_sub: [attend_previous]})

    @pl.when(t > n_units)
    def _():
        gate_up()


def _ffn_up(h2, w_gate, w_up, conv_w, conv_b, q, k_new, v_new, cache_k, cache_v, page_table, tm, tn, sub):
    B, S, D = h2.shape
    F = w_gate.shape[1]
    Bd, H, L, Dh = q.shape
    n_pages = page_table.shape[1]
    ppb = MOBA_BLOCK // PAGE_SIZE
    assert n_pages % ppb == 0, "past length must be a whole number of MoBA blocks"
    assert n_pages >= ppb and L <= V7X_SUBLANES
    hp = 2 if H % 2 == 0 else 1
    n_groups = H // hp
    n_units = Bd * n_groups
    n_b, n_s = B, S // tm
    assert (F // tn) * n_b * n_s > n_units >= 2, "not enough grid steps to host the sample attention"
    pad = V7X_SUBLANES

    def unit_spec(lag):
        def index_map(j, b, s, pt):
            u = jnp.clip((j * n_b + b) * n_s + s - lag, 0, n_units - 1)
            return (u // n_groups, lax.rem(u, n_groups), 0, 0)
        return pl.BlockSpec((1, hp, L, Dh), index_map)

    w_spec = pl.BlockSpec((D, tn), lambda j, b, s, pt: (0, j))
    any_spec = pl.BlockSpec(memory_space=pl.ANY)
    attn_scratch, attn_bytes = _paged_attention_scratch(n_pages, hp, L)
    block_bytes = (2 * tm * D * 2 + 2 * 2 * D * tn * 4 + 2 * D * tn * 2 + 2 * tm * tn * 2 + 8 * sub * tn * 4
                   + attn_bytes)
    grid_spec = pltpu.PrefetchScalarGridSpec(
        num_scalar_prefetch=1,
        grid=(F // tn, n_b, n_s),
        in_specs=[
            pl.BlockSpec((1, tm, D), lambda j, b, s, pt: (b, s, 0)),
            w_spec, w_spec,
            pl.BlockSpec((CONV_W, tn), lambda j, b, s, pt: (0, j)),
            pl.BlockSpec((1, tn), lambda j, b, s, pt: (0, j)),
            unit_spec(0), unit_spec(1), unit_spec(1), unit_spec(1), any_spec, any_spec,
        ],
        out_specs=[pl.BlockSpec((1, tm, tn), lambda j, b, s, pt: (b, s, j)),
                   pl.BlockSpec((1, pad, tn), lambda j, b, s, pt: (b, 0, j)),
                   unit_spec(1)],
        scratch_shapes=[pltpu.VMEM((D, tn), BF16), pltpu.VMEM((D, tn), BF16), pltpu.VMEM((pad, tn), F32)]
        + attn_scratch,
    )
    return pl.pallas_call(
        functools.partial(_ffn_up_kernel, tm=tm, sub=sub, n_b=n_b, n_s=n_s, n_units=n_units,
                          attn_params=dict(n_pages=n_pages, n_groups=n_groups, hp=hp, n_new=L)),
        grid_spec=grid_spec,
        out_shape=[jax.ShapeDtypeStruct((B, S, F), BF16), jax.ShapeDtypeStruct((B, pad, F), F32),
                   jax.ShapeDtypeStruct((Bd, H, L, Dh), F32)],
        compiler_params=_resident_params(("arbitrary", "arbitrary", "arbitrary"), block_bytes),
        name="ffn_up",
    )(page_table, h2, w_gate, w_up, conv_w, conv_b.reshape(1, F), q, q, k_new, v_new, cache_k, cache_v)


def _ffn_up_sample_kernel(h_ref, wg_ref, wu_ref, cw_ref, cb_ref, hist_ref, f_ref, gt_ref, *, n_steps, n_seq):
    h = h_ref[...]
    gt = _dot(h, wg_ref[...].astype(BF16))
    up = _dot(h, wu_ref[...].astype(BF16))
    gt_ref[...] = gt
    ext = [hist_ref[i * n_seq:(i + 1) * n_seq, :] for i in range(CONV_W - 1)]
    ext += [gt[t * n_seq:(t + 1) * n_seq, :] for t in range(n_steps)]
    for t in range(n_steps):
        c = cb_ref[...]
        for i in range(CONV_W):
            c = c + ext[t + i] * cw_ref[i:i + 1, :]
        rows = slice(t * n_seq, (t + 1) * n_seq)
        f_ref[rows, :] = (_gelu_tanh(c) * up[rows, :]).astype(BF16)


def _ffn_up_sample(h2, w_gate, w_up, conv_w, conv_b, hist_tm, n_steps, n_seq, tn):
    rows, D = h2.shape
    F = w_gate.shape[1]
    col_spec = lambda r: pl.BlockSpec((r, tn), lambda j: (0, j))
    block_bytes = 2 * rows * D * 2 + 2 * 2 * D * tn * 4 + 2 * D * tn * 2 + 2 * rows * tn * 6 \
        + 2 * hist_tm.shape[0] * tn * 4 + 4 * rows * tn * 4
    return pl.pallas_call(
        functools.partial(_ffn_up_sample_kernel, n_steps=n_steps, n_seq=n_seq),
        grid=(F // tn,),
        in_specs=[pl.BlockSpec((rows, D), lambda j: (0, 0)), col_spec(D), col_spec(D),
                  col_spec(CONV_W), col_spec(1), col_spec(hist_tm.shape[0])],
        out_specs=[col_spec(rows), col_spec(rows)],
        out_shape=[jax.ShapeDtypeStruct((rows, F), BF16), jax.ShapeDtypeStruct((rows, F), F32)],
        compiler_params=_params(("arbitrary",), block_bytes),
        name="ffn_up_sample",
    )(h2, w_gate, w_up, conv_w, conv_b.reshape(1, F), hist_tm)


def _ffn_down_kernel(f_ref, fs_ref, w_hbm, x1_ref, x1s_ref, g_ref, y_ref, ys_ref, wb_ref, stage_ref, sem,
                     *, n_tiles, chunk):
    i = pl.program_id(0)

    @pl.when(i == 0)
    def _():
        _round_weight_once(w_hbm, wb_ref, stage_ref, sem, chunk)

    @pl.when(i < n_tiles)
    def _():
        y_ref[...] = x1_ref[...] + _rms(_dot(f_ref[...], wb_ref[...]), g_ref[...])

    @pl.when(i == n_tiles)
    def _():
        ys_ref[...] = x1s_ref[...] + _rms(_dot(fs_ref[...], wb_ref[...]), g_ref[...])


def _ffn_down(f, f_s, w_down, x1, x1_s, g, tm):
    rows, F = f.shape
    rows_s = f_s.shape[0]
    D = w_down.shape[1]
    n_tiles = rows // tm
    chunk = _pick_tile(F, 512)
    while (F // chunk) % 2 == 0 and chunk * D * 4 > 4 * 1024 * 1024:
        chunk //= 2
    assert chunk % 16 == 0
    tile = lambda r: pl.BlockSpec((tm, r), lambda i: (jnp.minimum(i, n_tiles - 1), 0))
    whole = lambda r, c: pl.BlockSpec((r, c), lambda i: (0, 0))
    block_bytes = (2 * tm * F * 2 + 2 * rows_s * F * 2 + F * D * 2 + 2 * chunk * D * 4 + 4 * tm * D * 4
                   + 4 * rows_s * D * 4 + 2 * tm * D * 4)
    return pl.pallas_call(
        functools.partial(_ffn_down_kernel, n_tiles=n_tiles, chunk=chunk),
        grid=(n_tiles + 1,),
        in_specs=[tile(F), whole(rows_s, F), pl.BlockSpec(memory_space=pl.ANY), tile(D), whole(rows_s, D),
                  whole(1, D)],
        out_specs=[tile(D), whole(rows_s, D)],
        out_shape=[jax.ShapeDtypeStruct((rows, D), F32), jax.ShapeDtypeStruct((rows_s, D), F32)],
        scratch_shapes=[pltpu.VMEM((F, D), BF16), pltpu.VMEM((2, chunk, D), F32), pltpu.SemaphoreType.DMA((2,))],
        compiler_params=_resident_params(("arbitrary",), block_bytes),
        name="ffn_down",
    )(f, f_s, w_down, x1, x1_s, g.reshape(1, D))


def _pick_tile(n, target):
    t = min(n, target)
    while n % t:
        t //= 2
    return t


def _mixer_inputs(x, x_tm, pool_hist, page_table, lp, n_seq, n_steps):
    B, S, D = x.shape
    tm = _pick_tile(S, 512)
    past_len = page_table.shape[1] * PAGE_SIZE
    pos_s = past_len + jnp.repeat(jnp.arange(n_steps), n_seq)
    u, q, k, v, u_s, q_s, k_s, v_s = _inproj(x, x_tm[None], lp["g_mix_pre"], lp["w_in"],
                                             jnp.arange(S), pos_s, tm)
    pooled = _pool_prompt(u, lp["w_pool"], lp["pool_scale"], tm)
    attn = _moba_prompt(q, k, v)
    x1, h2 = _outproj(pooled.reshape(B * S, -1), attn.reshape(B * S, -1), x.reshape(B * S, D),
                      lp["w_out"], lp["g_mix_post"], lp["g_ffn_pre"], tm)

    def to_seq_major(t):
        H = t.shape[0]
        return t.reshape(H, n_steps, n_seq, HEAD_DIM).transpose(2, 0, 1, 3)

    pooled_s = _pool_sample(u_s[0], _seq_to_rows(pool_hist), lp["w_pool"], lp["pool_scale"],
                            n_steps, n_seq, past_len)
    sample = (to_seq_major(q_s[0]), to_seq_major(k_s[0]), to_seq_major(v_s[0]), pooled_s, u_s[0])
    return (x1, h2, k, v, u), sample


def _rows_to_seq(t, n_steps, n_seq):
    return t.reshape(n_steps, n_seq, t.shape[-1]).transpose(1, 0, 2)


def _seq_to_rows(t):
    return t.transpose(1, 0, 2).reshape(-1, t.shape[-1])


def _layer(xp, xs_tm, pool_hist, conv_hist, cache_k, cache_v, page_table, lp, n_seq, n_steps):
    B, S, D = xp.shape
    F = lp["w_gate"].shape[1]
    rows_s = xs_tm.shape[0]
    (x1p, h2p, kp, vp, up), (qs, ks, vs, pooled_s, us) = _mixer_inputs(xp, xs_tm, pool_hist, page_table, lp,
                                                                       n_seq, n_steps)
    tm = _pick_tile(S, 512)
    tn = _pick_tile(F, 512)
    fp, tail, attn_s = _ffn_up(h2p.reshape(B, S, D), lp["w_gate"], lp["w_up"], lp["conv_w"], lp["conv_b"],
                               qs, ks, vs, cache_k, cache_v, page_table, tm, tn, _pick_tile(tm, 256))
    attn_tm = attn_s.transpose(2, 0, 1, 3).reshape(rows_s, -1).astype(BF16)
    x1s, h2s = _outproj(pooled_s, attn_tm, xs_tm, lp["w_out"], lp["g_mix_post"], lp["g_ffn_pre"], rows_s)
    fs, gts = _ffn_up_sample(h2s, lp["w_gate"], lp["w_up"], lp["conv_w"], lp["conv_b"],
                             _seq_to_rows(conv_hist), n_steps, n_seq, tn)
    yp, ys = _ffn_down(fp.reshape(B * S, F), fs, lp["w_down"], x1p, x1s, lp["g_ffn_post"],
                       _pick_tile(B * S, 256))
    pool_p = up[:, S - POOL_HIST:, :]
    conv_p = tail[:, V7X_SUBLANES - (CONV_W - 1):, :]
    pool_s = jnp.concatenate([pool_hist, _rows_to_seq(us, n_steps, n_seq)], axis=1)[:, n_steps:]
    conv_s = jnp.concatenate([conv_hist, _rows_to_seq(gts, n_steps, n_seq)], axis=1)[:, n_steps:]
    return yp.reshape(B, S, D), ys, (kp, vp, ks, vs, pool_p, pool_s, conv_p, conv_s)


def kernel(x_prompt, x_sample, cache_k, cache_v, state_pool, state_conv, page_table,
           w_in, w_pool, pool_scale, w_out, g_mix_pre, g_mix_post,
           w_gate, w_up, conv_w, conv_b, w_down, g_ffn_pre, g_ffn_post):
    depth = w_in.shape[0]
    n_seq, n_steps, D = x_sample.shape
    assert w_in.shape[2] == 4 * pool_scale.shape[1], "pooling and attention widths must match"
    yp = x_prompt
    ys = x_sample.transpose(1, 0, 2).reshape(n_steps * n_seq, D)
    outs = [[] for _ in range(8)]
    for l in range(depth):
        lp = {"w_in": w_in[l], "w_pool": w_pool[l].astype(BF16), "pool_scale": pool_scale[l],
              "w_out": w_out[l].astype(BF16), "g_mix_pre": g_mix_pre[l], "g_mix_post": g_mix_post[l],
              "w_gate": w_gate[l], "w_up": w_up[l], "conv_w": conv_w[l],
              "conv_b": conv_b[l], "w_down": w_down[l], "g_ffn_pre": g_ffn_pre[l],
              "g_ffn_post": g_ffn_post[l]}
        yp, ys, states = _layer(yp, ys, state_pool[l], state_conv[l], cache_k[l], cache_v[l],
                                page_table, lp, n_seq, n_steps)
        for lst, val in zip(outs, states):
            lst.append(val)
    y_sample = ys.reshape(n_steps, n_seq, D).transpose(1, 0, 2)
    return (yp, y_sample) + tuple(jnp.stack(o) for o in outs)
```

```python
import functools

import jax
import jax.numpy as jnp
from jax import lax
from jax.experimental import pallas as pl
from jax.experimental.pallas import tpu as pltpu

F32 = jnp.float32
BF16 = jnp.bfloat16

POOL_WINDOWS = (2, 4, 8, 16)
POOL_HIST = max(POOL_WINDOWS) - 1
HEAD_DIM = 128
ROT_DIM = HEAD_DIM // 4
ROPE_THETA = 500000.0
MOBA_BLOCK = 256
MOBA_TOPK = 3
QUERY_BLOCKS_PER_DOT = 2
PAGE_SIZE = 128
CONV_W = 3
EPS = 1e-6
NEG_INF = float("-inf")
LOG2E = 1.4426950408889634

V7X_SUBLANES = 8
V7X_VMEM_BYTES = 64 * 1024 * 1024
VMEM_RESERVE_BYTES = 4 * 1024 * 1024


def _vmem_limit(block_bytes):
    return int(min(2 * block_bytes, V7X_VMEM_BYTES - VMEM_RESERVE_BYTES))


def _params(semantics, block_bytes):
    return pltpu.CompilerParams(dimension_semantics=semantics,
                                vmem_limit_bytes=_vmem_limit(block_bytes))


def _resident_params(semantics, block_bytes):
    limit = int(min(block_bytes + VMEM_RESERVE_BYTES, V7X_VMEM_BYTES - VMEM_RESERVE_BYTES))
    return pltpu.CompilerParams(dimension_semantics=semantics, vmem_limit_bytes=limit)


def _rms(x, g):
    return x * lax.rsqrt(jnp.mean(x * x, axis=-1, keepdims=True) + EPS) * g


def _dot(a, b):
    return jnp.dot(a, b, preferred_element_type=F32)


def _dot_nt(a, b, precision=None):
    return lax.dot_general(a, b, (((1,), (1,)), ((), ())), precision=precision,
                           preferred_element_type=F32)


def _rope_tables(pos):
    half = ROT_DIM // 2
    inv = ROPE_THETA ** (-jnp.arange(half, dtype=F32) * (2.0 / ROT_DIM))
    ang = pos.astype(F32)[:, None] * inv[None, :]
    cos, sin = jnp.cos(ang), jnp.sin(ang)
    n = pos.shape[0]
    c = jnp.concatenate([cos, cos, jnp.ones((n, HEAD_DIM - ROT_DIM), F32)], axis=1)
    s_lo = jnp.concatenate([-sin, jnp.zeros((n, HEAD_DIM - half), F32)], axis=1)
    s_hi = jnp.concatenate([jnp.zeros((n, half), F32), sin,
                            jnp.zeros((n, HEAD_DIM - ROT_DIM), F32)], axis=1)
    return c, s_lo, s_hi


def _round_weight_once(w_hbm, wb_ref, stage_ref, sem, chunk):
    n_chunks = wb_ref.shape[0] // chunk

    def w_copy(c):
        return pltpu.make_async_copy(w_hbm.at[pl.ds(c * chunk, chunk), :], stage_ref.at[c % 2], sem.at[c % 2])

    w_copy(0).start()
    for c in range(n_chunks):
        if c + 1 < n_chunks:
            w_copy(c + 1).start()
        w_copy(c).wait()
        wb_ref[c * chunk:(c + 1) * chunk, :] = stage_ref[c % 2].astype(BF16)


def _inproj_kernel(x_ref, xs_ref, g_ref, w_hbm, c_ref, slo_ref, shi_ref, cs_ref, slos_ref, shis_ref,
                   u_ref, q_ref, k_ref, v_ref, us_ref, qs_ref, ks_ref, vs_ref,
                   wb_ref, hn_ref, stage_ref, sem, *, n_heads, width, n_tiles, chunk):
    i = pl.program_id(0)
    pair = 2 * HEAD_DIM

    @pl.when(i == 0)
    def _():
        _round_weight_once(w_hbm, wb_ref, stage_ref, sem, chunk)

    def project(x, tables, outs):
        c, s_lo, s_hi = (t[...] for t in tables)
        u_out, q_out, k_out, v_out = outs
        rows = x.shape[0]
        hn_ref[0:rows, :] = _rms(x, g_ref[...]).astype(BF16)

        def rope(h):
            return (h * c + pltpu.roll(h, HEAD_DIM - ROT_DIM // 2, 1) * s_lo
                    + pltpu.roll(h, ROT_DIM // 2, 1) * s_hi)

        for p in range(width // pair):
            u_out[0, :, p * pair:(p + 1) * pair] = _dot(hn_ref[0:rows, :], wb_ref[:, p * pair:(p + 1) * pair])
        for out_ref, part, rotary in ((q_out, 1, True), (k_out, 2, True), (v_out, 3, False)):
            for p in range(n_heads // 2):
                col = part * width + p * pair
                res = _dot(hn_ref[0:rows, :], wb_ref[:, col:col + pair])
                for half in range(2):
                    h = res[:, half * HEAD_DIM:(half + 1) * HEAD_DIM]
                    out_ref[0, 2 * p + half] = rope(h) if rotary else h

    @pl.when(i < n_tiles)
    def _():
        project(x_ref[0], (c_ref, slo_ref, shi_ref), (u_ref, q_ref, k_ref, v_ref))

    @pl.when(i == n_tiles)
    def _():
        project(xs_ref[0], (cs_ref, slos_ref, shis_ref), (us_ref, qs_ref, ks_ref, vs_ref))


def _inproj(x, x_s, g, w_in, pos, pos_s, tm):
    B, S, D = x.shape
    rows_s = x_s.shape[1]
    width = w_in.shape[1] // 4
    n_heads = width // HEAD_DIM
    assert n_heads % 2 == 0 and rows_s <= tm
    n_s = S // tm
    n_tiles = B * n_s
    chunk = _pick_tile(D, 256)

    def tile(i):
        ii = jnp.minimum(i, n_tiles - 1)
        return ii // n_s, lax.rem(ii, n_s)

    tab_spec = pl.BlockSpec((tm, HEAD_DIM), lambda i: (tile(i)[1], 0))
    hm_spec = pl.BlockSpec((1, n_heads, tm, HEAD_DIM), lambda i: (tile(i)[0], 0, tile(i)[1], 0))
    hm_shape = jax.ShapeDtypeStruct((B, n_heads, S, HEAD_DIM), F32)
    whole = lambda shape: pl.BlockSpec(shape, lambda i: (0,) * len(shape))
    stab_spec = whole((rows_s, HEAD_DIM))
    shm_spec = whole((1, n_heads, rows_s, HEAD_DIM))
    shm_shape = jax.ShapeDtypeStruct((1, n_heads, rows_s, HEAD_DIM), F32)
    block_bytes = (2 * tm * D * 4 + tm * D * 2 + D * 4 * width * 2 + 2 * chunk * 4 * width * 4
                   + 2 * 4 * tm * width * 4 + 2 * rows_s * (D + 4 * width) * 4)
    return pl.pallas_call(
        functools.partial(_inproj_kernel, n_heads=n_heads, width=width, n_tiles=n_tiles, chunk=chunk),
        grid=(n_tiles + 1,),
        in_specs=[
            pl.BlockSpec((1, tm, D), lambda i: (tile(i)[0], tile(i)[1], 0)),
            whole((1, rows_s, D)),
            whole((1, D)),
            pl.BlockSpec(memory_space=pl.ANY),
            tab_spec, tab_spec, tab_spec, stab_spec, stab_spec, stab_spec,
        ],
        out_specs=[pl.BlockSpec((1, tm, width), lambda i: (tile(i)[0], tile(i)[1], 0)), hm_spec, hm_spec, hm_spec,
                   whole((1, rows_s, width)), shm_spec, shm_spec, shm_spec],
        out_shape=[jax.ShapeDtypeStruct((B, S, width), F32), hm_shape, hm_shape, hm_shape,
                   jax.ShapeDtypeStruct((1, rows_s, width), F32), shm_shape, shm_shape, shm_shape],
        scratch_shapes=[pltpu.VMEM((D, 4 * width), BF16), pltpu.VMEM((tm, D), BF16),
                        pltpu.VMEM((2, chunk, 4 * width), F32), pltpu.SemaphoreType.DMA((2,))],
        compiler_params=_resident_params(("arbitrary",), block_bytes),
        name="inproj",
    )(x, x_s, g.reshape(1, D), w_in, *_rope_tables(pos), *_rope_tables(pos_s))


def _shift_rows(x, prev8, k):
    pad = V7X_SUBLANES
    rolled = pltpu.roll(x, k, 0)
    row8 = lax.broadcasted_iota(jnp.int32, prev8.shape, 0)
    head = jnp.where(row8 < k, pltpu.roll(prev8, k % pad, 0) if k % pad else prev8, rolled[0:pad])
    return jnp.concatenate([head, rolled[pad:]], axis=0)


def _pool_kernel(u_ref, w_ref, scale_ref, o_ref, halo_ref, *, ts, group):
    s = pl.program_id(1)
    pad = V7X_SUBLANES

    @pl.when(s == 0)
    def _():
        halo_ref[...] = jnp.zeros(halo_ref.shape, F32)

    pos = s * ts + lax.broadcasted_iota(jnp.int32, (ts, 1), 0)
    slot = 0
    for gi, w in enumerate(POOL_WINDOWS):
        cols = slice(gi * group, (gi + 1) * group)
        x = u_ref[0, :, cols]
        acc = x
        span = 1
        while span < w:
            prev8 = halo_ref[slot]
            halo_ref[slot] = acc[ts - pad:ts, :]
            acc = acc + _shift_rows(acc, prev8, span)
            slot += 1
            span *= 2
        cnt = jnp.minimum(pos + 1, w).astype(F32)
        d = (acc / cnt - x).astype(BF16)
        y = _dot(d, w_ref[gi]) * scale_ref[:, cols]
        o_ref[0, :, cols] = y.astype(BF16)


def _pool_prompt(u, w_pool_bf, scale, ts):
    B, S, C = u.shape
    n_groups, group, _ = w_pool_bf.shape
    assert all(w & (w - 1) == 0 and w <= 2 * V7X_SUBLANES for w in POOL_WINDOWS)
    n_levels = sum(w.bit_length() - 1 for w in POOL_WINDOWS)
    block_bytes = 2 * ts * C * 4 + 2 * n_groups * group * group * 2 + 2 * ts * C * 2 + 6 * ts * group * 4
    return pl.pallas_call(
        functools.partial(_pool_kernel, ts=ts, group=group),
        grid=(B, S // ts),
        in_specs=[
            pl.BlockSpec((1, ts, C), lambda b, s: (b, s, 0)),
            pl.BlockSpec((n_groups, group, group), lambda b, s: (0, 0, 0)),
            pl.BlockSpec((1, C), lambda b, s: (0, 0)),
        ],
        out_specs=pl.BlockSpec((1, ts, C), lambda b, s: (b, s, 0)),
        out_shape=jax.ShapeDtypeStruct((B, S, C), BF16),
        scratch_shapes=[pltpu.VMEM((n_levels, V7X_SUBLANES, group), F32)],
        compiler_params=_params(("arbitrary", "arbitrary"), block_bytes),
        name="pool_prompt",
    )(u, w_pool_bf, scale.reshape(1, C))


def _pool_sample_kernel(u_ref, hist_ref, w_ref, scale_ref, o_ref, *, n_steps, n_seq, group, past_len):
    def ext(i):
        if i < POOL_HIST:
            return hist_ref[i * n_seq:(i + 1) * n_seq, :]
        return u_ref[(i - POOL_HIST) * n_seq:(i - POOL_HIST + 1) * n_seq, :]

    for gi, w in enumerate(POOL_WINDOWS):
        cols = slice(gi * group, (gi + 1) * group)
        ds = []
        for t in range(n_steps):
            cur = ext(POOL_HIST + t)[:, cols]
            acc = cur
            for back in range(1, w):
                acc = acc + ext(POOL_HIST + t - back)[:, cols]
            cnt = float(min(past_len + t + 1, w))
            ds.append((acc / cnt - cur).astype(BF16))
        d = jnp.concatenate(ds, axis=0)
        y = _dot(d, w_ref[gi]) * scale_ref[:, cols]
        o_ref[:, cols] = y.astype(BF16)


def _pool_sample(u_tm, hist_tm, w_pool_bf, scale, n_steps, n_seq, past_len):
    rows, C = u_tm.shape
    n_groups, group, _ = w_pool_bf.shape
    block_bytes = 2 * (rows + hist_tm.shape[0]) * C * 4 + 2 * n_groups * group * group * 2 + 2 * rows * C * 2
    return pl.pallas_call(
        functools.partial(_pool_sample_kernel, n_steps=n_steps, n_seq=n_seq, group=group,
                          past_len=past_len),
        out_shape=jax.ShapeDtypeStruct((rows, C), BF16),
        compiler_params=pltpu.CompilerParams(vmem_limit_bytes=_vmem_limit(block_bytes)),
        name="pool_sample",
    )(u_tm, hist_tm, w_pool_bf, scale.reshape(1, C))


class _PerHead:
    def __init__(self, refs):
        self.refs = refs

    def __getitem__(self, idx):
        return self.refs[idx[0]][idx[1:]] if isinstance(idx, tuple) else self.refs[idx][...]

    def __setitem__(self, idx, value):
        self.refs[idx[0]][idx[1:]] = value


def _moba_prompt_kernel(q_ref, k_ref, v_ref, o_ref, *scratch, seq, hp):
    blk = MOBA_BLOCK
    nb = seq // blk
    n_sel = min(MOBA_TOPK, nb - 1)
    scale = HEAD_DIM ** -0.5
    kb_ref, vt_ref, km_ref, s_ref, p_ref = (_PerHead(scratch[kind * hp:(kind + 1) * hp]) for kind in range(5))

    for hh in range(hp):
        for n in range(nb):
            rows = slice(n * blk, (n + 1) * blk)
            kn = k_ref[0, hh, rows, :]
            kb_ref[hh, rows, :] = kn.astype(BF16)
            km_ref[hh, n:n + 1, :] = jnp.sum(kn, axis=0, keepdims=True) * (1.0 / blk)
            vt_ref[hh, :, rows] = v_ref[0, hh, rows, :].T.astype(BF16)

    blk_id = lax.broadcasted_iota(jnp.int32, (nb, blk), 0)
    key_i = lax.broadcasted_iota(jnp.int32, (blk, blk), 0)
    qry_i = lax.broadcasted_iota(jnp.int32, (blk, blk), 1)
    causal_bias = jnp.where(key_i <= qry_i, 0.0, NEG_INF).astype(F32)

    def scores(hh, qblocks):
        first, last = qblocks[0], qblocks[-1]
        nk = (last + 1) * blk
        width = len(qblocks) * blk
        qg = q_ref[0, hh, first * blk:nk, :]
        s_ref[hh, 0:nk, 0:width] = _dot_nt(kb_ref[hh, 0:nk, :], (qg * (scale * LOG2E)).astype(BF16))
        return _dot_nt(km_ref[hh], qg, precision=lax.Precision.HIGHEST) if last > 0 else None

    def softmax(hh, qblocks, gate):
        last = qblocks[-1]
        sums = []
        for a, i in enumerate(qblocks):
            cols = slice(a * blk, (a + 1) * blk)
            own = slice(i * blk, (i + 1) * blk)
            biases = []
            if i > 0:
                gate_a = gate[:, cols]
                valid = jnp.where(blk_id < i, 1.0, 0.0)
                for n in range(i):
                    gn = gate_a[n:n + 1, :]
                    beats = jnp.where(blk_id < n, jnp.where(gate_a >= gn, 1.0, 0.0),
                                      jnp.where(gate_a > gn, 1.0, 0.0))
                    rank = jnp.sum(beats * valid, axis=0, keepdims=True)
                    biases.append(jnp.where(rank < n_sel, 0.0, NEG_INF).astype(F32))

            s_own = s_ref[hh, own, cols] + causal_bias
            m = jnp.max(s_own, axis=0, keepdims=True)
            for n in range(i):
                m = jnp.maximum(m, jnp.max(s_ref[hh, n * blk:(n + 1) * blk, cols], axis=0, keepdims=True)
                                + biases[n])

            p = jnp.exp2(s_own - m)
            l = jnp.sum(p, axis=0, keepdims=True)
            p_ref[hh, own, cols] = p.astype(BF16)
            for n in range(i):
                rows = slice(n * blk, (n + 1) * blk)
                p = jnp.exp2(s_ref[hh, rows, cols] + (biases[n] - m))
                l = l + jnp.sum(p, axis=0, keepdims=True)
                p_ref[hh, rows, cols] = p.astype(BF16)
            for n in range(i + 1, last + 1):
                p_ref[hh, n * blk:(n + 1) * blk, cols] = jnp.zeros((blk, blk), BF16)
            sums.append(l)
        return sums

    def output(hh, qblocks, sums):
        nk = (qblocks[-1] + 1) * blk
        width = len(qblocks) * blk
        o_t = _dot(vt_ref[hh, :, 0:nk], p_ref[hh, 0:nk, 0:width])
        for a, i in enumerate(qblocks):
            o_a = o_t[:, a * blk:(a + 1) * blk] / sums[a]
            o_ref[0, i * blk:(i + 1) * blk, hh * HEAD_DIM:(hh + 1) * HEAD_DIM] = o_a.T.astype(BF16)

    items = [(hh, list(range(first, min(first + QUERY_BLOCKS_PER_DOT, nb))))
             for first in range(0, nb, QUERY_BLOCKS_PER_DOT) for hh in range(hp)]
    gate = scores(*items[0])
    for cur, nxt in zip(items, items[1:] + [None]):
        next_gate = scores(*nxt) if nxt is not None else None
        output(*cur, softmax(*cur, gate))
        gate = next_gate


def _moba_prompt(q, k, v):
    B, H, S, Dh = q.shape
    nb = S // MOBA_BLOCK
    hp = 2 if H % 2 == 0 else 1
    width = QUERY_BLOCKS_PER_DOT * MOBA_BLOCK
    in_spec = pl.BlockSpec((1, hp, S, Dh), lambda b, g: (b, g, 0, 0))
    block_bytes = hp * (2 * 3 * S * Dh * 4 + 2 * S * Dh * 2 + 2 * S * Dh * 2 + S * width * 6)
    return pl.pallas_call(
        functools.partial(_moba_prompt_kernel, seq=S, hp=hp),
        grid=(B, H // hp),
        in_specs=[in_spec, in_spec, in_spec],
        out_specs=pl.BlockSpec((1, S, hp * Dh), lambda b, g: (b, 0, g)),
        out_shape=jax.ShapeDtypeStruct((B, S, H * Dh), BF16),
        scratch_shapes=(
            [pltpu.VMEM((S, Dh), BF16)] * hp
            + [pltpu.VMEM((Dh, S), BF16)] * hp
            + [pltpu.VMEM((nb, Dh), F32)] * hp
            + [pltpu.VMEM((S, width), F32)] * hp
            + [pltpu.VMEM((S, width), BF16)] * hp
        ),
        compiler_params=_params(("arbitrary", "arbitrary"), block_bytes),
        name="moba_prompt",
    )(q, k, v)


class _PagedAttention:
    def __init__(self, pt_ref, qc_ref, qp_ref, kn_ref, vn_ref, ck_ref, cv_ref, o_ref,
                 kbuf, ksel, vbuf, km_ref, q8_ref, sel_ref, ksem, vsem, *, n_pages, n_groups, hp, n_new):
        self.__dict__.update(locals())
        self.ppb = MOBA_BLOCK // PAGE_SIZE
        self.nblk = n_pages // self.ppb
        self.n_sel = min(MOBA_TOPK, self.nblk)

    def unit_bh(self, u):
        return u // self.n_groups, lax.rem(u, self.n_groups) * self.hp

    def k_copy(self, u, page):
        bb, h0 = self.unit_bh(u)
        sl = lax.rem(u, 2)
        return pltpu.make_async_copy(self.ck_ref.at[self.pt_ref[bb, page], pl.ds(h0, self.hp)],
                                     self.kbuf.at[sl, :, pl.ds(page * PAGE_SIZE, PAGE_SIZE), :],
                                     self.ksem.at[sl])

    def start_k(self, u):
        def body(page, carry):
            self.k_copy(u, page).start()
            return carry
        lax.fori_loop(0, self.n_pages, body, 0, unroll=8)

    def wait_k(self, u):
        for page in range(self.n_pages):
            self.k_copy(u, page).wait()

    def sel_index(self, hh, j, c):
        return (hh * self.n_new + j) * self.n_sel + c

    def v_copies(self, u, heads):
        bb, h0 = self.unit_bh(u)
        sl = lax.rem(u, 2)
        for hh in heads:
            for j in range(self.n_new):
                for c in range(self.n_sel):
                    blk = self.sel_ref[sl, self.sel_index(hh, j, c)]
                    for pg in range(self.ppb):
                        page = self.pt_ref[bb, blk * self.ppb + pg]
                        dst = ((j * self.n_sel + c) * self.ppb + pg) * PAGE_SIZE
                        yield pltpu.make_async_copy(self.cv_ref.at[page, h0 + hh],
                                                    self.vbuf.at[sl, hh, pl.ds(dst, PAGE_SIZE), :],
                                                    self.vsem.at[sl])

    def start_v(self, u, heads):
        for cp in self.v_copies(u, heads):
            cp.start()

    def wait_v(self, u):
        for cp in self.v_copies(u, range(self.hp)):
            cp.wait()

    def block_means(self, u, heads):
        sl = lax.rem(u, 2)
        for hh in heads:
            for n in range(self.nblk):
                kn_blk = self.kbuf[sl, hh, n * MOBA_BLOCK:(n + 1) * MOBA_BLOCK, :]
                self.km_ref[hh, n:n + 1, :] = jnp.sum(kn_blk, axis=0, keepdims=True) * (1.0 / MOBA_BLOCK)

    def select(self, u, heads):
        sl = lax.rem(u, 2)
        for hh in heads:
            self.q8_ref[hh] = jnp.zeros(self.q8_ref.shape[1:], F32)
            self.q8_ref[hh, 0:self.n_new, :] = self.qc_ref[0, hh]
            gate = _dot_nt(self.q8_ref[hh], self.km_ref[hh], precision=lax.Precision.HIGHEST)
            lane = lax.broadcasted_iota(jnp.int32, gate.shape, 1)
            picks = [jnp.zeros((gate.shape[0], 1), F32) for _ in range(self.n_sel)]
            for n in range(self.nblk):
                gn = gate[:, n:n + 1]
                beats = jnp.where(lane < n, jnp.where(gate >= gn, 1.0, 0.0), jnp.where(gate > gn, 1.0, 0.0))
                rank = jnp.sum(beats, axis=1, keepdims=True)
                for c in range(self.n_sel):
                    picks[c] = jnp.where(rank == float(c), float(n), picks[c])
            for j in range(self.n_new):
                for c in range(self.n_sel):
                    blk = picks[c][j, 0].astype(jnp.int32)
                    self.sel_ref[sl, self.sel_index(hh, j, c)] = blk
                    row0 = (j * self.n_sel + c) * MOBA_BLOCK
                    start = pl.multiple_of(blk * MOBA_BLOCK, MOBA_BLOCK)
                    self.ksel[sl, hh, row0:row0 + MOBA_BLOCK, :] = self.kbuf[sl, hh, pl.ds(start, MOBA_BLOCK), :]

    def attend(self, u, heads):
        sl = lax.rem(u, 2)
        scale = HEAD_DIM ** -0.5
        new_row = lax.broadcasted_iota(jnp.int32, (self.n_new, 1), 0)
        for hh in heads:
            q = self.qp_ref[0, hh]
            kn = self.kn_ref[0, hh]
            vn = self.vn_ref[0, hh]
            for j in range(self.n_new):
                qj = q[j:j + 1, :] * (scale * LOG2E)
                rows = [slice((j * self.n_sel + c) * MOBA_BLOCK, (j * self.n_sel + c + 1) * MOBA_BLOCK)
                        for c in range(self.n_sel)]
                s_sel = [jnp.sum(self.ksel[sl, hh, r, :] * qj, axis=1, keepdims=True) for r in rows]
                s_new = jnp.sum(kn * qj, axis=1, keepdims=True)
                s_new = jnp.where(new_row <= j, s_new, NEG_INF)
                m = jnp.max(s_new, axis=0, keepdims=True)
                for s in s_sel:
                    m = jnp.maximum(m, jnp.max(s, axis=0, keepdims=True))
                p_new = jnp.exp2(s_new - m)
                l = jnp.sum(p_new, axis=0, keepdims=True)
                acc = jnp.sum(p_new * vn, axis=0, keepdims=True)
                for s, r in zip(s_sel, rows):
                    p = jnp.exp2(s - m)
                    l = l + jnp.sum(p, axis=0, keepdims=True)
                    acc = acc + jnp.sum(p * self.vbuf[sl, hh, r, :], axis=0, keepdims=True)
                self.o_ref[0, hh, j:j + 1, :] = acc / l


def _paged_attention_scratch(n_pages, hp, n_new):
    nblk = n_pages // (MOBA_BLOCK // PAGE_SIZE)
    n_sel = min(MOBA_TOPK, nblk)
    picked_rows = n_new * n_sel * MOBA_BLOCK
    shapes = [
        pltpu.VMEM((2, hp, n_pages * PAGE_SIZE, HEAD_DIM), F32),
        pltpu.VMEM((2, hp, picked_rows, HEAD_DIM), F32),
        pltpu.VMEM((2, hp, picked_rows, HEAD_DIM), F32),
        pltpu.VMEM((hp, nblk, HEAD_DIM), F32),
        pltpu.VMEM((hp, V7X_SUBLANES, HEAD_DIM), F32),
        pltpu.SMEM((2, hp * n_new * n_sel), jnp.int32),
        pltpu.SemaphoreType.DMA((2,)),
        pltpu.SemaphoreType.DMA((2,)),
    ]
    n_bytes = 4 * HEAD_DIM * 2 * hp * (n_pages * PAGE_SIZE + 2 * picked_rows)
    return shapes, n_bytes


def _outproj_kernel(pooled_ref, attn_ref, x_ref, w_ref, gpost_ref, gffn_ref, x1_ref, h2_ref, *, split, sub):
    for r in range(x_ref.shape[0] // sub):
        rows = slice(r * sub, (r + 1) * sub)
        mix = _dot(pooled_ref[rows, :], w_ref[0:split, :]) + _dot(attn_ref[rows, :], w_ref[split:, :])
        x1 = x_ref[rows, :] + _rms(mix, gpost_ref[...])
        x1_ref[rows, :] = x1
        h2_ref[rows, :] = _rms(x1, gffn_ref[...]).astype(BF16)


def _outproj(pooled, attn, x, w_bf, g_post, g_ffn, tm):
    rows, D = x.shape
    split = pooled.shape[1]
    wa = attn.shape[1]
    row_spec = lambda width: pl.BlockSpec((tm, width), lambda i: (i, 0))
    vec_spec = pl.BlockSpec((1, D), lambda i: (0, 0))
    block_bytes = 2 * tm * (split + wa) * 2 + 2 * tm * D * 4 + 2 * (split + wa) * D * 2 \
        + 2 * tm * D * 6 + 2 * tm * D * 4
    return pl.pallas_call(
        functools.partial(_outproj_kernel, split=split, sub=_pick_tile(tm, 128)),
        grid=(rows // tm,),
        in_specs=[row_spec(split), row_spec(wa), row_spec(D),
                  pl.BlockSpec((split + wa, D), lambda i: (0, 0)), vec_spec, vec_spec],
        out_specs=[row_spec(D), row_spec(D)],
        out_shape=[jax.ShapeDtypeStruct((rows, D), F32), jax.ShapeDtypeStruct((rows, D), BF16)],
        compiler_params=_params(("arbitrary",), block_bytes),
        name="outproj",
    )(pooled, attn, x, w_bf, g_post.reshape(1, D), g_ffn.reshape(1, D))


def _gelu_tanh(c):
    return c * (0.5 * (1.0 + jnp.tanh(0.7978845608028654 * (c + 0.044715 * (c * c * c)))))


def _conv_gelu_gate(gt, prev8, up, cw_ref, cb_ref):
    c = cb_ref[...]
    for i in range(CONV_W):
        back = CONV_W - 1 - i
        c = c + (_shift_rows(gt, prev8, back) if back else gt) * cw_ref[i:i + 1, :]
    return _gelu_tanh(c) * up


def _ffn_up_kernel(pt_ref, h_ref, wg_ref, wu_ref, cw_ref, cb_ref, qc_ref, qp_ref, kn_ref, vn_ref, ck_ref, cv_ref,
                   f_ref, tail_ref, o_ref, wgb_ref, wub_ref, halo_ref, *attn_scratch,
                   tm, sub, n_b, n_s, n_units, attn_params):
    j, b, s = pl.program_id(0), pl.program_id(1), pl.program_id(2)
    t = (j * n_b + b) * n_s + s
    pad = V7X_SUBLANES
    last_sub = tm // sub - 1
    attn = _PagedAttention(pt_ref, qc_ref, qp_ref, kn_ref, vn_ref, ck_ref, cv_ref, o_ref,
                           *attn_scratch, **attn_params)

    @pl.when((b == 0) & (s == 0))
    def _():
        wgb_ref[...] = wg_ref[...].astype(BF16)
        wub_ref[...] = wu_ref[...].astype(BF16)

    @pl.when(s == 0)
    def _():
        halo_ref[...] = jnp.zeros(halo_ref.shape, F32)

    @pl.when(t == 0)
    def _():
        attn.start_k(t)

    @pl.when(t < n_units)
    def _():
        attn.wait_k(t)

    @pl.when(t + 1 < n_units)
    def _():
        attn.start_k(t + 1)

    def gate_up(before=None, middle=None, end=()):
        before, middle = before or {}, middle or {}
        prev8 = halo_ref[...]
        half = sub // 2
        for r in range(tm // sub):
            for stage in before.get(r, ()):
                stage()
            row0 = r * sub
            if r in middle:
                lo = h_ref[0, row0:row0 + half, :]
                gt_lo, up_lo = _dot(lo, wgb_ref[...]), _dot(lo, wub_ref[...])
                for stage in middle[r]:
                    stage()
                hi = h_ref[0, row0 + half:row0 + sub, :]
                gt = jnp.concatenate([gt_lo, _dot(hi, wgb_ref[...])], axis=0)
                up = jnp.concatenate([up_lo, _dot(hi, wub_ref[...])], axis=0)
            else:
                hr = h_ref[0, row0:row0 + sub, :]
                gt, up = _dot(hr, wgb_ref[...]), _dot(hr, wub_ref[...])
            f_ref[0, row0:row0 + sub, :] = _conv_gelu_gate(gt, prev8, up, cw_ref, cb_ref).astype(BF16)
            prev8 = gt[sub - pad:sub, :]
        halo_ref[...] = prev8
        tail_ref[0] = prev8
        for stage in end:
            stage()

    heads = tuple(range(attn.hp))

    def means():
        attn.block_means(t, heads)

    def select():
        attn.select(t, heads)

    def attend_previous():
        attn.wait_v(t - 1)
        attn.attend(t - 1, heads)

    def fetch_values():
        attn.start_v(t, heads)

    @pl.when(t == 0)
    def _():
        gate_up(before={0: [means]}, middle={0: [select]}, end=[fetch_values])

    @pl.when((t >= 1) & (t < n_units))
    def _():
        before = {0: [means]}
        before.setdefault(last_sub, []).append(attend_previous)
        gate_up(before=before, middle={0: [select]}, end=[fetch_values])

    @pl.when(t == n_units)
    def _():
        gate_up(before={last_sub: [attend_previous]})

    @pl.when(t > n_units)
    def _():
        gate_up()


def _ffn_up(h2, w_gate, w_up, conv_w, conv_b, q, k_new, v_new, cache_k, cache_v, page_table, tm, tn, sub):
    B, S, D = h2.shape
    F = w_gate.shape[1]
    Bd, H, L, Dh = q.shape
    n_pages = page_table.shape[1]
    ppb = MOBA_BLOCK // PAGE_SIZE
    assert n_pages % ppb == 0, "past length must be a whole number of MoBA blocks"
    assert n_pages >= ppb and L <= V7X_SUBLANES
    hp = 2 if H % 2 == 0 else 1
    n_groups = H // hp
    n_units = Bd * n_groups
    n_b, n_s = B, S // tm
    assert (F // tn) * n_b * n_s > n_units >= 2, "not enough grid steps to host the sample attention"
    pad = V7X_SUBLANES

    def unit_spec(lag):
        def index_map(j, b, s, pt):
            u = jnp.clip((j * n_b + b) * n_s + s - lag, 0, n_units - 1)
            return (u // n_groups, lax.rem(u, n_groups), 0, 0)
        return pl.BlockSpec((1, hp, L, Dh), index_map)

    w_spec = pl.BlockSpec((D, tn), lambda j, b, s, pt: (0, j))
    any_spec = pl.BlockSpec(memory_space=pl.ANY)
    attn_scratch, attn_bytes = _paged_attention_scratch(n_pages, hp, L)
    block_bytes = (2 * tm * D * 2 + 2 * 2 * D * tn * 4 + 2 * D * tn * 2 + 2 * tm * tn * 2 + 8 * sub * tn * 4
                   + attn_bytes)
    grid_spec = pltpu.PrefetchScalarGridSpec(
        num_scalar_prefetch=1,
        grid=(F // tn, n_b, n_s),
        in_specs=[
            pl.BlockSpec((1, tm, D), lambda j, b, s, pt: (b, s, 0)),
            w_spec, w_spec,
            pl.BlockSpec((CONV_W, tn), lambda j, b, s, pt: (0, j)),
            pl.BlockSpec((1, tn), lambda j, b, s, pt: (0, j)),
            unit_spec(0), unit_spec(1), unit_spec(1), unit_spec(1), any_spec, any_spec,
        ],
        out_specs=[pl.BlockSpec((1, tm, tn), lambda j, b, s, pt: (b, s, j)),
                   pl.BlockSpec((1, pad, tn), lambda j, b, s, pt: (b, 0, j)),
                   unit_spec(1)],
        scratch_shapes=[pltpu.VMEM((D, tn), BF16), pltpu.VMEM((D, tn), BF16), pltpu.VMEM((pad, tn), F32)]
        + attn_scratch,
    )
    return pl.pallas_call(
        functools.partial(_ffn_up_kernel, tm=tm, sub=sub, n_b=n_b, n_s=n_s, n_units=n_units,
                          attn_params=dict(n_pages=n_pages, n_groups=n_groups, hp=hp, n_new=L)),
        grid_spec=grid_spec,
        out_shape=[jax.ShapeDtypeStruct((B, S, F), BF16), jax.ShapeDtypeStruct((B, pad, F), F32),
                   jax.ShapeDtypeStruct((Bd, H, L, Dh), F32)],
        compiler_params=_resident_params(("arbitrary", "arbitrary", "arbitrary"), block_bytes),
        name="ffn_up",
    )(page_table, h2, w_gate, w_up, conv_w, conv_b.reshape(1, F), q, q, k_new, v_new, cache_k, cache_v)


def _ffn_up_sample_kernel(h_ref, wg_ref, wu_ref, cw_ref, cb_ref, hist_ref, f_ref, gt_ref, *, n_steps, n_seq):
    h = h_ref[...]
    gt = _dot(h, wg_ref[...].astype(BF16))
    up = _dot(h, wu_ref[...].astype(BF16))
    gt_ref[...] = gt
    ext = [hist_ref[i * n_seq:(i + 1) * n_seq, :] for i in range(CONV_W - 1)]
    ext += [gt[t * n_seq:(t + 1) * n_seq, :] for t in range(n_steps)]
    for t in range(n_steps):
        c = cb_ref[...]
        for i in range(CONV_W):
            c = c + ext[t + i] * cw_ref[i:i + 1, :]
        rows = slice(t * n_seq, (t + 1) * n_seq)
        f_ref[rows, :] = (_gelu_tanh(c) * up[rows, :]).astype(BF16)


def _ffn_up_sample(h2, w_gate, w_up, conv_w, conv_b, hist_tm, n_steps, n_seq, tn):
    rows, D = h2.shape
    F = w_gate.shape[1]
    col_spec = lambda r: pl.BlockSpec((r, tn), lambda j: (0, j))
    block_bytes = 2 * rows * D * 2 + 2 * 2 * D * tn * 4 + 2 * D * tn * 2 + 2 * rows * tn * 6 \
        + 2 * hist_tm.shape[0] * tn * 4 + 4 * rows * tn * 4
    return pl.pallas_call(
        functools.partial(_ffn_up_sample_kernel, n_steps=n_steps, n_seq=n_seq),
        grid=(F // tn,),
        in_specs=[pl.BlockSpec((rows, D), lambda j: (0, 0)), col_spec(D), col_spec(D),
                  col_spec(CONV_W), col_spec(1), col_spec(hist_tm.shape[0])],
        out_specs=[col_spec(rows), col_spec(rows)],
        out_shape=[jax.ShapeDtypeStruct((rows, F), BF16), jax.ShapeDtypeStruct((rows, F), F32)],
        compiler_params=_params(("arbitrary",), block_bytes),
        name="ffn_up_sample",
    )(h2, w_gate, w_up, conv_w, conv_b.reshape(1, F), hist_tm)


def _ffn_down_kernel(f_ref, fs_ref, w_hbm, x1_ref, x1s_ref, g_ref, y_ref, ys_ref, wb_ref, stage_ref, sem,
                     *, n_tiles, chunk):
    i = pl.program_id(0)

    @pl.when(i == 0)
    def _():
        _round_weight_once(w_hbm, wb_ref, stage_ref, sem, chunk)

    @pl.when(i < n_tiles)
    def _():
        y_ref[...] = x1_ref[...] + _rms(_dot(f_ref[...], wb_ref[...]), g_ref[...])

    @pl.when(i == n_tiles)
    def _():
        ys_ref[...] = x1s_ref[...] + _rms(_dot(fs_ref[...], wb_ref[...]), g_ref[...])


def _ffn_down(f, f_s, w_down, x1, x1_s, g, tm):
    rows, F = f.shape
    rows_s = f_s.shape[0]
    D = w_down.shape[1]
    n_tiles = rows // tm
    chunk = _pick_tile(F, 512)
    while (F // chunk) % 2 == 0 and chunk * D * 4 > 4 * 1024 * 1024:
        chunk //= 2
    assert chunk % 16 == 0
    tile = lambda r: pl.BlockSpec((tm, r), lambda i: (jnp.minimum(i, n_tiles - 1), 0))
    whole = lambda r, c: pl.BlockSpec((r, c), lambda i: (0, 0))
    block_bytes = (2 * tm * F * 2 + 2 * rows_s * F * 2 + F * D * 2 + 2 * chunk * D * 4 + 4 * tm * D * 4
                   + 4 * rows_s * D * 4 + 2 * tm * D * 4)
    return pl.pallas_call(
        functools.partial(_ffn_down_kernel, n_tiles=n_tiles, chunk=chunk),
        grid=(n_tiles + 1,),
        in_specs=[tile(F), whole(rows_s, F), pl.BlockSpec(memory_space=pl.ANY), tile(D), whole(rows_s, D),
                  whole(1, D)],
        out_specs=[tile(D), whole(rows_s, D)],
        out_shape=[jax.ShapeDtypeStruct((rows, D), F32), jax.ShapeDtypeStruct((rows_s, D), F32)],
        scratch_shapes=[pltpu.VMEM((F, D), BF16), pltpu.VMEM((2, chunk, D), F32), pltpu.SemaphoreType.DMA((2,))],
        compiler_params=_resident_params(("arbitrary",), block_bytes),
        name="ffn_down",
    )(f, f_s, w_down, x1, x1_s, g.reshape(1, D))


def _pick_tile(n, target):
    t = min(n, target)
    while n % t:
        t //= 2
    return t


def _mixer_inputs(x, x_tm, pool_hist, page_table, lp, n_seq, n_steps):
    B, S, D = x.shape
    tm = _pick_tile(S, 512)
    past_len = page_table.shape[1] * PAGE_SIZE
    pos_s = past_len + jnp.repeat(jnp.arange(n_steps), n_seq)
    u, q, k, v, u_s, q_s, k_s, v_s = _inproj(x, x_tm[None], lp["g_mix_pre"], lp["w_in"],
                                             jnp.arange(S), pos_s, tm)
    pooled = _pool_prompt(u, lp["w_pool"], lp["pool_scale"], tm)
    attn = _moba_prompt(q, k, v)
    x1, h2 = _outproj(pooled.reshape(B * S, -1), attn.reshape(B * S, -1), x.reshape(B * S, D),
                      lp["w_out"], lp["g_mix_post"], lp["g_ffn_pre"], tm)

    def to_seq_major(t):
        H = t.shape[0]
        return t.reshape(H, n_steps, n_seq, HEAD_DIM).transpose(2, 0, 1, 3)

    pooled_s = _pool_sample(u_s[0], _seq_to_rows(pool_hist), lp["w_pool"], lp["pool_scale"],
                            n_steps, n_seq, past_len)
    sample = (to_seq_major(q_s[0]), to_seq_major(k_s[0]), to_seq_major(v_s[0]), pooled_s, u_s[0])
    return (x1, h2, k, v, u), sample


def _rows_to_seq(t, n_steps, n_seq):
    return t.reshape(n_steps, n_seq, t.shape[-1]).transpose(1, 0, 2)


def _seq_to_rows(t):
    return t.transpose(1, 0, 2).reshape(-1, t.shape[-1])


def _layer(xp, xs_tm, pool_hist, conv_hist, cache_k, cache_v, page_table, lp, n_seq, n_steps):
    B, S, D = xp.shape
    F = lp["w_gate"].shape[1]
    rows_s = xs_tm.shape[0]
    (x1p, h2p, kp, vp, up), (qs, ks, vs, pooled_s, us) = _mixer_inputs(xp, xs_tm, pool_hist, page_table, lp,
                                                                       n_seq, n_steps)
    tm = _pick_tile(S, 512)
    tn = _pick_tile(F, 512)
    fp, tail, attn_s = _ffn_up(h2p.reshape(B, S, D), lp["w_gate"], lp["w_up"], lp["conv_w"], lp["conv_b"],
                               qs, ks, vs, cache_k, cache_v, page_table, tm, tn, _pick_tile(tm, 256))
    attn_tm = attn_s.transpose(2, 0, 1, 3).reshape(rows_s, -1).astype(BF16)
    x1s, h2s = _outproj(pooled_s, attn_tm, xs_tm, lp["w_out"], lp["g_mix_post"], lp["g_ffn_pre"], rows_s)
    fs, gts = _ffn_up_sample(h2s, lp["w_gate"], lp["w_up"], lp["conv_w"], lp["conv_b"],
                             _seq_to_rows(conv_hist), n_steps, n_seq, tn)
    yp, ys = _ffn_down(fp.reshape(B * S, F), fs, lp["w_down"], x1p, x1s, lp["g_ffn_post"],
                       _pick_tile(B * S, 256))
    pool_p = up[:, S - POOL_HIST:, :]
    conv_p = tail[:, V7X_SUBLANES - (CONV_W - 1):, :]
    pool_s = jnp.concatenate([pool_hist, _rows_to_seq(us, n_steps, n_seq)], axis=1)[:, n_steps:]
    conv_s = jnp.concatenate([conv_hist, _rows_to_seq(gts, n_steps, n_seq)], axis=1)[:, n_steps:]
    return yp.reshape(B, S, D), ys, (kp, vp, ks, vs, pool_p, pool_s, conv_p, conv_s)


def kernel(x_prompt, x_sample, cache_k, cache_v, state_pool, state_conv, page_table,
           w_in, w_pool, pool_scale, w_out, g_mix_pre, g_mix_post,
           w_gate, w_up, conv_w, conv_b, w_down, g_ffn_pre, g_ffn_post):
    depth = w_in.shape[0]
    n_seq, n_steps, D = x_sample.shape
    assert w_in.shape[2] == 4 * pool_scale.shape[1], "pooling and attention widths must match"
    yp = x_prompt
    ys = x_sample.transpose(1, 0, 2).reshape(n_steps * n_seq, D)
    outs = [[] for _ in range(8)]
    for l in range(depth):
        lp = {"w_in": w_in[l], "w_pool": w_pool[l].astype(BF16), "pool_scale": pool_scale[l],
              "w_out": w_out[l].astype(BF16), "g_mix_pre": g_mix_pre[l], "g_mix_post": g_mix_post[l],
              "w_gate": w_gate[l], "w_up": w_up[l], "conv_w": conv_w[l],
              "conv_b": conv_b[l], "w_down": w_down[l], "g_ffn_pre": g_ffn_pre[l],
              "g_ffn_post": g_ffn_post[l]}
        yp, ys, states = _layer(yp, ys, state_pool[l], state_conv[l], cache_k[l], cache_v[l],
                                page_table, lp, n_seq, n_steps)
        for lst, val in zip(outs, states):
            lst.append(val)
    y_sample = ys.reshape(n_steps, n_seq, D).transpose(1, 0, 2)
    return (yp, y_sample) + tuple(jnp.stack(o) for o in outs)
```

```python
import functools

import jax
import jax.numpy as jnp
from jax import lax
from jax.experimental import pallas as pl
from jax.experimental.pallas import tpu as pltpu

F32 = jnp.float32
BF16 = jnp.bfloat16

POOL_WINDOWS = (2, 4, 8, 16)
POOL_HIST = max(POOL_WINDOWS) - 1
HEAD_DIM = 128
ROT_DIM = HEAD_DIM // 4
ROPE_THETA = 500000.0
MOBA_BLOCK = 256
MOBA_TOPK = 3
QUERY_BLOCKS_PER_DOT = 2
PAGE_SIZE = 128
CONV_W = 3
EPS = 1e-6
NEG_INF = float("-inf")
LOG2E = 1.4426950408889634

V7X_SUBLANES = 8
DMA_PRIORITIES = 2
V7X_VMEM_BYTES = 64 * 1024 * 1024
VMEM_RESERVE_BYTES = 4 * 1024 * 1024


def _vmem_limit(block_bytes):
    return int(min(2 * block_bytes, V7X_VMEM_BYTES - VMEM_RESERVE_BYTES))


def _params(semantics, block_bytes):
    return pltpu.CompilerParams(dimension_semantics=semantics,
                                vmem_limit_bytes=_vmem_limit(block_bytes))


def _resident_params(semantics, block_bytes):
    limit = int(min(block_bytes + VMEM_RESERVE_BYTES, V7X_VMEM_BYTES - VMEM_RESERVE_BYTES))
    return pltpu.CompilerParams(dimension_semantics=semantics, vmem_limit_bytes=limit)


def _rms(x, g):
    return x * lax.rsqrt(jnp.mean(x * x, axis=-1, keepdims=True) + EPS) * g


def _dot(a, b):
    return jnp.dot(a, b, preferred_element_type=F32)


def _dot_nt(a, b, precision=None):
    return lax.dot_general(a, b, (((1,), (1,)), ((), ())), precision=precision,
                           preferred_element_type=F32)


def _rope_tables(pos):
    half = ROT_DIM // 2
    inv = ROPE_THETA ** (-jnp.arange(half, dtype=F32) * (2.0 / ROT_DIM))
    ang = pos.astype(F32)[:, None] * inv[None, :]
    cos, sin = jnp.cos(ang), jnp.sin(ang)
    n = pos.shape[0]
    c = jnp.concatenate([cos, cos, jnp.ones((n, HEAD_DIM - ROT_DIM), F32)], axis=1)
    s_lo = jnp.concatenate([-sin, jnp.zeros((n, HEAD_DIM - half), F32)], axis=1)
    s_hi = jnp.concatenate([jnp.zeros((n, half), F32), sin,
                            jnp.zeros((n, HEAD_DIM - ROT_DIM), F32)], axis=1)
    return c, s_lo, s_hi


def _round_weight_once(w_hbm, wb_ref, stage_ref, sem, chunk):
    n_chunks = wb_ref.shape[0] // chunk

    def w_copy(c):
        return pltpu.make_async_copy(w_hbm.at[pl.ds(c * chunk, chunk), :], stage_ref.at[c % 2], sem.at[c % 2])

    w_copy(0).start()
    for c in range(n_chunks):
        if c + 1 < n_chunks:
            w_copy(c + 1).start(priority=(c + 1) % DMA_PRIORITIES)
        w_copy(c).wait()
        wb_ref[c * chunk:(c + 1) * chunk, :] = stage_ref[c % 2].astype(BF16)


def _inproj_kernel(x_ref, xs_ref, g_ref, w_hbm, c_ref, slo_ref, shi_ref, cs_ref, slos_ref, shis_ref,
                   u_ref, q_ref, k_ref, v_ref, us_ref, qs_ref, ks_ref, vs_ref,
                   wb_ref, hn_ref, stage_ref, sem, *, n_heads, width, n_tiles, chunk):
    i = pl.program_id(0)
    pair = 2 * HEAD_DIM

    @pl.when(i == 0)
    def _():
        _round_weight_once(w_hbm, wb_ref, stage_ref, sem, chunk)

    def project(x, tables, outs):
        c, s_lo, s_hi = (t[...] for t in tables)
        u_out, q_out, k_out, v_out = outs
        rows = x.shape[0]
        hn_ref[0:rows, :] = _rms(x, g_ref[...]).astype(BF16)

        def rope(h):
            return (h * c + pltpu.roll(h, HEAD_DIM - ROT_DIM // 2, 1) * s_lo
                    + pltpu.roll(h, ROT_DIM // 2, 1) * s_hi)

        for p in range(width // pair):
            u_out[0, :, p * pair:(p + 1) * pair] = _dot(hn_ref[0:rows, :], wb_ref[:, p * pair:(p + 1) * pair])
        for out_ref, part, rotary in ((q_out, 1, True), (k_out, 2, True), (v_out, 3, False)):
            for p in range(n_heads // 2):
                col = part * width + p * pair
                res = _dot(hn_ref[0:rows, :], wb_ref[:, col:col + pair])
                for half in range(2):
                    h = res[:, half * HEAD_DIM:(half + 1) * HEAD_DIM]
                    out_ref[0, 2 * p + half] = rope(h) if rotary else h

    @pl.when(i < n_tiles)
    def _():
        project(x_ref[0], (c_ref, slo_ref, shi_ref), (u_ref, q_ref, k_ref, v_ref))

    @pl.when(i == n_tiles)
    def _():
        project(xs_ref[0], (cs_ref, slos_ref, shis_ref), (us_ref, qs_ref, ks_ref, vs_ref))


def _inproj(x, x_s, g, w_in, pos, pos_s, tm):
    B, S, D = x.shape
    rows_s = x_s.shape[1]
    width = w_in.shape[1] // 4
    n_heads = width // HEAD_DIM
    assert n_heads % 2 == 0 and rows_s <= tm
    n_s = S // tm
    n_tiles = B * n_s
    chunk = _pick_tile(D, 256)

    def tile(i):
        ii = jnp.minimum(i, n_tiles - 1)
        return ii // n_s, lax.rem(ii, n_s)

    tab_spec = pl.BlockSpec((tm, HEAD_DIM), lambda i: (tile(i)[1], 0))
    hm_spec = pl.BlockSpec((1, n_heads, tm, HEAD_DIM), lambda i: (tile(i)[0], 0, tile(i)[1], 0))
    hm_shape = jax.ShapeDtypeStruct((B, n_heads, S, HEAD_DIM), F32)
    whole = lambda shape: pl.BlockSpec(shape, lambda i: (0,) * len(shape))
    stab_spec = whole((rows_s, HEAD_DIM))
    shm_spec = whole((1, n_heads, rows_s, HEAD_DIM))
    shm_shape = jax.ShapeDtypeStruct((1, n_heads, rows_s, HEAD_DIM), F32)
    block_bytes = (2 * tm * D * 4 + tm * D * 2 + D * 4 * width * 2 + 2 * chunk * 4 * width * 4
                   + 2 * 4 * tm * width * 4 + 2 * rows_s * (D + 4 * width) * 4)
    return pl.pallas_call(
        functools.partial(_inproj_kernel, n_heads=n_heads, width=width, n_tiles=n_tiles, chunk=chunk),
        grid=(n_tiles + 1,),
        in_specs=[
            pl.BlockSpec((1, tm, D), lambda i: (tile(i)[0], tile(i)[1], 0)),
            whole((1, rows_s, D)),
            whole((1, D)),
            pl.BlockSpec(memory_space=pl.ANY),
            tab_spec, tab_spec, tab_spec, stab_spec, stab_spec, stab_spec,
        ],
        out_specs=[pl.BlockSpec((1, tm, width), lambda i: (tile(i)[0], tile(i)[1], 0)), hm_spec, hm_spec, hm_spec,
                   whole((1, rows_s, width)), shm_spec, shm_spec, shm_spec],
        out_shape=[jax.ShapeDtypeStruct((B, S, width), F32), hm_shape, hm_shape, hm_shape,
                   jax.ShapeDtypeStruct((1, rows_s, width), F32), shm_shape, shm_shape, shm_shape],
        scratch_shapes=[pltpu.VMEM((D, 4 * width), BF16), pltpu.VMEM((tm, D), BF16),
                        pltpu.VMEM((2, chunk, 4 * width), F32), pltpu.SemaphoreType.DMA((2,))],
        compiler_params=_resident_params(("arbitrary",), block_bytes),
        name="inproj",
    )(x, x_s, g.reshape(1, D), w_in, *_rope_tables(pos), *_rope_tables(pos_s))


def _shift_rows(x, prev8, k):
    pad = V7X_SUBLANES
    rolled = pltpu.roll(x, k, 0)
    row8 = lax.broadcasted_iota(jnp.int32, prev8.shape, 0)
    head = jnp.where(row8 < k, pltpu.roll(prev8, k % pad, 0) if k % pad else prev8, rolled[0:pad])
    return jnp.concatenate([head, rolled[pad:]], axis=0)


def _pool_kernel(u_ref, w_ref, scale_ref, o_ref, halo_ref, *, ts, group):
    s = pl.program_id(1)
    pad = V7X_SUBLANES

    @pl.when(s == 0)
    def _():
        halo_ref[...] = jnp.zeros(halo_ref.shape, F32)

    pos = s * ts + lax.broadcasted_iota(jnp.int32, (ts, 1), 0)
    slot = 0
    for gi, w in enumerate(POOL_WINDOWS):
        cols = slice(gi * group, (gi + 1) * group)
        x = u_ref[0, :, cols]
        acc = x
        span = 1
        while span < w:
            prev8 = halo_ref[slot]
            halo_ref[slot] = acc[ts - pad:ts, :]
            acc = acc + _shift_rows(acc, prev8, span)
            slot += 1
            span *= 2
        cnt = jnp.minimum(pos + 1, w).astype(F32)
        d = (acc / cnt - x).astype(BF16)
        y = _dot(d, w_ref[gi]) * scale_ref[:, cols]
        o_ref[0, :, cols] = y.astype(BF16)


def _pool_prompt(u, w_pool_bf, scale, ts):
    B, S, C = u.shape
    n_groups, group, _ = w_pool_bf.shape
    assert all(w & (w - 1) == 0 and w <= 2 * V7X_SUBLANES for w in POOL_WINDOWS)
    n_levels = sum(w.bit_length() - 1 for w in POOL_WINDOWS)
    block_bytes = 2 * ts * C * 4 + 2 * n_groups * group * group * 2 + 2 * ts * C * 2 + 6 * ts * group * 4
    return pl.pallas_call(
        functools.partial(_pool_kernel, ts=ts, group=group),
        grid=(B, S // ts),
        in_specs=[
            pl.BlockSpec((1, ts, C), lambda b, s: (b, s, 0)),
            pl.BlockSpec((n_groups, group, group), lambda b, s: (0, 0, 0)),
            pl.BlockSpec((1, C), lambda b, s: (0, 0)),
        ],
        out_specs=pl.BlockSpec((1, ts, C), lambda b, s: (b, s, 0)),
        out_shape=jax.ShapeDtypeStruct((B, S, C), BF16),
        scratch_shapes=[pltpu.VMEM((n_levels, V7X_SUBLANES, group), F32)],
        compiler_params=_params(("arbitrary", "arbitrary"), block_bytes),
        name="pool_prompt",
    )(u, w_pool_bf, scale.reshape(1, C))


def _pool_sample_kernel(u_ref, hist_ref, w_ref, scale_ref, o_ref, *, n_steps, n_seq, group, past_len):
    def ext(i):
        if i < POOL_HIST:
            return hist_ref[i * n_seq:(i + 1) * n_seq, :]
        return u_ref[(i - POOL_HIST) * n_seq:(i - POOL_HIST + 1) * n_seq, :]

    for gi, w in enumerate(POOL_WINDOWS):
        cols = slice(gi * group, (gi + 1) * group)
        ds = []
        for t in range(n_steps):
            cur = ext(POOL_HIST + t)[:, cols]
            acc = cur
            for back in range(1, w):
                acc = acc + ext(POOL_HIST + t - back)[:, cols]
            cnt = float(min(past_len + t + 1, w))
            ds.append((acc / cnt - cur).astype(BF16))
        d = jnp.concatenate(ds, axis=0)
        y = _dot(d, w_ref[gi]) * scale_ref[:, cols]
        o_ref[:, cols] = y.astype(BF16)


def _pool_sample(u_tm, hist_tm, w_pool_bf, scale, n_steps, n_seq, past_len):
    rows, C = u_tm.shape
    n_groups, group, _ = w_pool_bf.shape
    block_bytes = 2 * (rows + hist_tm.shape[0]) * C * 4 + 2 * n_groups * group * group * 2 + 2 * rows * C * 2
    return pl.pallas_call(
        functools.partial(_pool_sample_kernel, n_steps=n_steps, n_seq=n_seq, group=group,
                          past_len=past_len),
        out_shape=jax.ShapeDtypeStruct((rows, C), BF16),
        compiler_params=pltpu.CompilerParams(vmem_limit_bytes=_vmem_limit(block_bytes)),
        name="pool_sample",
    )(u_tm, hist_tm, w_pool_bf, scale.reshape(1, C))


class _PerHead:
    def __init__(self, refs):
        self.refs = refs

    def __getitem__(self, idx):
        return self.refs[idx[0]][idx[1:]] if isinstance(idx, tuple) else self.refs[idx][...]

    def __setitem__(self, idx, value):
        self.refs[idx[0]][idx[1:]] = value


def _moba_prompt_kernel(q_ref, k_ref, v_ref, o_ref, *scratch, seq, hp):
    blk = MOBA_BLOCK
    nb = seq // blk
    n_sel = min(MOBA_TOPK, nb - 1)
    scale = HEAD_DIM ** -0.5
    kb_ref, vt_ref, km_ref, s_ref, p_ref = (_PerHead(scratch[kind * hp:(kind + 1) * hp]) for kind in range(5))

    for hh in range(hp):
        for n in range(nb):
            rows = slice(n * blk, (n + 1) * blk)
            kn = k_ref[0, hh, rows, :]
            kb_ref[hh, rows, :] = kn.astype(BF16)
            km_ref[hh, n:n + 1, :] = jnp.sum(kn, axis=0, keepdims=True) * (1.0 / blk)
            vt_ref[hh, :, rows] = v_ref[0, hh, rows, :].T.astype(BF16)

    blk_id = lax.broadcasted_iota(jnp.int32, (nb, blk), 0)
    key_i = lax.broadcasted_iota(jnp.int32, (blk, blk), 0)
    qry_i = lax.broadcasted_iota(jnp.int32, (blk, blk), 1)
    causal_bias = jnp.where(key_i <= qry_i, 0.0, NEG_INF).astype(F32)

    def scores(hh, qblocks):
        first, last = qblocks[0], qblocks[-1]
        nk = (last + 1) * blk
        width = len(qblocks) * blk
        qg = q_ref[0, hh, first * blk:nk, :]
        s_ref[hh, 0:nk, 0:width] = _dot_nt(kb_ref[hh, 0:nk, :], (qg * (scale * LOG2E)).astype(BF16))
        return _dot_nt(km_ref[hh], qg, precision=lax.Precision.HIGHEST) if last > 0 else None

    def softmax(hh, qblocks, gate):
        last = qblocks[-1]
        sums = []
        for a, i in enumerate(qblocks):
            cols = slice(a * blk, (a + 1) * blk)
            own = slice(i * blk, (i + 1) * blk)
            biases = []
            if i > 0:
                gate_a = gate[:, cols]
                valid = jnp.where(blk_id < i, 1.0, 0.0)
                for n in range(i):
                    gn = gate_a[n:n + 1, :]
                    beats = jnp.where(blk_id < n, jnp.where(gate_a >= gn, 1.0, 0.0),
                                      jnp.where(gate_a > gn, 1.0, 0.0))
                    rank = jnp.sum(beats * valid, axis=0, keepdims=True)
                    biases.append(jnp.where(rank < n_sel, 0.0, NEG_INF).astype(F32))

            s_own = s_ref[hh, own, cols] + causal_bias
            m = jnp.max(s_own, axis=0, keepdims=True)
            for n in range(i):
                m = jnp.maximum(m, jnp.max(s_ref[hh, n * blk:(n + 1) * blk, cols], axis=0, keepdims=True)
                                + biases[n])

            p = jnp.exp2(s_own - m)
            l = jnp.sum(p, axis=0, keepdims=True)
            p_ref[hh, own, cols] = p.astype(BF16)
            for n in range(i):
                rows = slice(n * blk, (n + 1) * blk)
                p = jnp.exp2(s_ref[hh, rows, cols] + (biases[n] - m))
                l = l + jnp.sum(p, axis=0, keepdims=True)
                p_ref[hh, rows, cols] = p.astype(BF16)
            for n in range(i + 1, last + 1):
                p_ref[hh, n * blk:(n + 1) * blk, cols] = jnp.zeros((blk, blk), BF16)
            sums.append(l)
        return sums

    def output(hh, qblocks, sums):
        nk = (qblocks[-1] + 1) * blk
        width = len(qblocks) * blk
        o_t = _dot(vt_ref[hh, :, 0:nk], p_ref[hh, 0:nk, 0:width])
        for a, i in enumerate(qblocks):
            o_a = o_t[:, a * blk:(a + 1) * blk] / sums[a]
            o_ref[0, i * blk:(i + 1) * blk, hh * HEAD_DIM:(hh + 1) * HEAD_DIM] = o_a.T.astype(BF16)

    items = [(hh, list(range(first, min(first + QUERY_BLOCKS_PER_DOT, nb))))
             for first in range(0, nb, QUERY_BLOCKS_PER_DOT) for hh in range(hp)]
    gate = scores(*items[0])
    for cur, nxt in zip(items, items[1:] + [None]):
        next_gate = scores(*nxt) if nxt is not None else None
        output(*cur, softmax(*cur, gate))
        gate = next_gate


def _moba_prompt(q, k, v):
    B, H, S, Dh = q.shape
    nb = S // MOBA_BLOCK
    hp = 2 if H % 2 == 0 else 1
    width = QUERY_BLOCKS_PER_DOT * MOBA_BLOCK
    in_spec = pl.BlockSpec((1, hp, S, Dh), lambda b, g: (b, g, 0, 0))
    block_bytes = hp * (2 * 3 * S * Dh * 4 + 2 * S * Dh * 2 + 2 * S * Dh * 2 + S * width * 6)
    return pl.pallas_call(
        functools.partial(_moba_prompt_kernel, seq=S, hp=hp),
        grid=(B, H // hp),
        in_specs=[in_spec, in_spec, in_spec],
        out_specs=pl.BlockSpec((1, S, hp * Dh), lambda b, g: (b, 0, g)),
        out_shape=jax.ShapeDtypeStruct((B, S, H * Dh), BF16),
        scratch_shapes=(
            [pltpu.VMEM((S, Dh), BF16)] * hp
            + [pltpu.VMEM((Dh, S), BF16)] * hp
            + [pltpu.VMEM((nb, Dh), F32)] * hp
            + [pltpu.VMEM((S, width), F32)] * hp
            + [pltpu.VMEM((S, width), BF16)] * hp
        ),
        compiler_params=_params(("arbitrary", "arbitrary"), block_bytes),
        name="moba_prompt",
    )(q, k, v)


class _PagedAttention:
    def __init__(self, pt_ref, qc_ref, qp_ref, kn_ref, vn_ref, ck_ref, cv_ref, o_ref,
                 kbuf, ksel, vbuf, km_ref, q8_ref, sel_ref, ksem, vsem, *, n_pages, n_groups, hp, n_new):
        self.__dict__.update(locals())
        self.ppb = MOBA_BLOCK // PAGE_SIZE
        self.nblk = n_pages // self.ppb
        self.n_sel = min(MOBA_TOPK, self.nblk)

    def unit_bh(self, u):
        return u // self.n_groups, lax.rem(u, self.n_groups) * self.hp

    def k_copy(self, u, page):
        bb, h0 = self.unit_bh(u)
        sl = lax.rem(u, 2)
        return pltpu.make_async_copy(self.ck_ref.at[self.pt_ref[bb, page], pl.ds(h0, self.hp)],
                                     self.kbuf.at[sl, :, pl.ds(page * PAGE_SIZE, PAGE_SIZE), :],
                                     self.ksem.at[sl])

    def start_k(self, u):
        def body(pair, carry):
            for lane in range(DMA_PRIORITIES):
                self.k_copy(u, pair * DMA_PRIORITIES + lane).start(priority=lane)
            return carry
        assert self.n_pages % DMA_PRIORITIES == 0
        lax.fori_loop(0, self.n_pages // DMA_PRIORITIES, body, 0, unroll=4)

    def wait_k(self, u):
        for page in range(self.n_pages):
            self.k_copy(u, page).wait()

    def sel_index(self, hh, j, c):
        return (hh * self.n_new + j) * self.n_sel + c

    def v_copies(self, u, heads):
        bb, h0 = self.unit_bh(u)
        sl = lax.rem(u, 2)
        for hh in heads:
            for j in range(self.n_new):
                for c in range(self.n_sel):
                    blk = self.sel_ref[sl, self.sel_index(hh, j, c)]
                    for pg in range(self.ppb):
                        page = self.pt_ref[bb, blk * self.ppb + pg]
                        dst = ((j * self.n_sel + c) * self.ppb + pg) * PAGE_SIZE
                        yield pltpu.make_async_copy(self.cv_ref.at[page, h0 + hh],
                                                    self.vbuf.at[sl, hh, pl.ds(dst, PAGE_SIZE), :],
                                                    self.vsem.at[sl])

    def start_v(self, u, heads):
        for i, cp in enumerate(self.v_copies(u, heads)):
            cp.start(priority=i % DMA_PRIORITIES)

    def wait_v(self, u):
        for cp in self.v_copies(u, range(self.hp)):
            cp.wait()

    def block_means(self, u, heads):
        sl = lax.rem(u, 2)
        for hh in heads:
            for n in range(self.nblk):
                kn_blk = self.kbuf[sl, hh, n * MOBA_BLOCK:(n + 1) * MOBA_BLOCK, :]
                self.km_ref[hh, n:n + 1, :] = jnp.sum(kn_blk, axis=0, keepdims=True) * (1.0 / MOBA_BLOCK)

    def select(self, u, heads):
        sl = lax.rem(u, 2)
        for hh in heads:
            self.q8_ref[hh] = jnp.zeros(self.q8_ref.shape[1:], F32)
            self.q8_ref[hh, 0:self.n_new, :] = self.qc_ref[0, hh]
            gate = _dot_nt(self.q8_ref[hh], self.km_ref[hh], precision=lax.Precision.HIGHEST)
            lane = lax.broadcasted_iota(jnp.int32, gate.shape, 1)
            picks = [jnp.zeros((gate.shape[0], 1), F32) for _ in range(self.n_sel)]
            for n in range(self.nblk):
                gn = gate[:, n:n + 1]
                beats = jnp.where(lane < n, jnp.where(gate >= gn, 1.0, 0.0), jnp.where(gate > gn, 1.0, 0.0))
                rank = jnp.sum(beats, axis=1, keepdims=True)
                for c in range(self.n_sel):
                    picks[c] = jnp.where(rank == float(c), float(n), picks[c])
            for j in range(self.n_new):
                for c in range(self.n_sel):
                    blk = picks[c][j, 0].astype(jnp.int32)
                    self.sel_ref[sl, self.sel_index(hh, j, c)] = blk
                    row0 = (j * self.n_sel + c) * MOBA_BLOCK
                    start = pl.multiple_of(blk * MOBA_BLOCK, MOBA_BLOCK)
                    self.ksel[sl, hh, row0:row0 + MOBA_BLOCK, :] = self.kbuf[sl, hh, pl.ds(start, MOBA_BLOCK), :]

    def attend(self, u, heads):
        sl = lax.rem(u, 2)
        scale = HEAD_DIM ** -0.5
        new_row = lax.broadcasted_iota(jnp.int32, (self.n_new, 1), 0)
        for hh in heads:
            q = self.qp_ref[0, hh]
            kn = self.kn_ref[0, hh]
            vn = self.vn_ref[0, hh]
            for j in range(self.n_new):
                qj = q[j:j + 1, :] * (scale * LOG2E)
                rows = [slice((j * self.n_sel + c) * MOBA_BLOCK, (j * self.n_sel + c + 1) * MOBA_BLOCK)
                        for c in range(self.n_sel)]
                s_sel = [jnp.sum(self.ksel[sl, hh, r, :] * qj, axis=1, keepdims=True) for r in rows]
                s_new = jnp.sum(kn * qj, axis=1, keepdims=True)
                s_new = jnp.where(new_row <= j, s_new, NEG_INF)
                m = jnp.max(s_new, axis=0, keepdims=True)
                for s in s_sel:
                    m = jnp.maximum(m, jnp.max(s, axis=0, keepdims=True))
                p_new = jnp.exp2(s_new - m)
                l = jnp.sum(p_new, axis=0, keepdims=True)
                acc = jnp.sum(p_new * vn, axis=0, keepdims=True)
                for s, r in zip(s_sel, rows):
                    p = jnp.exp2(s - m)
                    l = l + jnp.sum(p, axis=0, keepdims=True)
                    acc = acc + jnp.sum(p * self.vbuf[sl, hh, r, :], axis=0, keepdims=True)
                self.o_ref[0, hh, j:j + 1, :] = acc / l


def _paged_attention_scratch(n_pages, hp, n_new):
    nblk = n_pages // (MOBA_BLOCK // PAGE_SIZE)
    n_sel = min(MOBA_TOPK, nblk)
    picked_rows = n_new * n_sel * MOBA_BLOCK
    shapes = [
        pltpu.VMEM((2, hp, n_pages * PAGE_SIZE, HEAD_DIM), F32),
        pltpu.VMEM((2, hp, picked_rows, HEAD_DIM), F32),
        pltpu.VMEM((2, hp, picked_rows, HEAD_DIM), F32),
        pltpu.VMEM((hp, nblk, HEAD_DIM), F32),
        pltpu.VMEM((hp, V7X_SUBLANES, HEAD_DIM), F32),
        pltpu.SMEM((2, hp * n_new * n_sel), jnp.int32),
        pltpu.SemaphoreType.DMA((2,)),
        pltpu.SemaphoreType.DMA((2,)),
    ]
    n_bytes = 4 * HEAD_DIM * 2 * hp * (n_pages * PAGE_SIZE + 2 * picked_rows)
    return shapes, n_bytes


def _outproj_kernel(pooled_ref, attn_ref, x_ref, w_ref, gpost_ref, gffn_ref, x1_ref, h2_ref, *, split, sub):
    for r in range(x_ref.shape[0] // sub):
        rows = slice(r * sub, (r + 1) * sub)
        mix = _dot(pooled_ref[rows, :], w_ref[0:split, :]) + _dot(attn_ref[rows, :], w_ref[split:, :])
        x1 = x_ref[rows, :] + _rms(mix, gpost_ref[...])
        x1_ref[rows, :] = x1
        h2_ref[rows, :] = _rms(x1, gffn_ref[...]).astype(BF16)


def _outproj(pooled, attn, x, w_bf, g_post, g_ffn, tm):
    rows, D = x.shape
    split = pooled.shape[1]
    wa = attn.shape[1]
    row_spec = lambda width: pl.BlockSpec((tm, width), lambda i: (i, 0))
    vec_spec = pl.BlockSpec((1, D), lambda i: (0, 0))
    block_bytes = 2 * tm * (split + wa) * 2 + 2 * tm * D * 4 + 2 * (split + wa) * D * 2 \
        + 2 * tm * D * 6 + 2 * tm * D * 4
    return pl.pallas_call(
        functools.partial(_outproj_kernel, split=split, sub=_pick_tile(tm, 128)),
        grid=(rows // tm,),
        in_specs=[row_spec(split), row_spec(wa), row_spec(D),
                  pl.BlockSpec((split + wa, D), lambda i: (0, 0)), vec_spec, vec_spec],
        out_specs=[row_spec(D), row_spec(D)],
        out_shape=[jax.ShapeDtypeStruct((rows, D), F32), jax.ShapeDtypeStruct((rows, D), BF16)],
        compiler_params=_params(("arbitrary",), block_bytes),
        name="outproj",
    )(pooled, attn, x, w_bf, g_post.reshape(1, D), g_ffn.reshape(1, D))


def _gelu_tanh(c):
    return c * (0.5 * (1.0 + jnp.tanh(0.7978845608028654 * (c + 0.044715 * (c * c * c)))))


def _conv_gelu_gate(gt, prev8, up, cw_ref, cb_ref):
    c = cb_ref[...]
    for i in range(CONV_W):
        back = CONV_W - 1 - i
        c = c + (_shift_rows(gt, prev8, back) if back else gt) * cw_ref[i:i + 1, :]
    return _gelu_tanh(c) * up


def _ffn_up_kernel(pt_ref, h_ref, wg_ref, wu_ref, cw_ref, cb_ref, qc_ref, qp_ref, kn_ref, vn_ref, ck_ref, cv_ref,
                   f_ref, tail_ref, o_ref, wgb_ref, wub_ref, halo_ref, *attn_scratch,
                   tm, sub, n_b, n_s, n_units, attn_params):
    j, b, s = pl.program_id(0), pl.program_id(1), pl.program_id(2)
    t = (j * n_b + b) * n_s + s
    pad = V7X_SUBLANES
    last_sub = tm // sub - 1
    attn = _PagedAttention(pt_ref, qc_ref, qp_ref, kn_ref, vn_ref, ck_ref, cv_ref, o_ref,
                           *attn_scratch, **attn_params)

    @pl.when((b == 0) & (s == 0))
    def _():
        wgb_ref[...] = wg_ref[...].astype(BF16)
        wub_ref[...] = wu_ref[...].astype(BF16)

    @pl.when(s == 0)
    def _():
        halo_ref[...] = jnp.zeros(halo_ref.shape, F32)

    @pl.when(t == 0)
    def _():
        attn.start_k(t)

    @pl.when(t < n_units)
    def _():
        attn.wait_k(t)

    @pl.when(t + 1 < n_units)
    def _():
        attn.start_k(t + 1)

    def gate_up(before=None, middle=None, end=()):
        before, middle = before or {}, middle or {}
        prev8 = halo_ref[...]
        half = sub // 2
        for r in range(tm // sub):
            for stage in before.get(r, ()):
                stage()
            row0 = r * sub
            if r in middle:
                lo = h_ref[0, row0:row0 + half, :]
                gt_lo, up_lo = _dot(lo, wgb_ref[...]), _dot(lo, wub_ref[...])
                for stage in middle[r]:
                    stage()
                hi = h_ref[0, row0 + half:row0 + sub, :]
                gt = jnp.concatenate([gt_lo, _dot(hi, wgb_ref[...])], axis=0)
                up = jnp.concatenate([up_lo, _dot(hi, wub_ref[...])], axis=0)
            else:
                hr = h_ref[0, row0:row0 + sub, :]
                gt, up = _dot(hr, wgb_ref[...]), _dot(hr, wub_ref[...])
            f_ref[0, row0:row0 + sub, :] = _conv_gelu_gate(gt, prev8, up, cw_ref, cb_ref).astype(BF16)
            prev8 = gt[sub - pad:sub, :]
        halo_ref[...] = prev8
        tail_ref[0] = prev8
        for stage in end:
            stage()

    heads = tuple(range(attn.hp))

    def means():
        attn.block_means(t, heads)

    def select():
        attn.select(t, heads)

    def attend_previous():
        attn.wait_v(t - 1)
        attn.attend(t - 1, heads)

    def fetch_values():
        attn.start_v(t, heads)

    @pl.when(t == 0)
    def _():
        gate_up(before={0: [means]}, middle={0: [select]}, end=[fetch_values])

    @pl.when((t >= 1) & (t < n_units))
    def _():
        before = {0: [means]}
        before.setdefault(last_sub, []).append(attend_previous)
        gate_up(before=before, middle={0: [select]}, end=[fetch_values])

    @pl.when(t == n_units)
    def _():
        gate_up(before={last_sub: [attend_previous]})

    @pl.when(t > n_units)
    def _():
        gate_up()


def _ffn_up(h2, w_gate, w_up, conv_w, conv_b, q, k_new, v_new, cache_k, cache_v, page_table, tm, tn, sub):
    B, S, D = h2.shape
    F = w_gate.shape[1]
    Bd, H, L, Dh = q.shape
    n_pages = page_table.shape[1]
    ppb = MOBA_BLOCK // PAGE_SIZE
    assert n_pages % ppb == 0, "past length must be a whole number of MoBA blocks"
    assert n_pages >= ppb and L <= V7X_SUBLANES
    hp = 2 if H % 2 == 0 else 1
    n_groups = H // hp
    n_units = Bd * n_groups
    n_b, n_s = B, S // tm
    assert (F // tn) * n_b * n_s > n_units >= 2, "not enough grid steps to host the sample attention"
    pad = V7X_SUBLANES

    def unit_spec(lag):
        def index_map(j, b, s, pt):
            u = jnp.clip((j * n_b + b) * n_s + s - lag, 0, n_units - 1)
            return (u // n_groups, lax.rem(u, n_groups), 0, 0)
        return pl.BlockSpec((1, hp, L, Dh), index_map)

    w_spec = pl.BlockSpec((D, tn), lambda j, b, s, pt: (0, j))
    any_spec = pl.BlockSpec(memory_space=pl.ANY)
    attn_scratch, attn_bytes = _paged_attention_scratch(n_pages, hp, L)
    block_bytes = (2 * tm * D * 2 + 2 * 2 * D * tn * 4 + 2 * D * tn * 2 + 2 * tm * tn * 2 + 8 * sub * tn * 4
                   + attn_bytes)
    grid_spec = pltpu.PrefetchScalarGridSpec(
        num_scalar_prefetch=1,
        grid=(F // tn, n_b, n_s),
        in_specs=[
            pl.BlockSpec((1, tm, D), lambda j, b, s, pt: (b, s, 0)),
            w_spec, w_spec,
            pl.BlockSpec((CONV_W, tn), lambda j, b, s, pt: (0, j)),
            pl.BlockSpec((1, tn), lambda j, b, s, pt: (0, j)),
            unit_spec(0), unit_spec(1), unit_spec(1), unit_spec(1), any_spec, any_spec,
        ],
        out_specs=[pl.BlockSpec((1, tm, tn), lambda j, b, s, pt: (b, s, j)),
                   pl.BlockSpec((1, pad, tn), lambda j, b, s, pt: (b, 0, j)),
                   unit_spec(1)],
        scratch_shapes=[pltpu.VMEM((D, tn), BF16), pltpu.VMEM((D, tn), BF16), pltpu.VMEM((pad, tn), F32)]
        + attn_scratch,
    )
    return pl.pallas_call(
        functools.partial(_ffn_up_kernel, tm=tm, sub=sub, n_b=n_b, n_s=n_s, n_units=n_units,
                          attn_params=dict(n_pages=n_pages, n_groups=n_groups, hp=hp, n_new=L)),
        grid_spec=grid_spec,
        out_shape=[jax.ShapeDtypeStruct((B, S, F), BF16), jax.ShapeDtypeStruct((B, pad, F), F32),
                   jax.ShapeDtypeStruct((Bd, H, L, Dh), F32)],
        compiler_params=_resident_params(("arbitrary", "arbitrary", "arbitrary"), block_bytes),
        name="ffn_up",
    )(page_table, h2, w_gate, w_up, conv_w, conv_b.reshape(1, F), q, q, k_new, v_new, cache_k, cache_v)


def _ffn_up_sample_kernel(h_ref, wg_ref, wu_ref, cw_ref, cb_ref, hist_ref, f_ref, gt_ref, *, n_steps, n_seq):
    h = h_ref[...]
    gt = _dot(h, wg_ref[...].astype(BF16))
    up = _dot(h, wu_ref[...].astype(BF16))
    gt_ref[...] = gt
    ext = [hist_ref[i * n_seq:(i + 1) * n_seq, :] for i in range(CONV_W - 1)]
    ext += [gt[t * n_seq:(t + 1) * n_seq, :] for t in range(n_steps)]
    for t in range(n_steps):
        c = cb_ref[...]
        for i in range(CONV_W):
            c = c + ext[t + i] * cw_ref[i:i + 1, :]
        rows = slice(t * n_seq, (t + 1) * n_seq)
        f_ref[rows, :] = (_gelu_tanh(c) * up[rows, :]).astype(BF16)


def _ffn_up_sample(h2, w_gate, w_up, conv_w, conv_b, hist_tm, n_steps, n_seq, tn):
    rows, D = h2.shape
    F = w_gate.shape[1]
    col_spec = lambda r: pl.BlockSpec((r, tn), lambda j: (0, j))
    block_bytes = 2 * rows * D * 2 + 2 * 2 * D * tn * 4 + 2 * D * tn * 2 + 2 * rows * tn * 6 \
        + 2 * hist_tm.shape[0] * tn * 4 + 4 * rows * tn * 4
    return pl.pallas_call(
        functools.partial(_ffn_up_sample_kernel, n_steps=n_steps, n_seq=n_seq),
        grid=(F // tn,),
        in_specs=[pl.BlockSpec((rows, D), lambda j: (0, 0)), col_spec(D), col_spec(D),
                  col_spec(CONV_W), col_spec(1), col_spec(hist_tm.shape[0])],
        out_specs=[col_spec(rows), col_spec(rows)],
        out_shape=[jax.ShapeDtypeStruct((rows, F), BF16), jax.ShapeDtypeStruct((rows, F), F32)],
        compiler_params=_params(("arbitrary",), block_bytes),
        name="ffn_up_sample",
    )(h2, w_gate, w_up, conv_w, conv_b.reshape(1, F), hist_tm)


def _ffn_down_kernel(f_ref, fs_ref, w_hbm, x1_ref, x1s_ref, g_ref, y_ref, ys_ref, wb_ref, stage_ref, sem,
                     *, n_tiles, chunk):
    i = pl.program_id(0)

    @pl.when(i == 0)
    def _():
        _round_weight_once(w_hbm, wb_ref, stage_ref, sem, chunk)

    @pl.when(i < n_tiles)
    def _():
        y_ref[...] = x1_ref[...] + _rms(_dot(f_ref[...], wb_ref[...]), g_ref[...])

    @pl.when(i == n_tiles)
    def _():
        ys_ref[...] = x1s_ref[...] + _rms(_dot(fs_ref[...], wb_ref[...]), g_ref[...])


def _ffn_down(f, f_s, w_down, x1, x1_s, g, tm):
    rows, F = f.shape
    rows_s = f_s.shape[0]
    D = w_down.shape[1]
    n_tiles = rows // tm
    chunk = _pick_tile(F, 512)
    while (F // chunk) % 2 == 0 and chunk * D * 4 > 4 * 1024 * 1024:
        chunk //= 2
    assert chunk % 16 == 0
    tile = lambda r: pl.BlockSpec((tm, r), lambda i: (jnp.minimum(i, n_tiles - 1), 0))
    whole = lambda r, c: pl.BlockSpec((r, c), lambda i: (0, 0))
    block_bytes = (2 * tm * F * 2 + 2 * rows_s * F * 2 + F * D * 2 + 2 * chunk * D * 4 + 4 * tm * D * 4
                   + 4 * rows_s * D * 4 + 2 * tm * D * 4)
    return pl.pallas_call(
        functools.partial(_ffn_down_kernel, n_tiles=n_tiles, chunk=chunk),
        grid=(n_tiles + 1,),
        in_specs=[tile(F), whole(rows_s, F), pl.BlockSpec(memory_space=pl.ANY), tile(D), whole(rows_s, D),
                  whole(1, D)],
        out_specs=[tile(D), whole(rows_s, D)],
        out_shape=[jax.ShapeDtypeStruct((rows, D), F32), jax.ShapeDtypeStruct((rows_s, D), F32)],
        scratch_shapes=[pltpu.VMEM((F, D), BF16), pltpu.VMEM((2, chunk, D), F32), pltpu.SemaphoreType.DMA((2,))],
        compiler_params=_resident_params(("arbitrary",), block_bytes),
        name="ffn_down",
    )(f, f_s, w_down, x1, x1_s, g.reshape(1, D))


def _pick_tile(n, target):
    t = min(n, target)
    while n % t:
        t //= 2
    return t


def _mixer_inputs(x, x_tm, pool_hist, page_table, lp, n_seq, n_steps):
    B, S, D = x.shape
    tm = _pick_tile(S, 512)
    past_len = page_table.shape[1] * PAGE_SIZE
    pos_s = past_len + jnp.repeat(jnp.arange(n_steps), n_seq)
    u, q, k, v, u_s, q_s, k_s, v_s = _inproj(x, x_tm[None], lp["g_mix_pre"], lp["w_in"],
                                             jnp.arange(S), pos_s, tm)
    pooled = _pool_prompt(u, lp["w_pool"], lp["pool_scale"], tm)
    attn = _moba_prompt(q, k, v)
    x1, h2 = _outproj(pooled.reshape(B * S, -1), attn.reshape(B * S, -1), x.reshape(B * S, D),
                      lp["w_out"], lp["g_mix_post"], lp["g_ffn_pre"], tm)

    def to_seq_major(t):
        H = t.shape[0]
        return t.reshape(H, n_steps, n_seq, HEAD_DIM).transpose(2, 0, 1, 3)

    pooled_s = _pool_sample(u_s[0], _seq_to_rows(pool_hist), lp["w_pool"], lp["pool_scale"],
                            n_steps, n_seq, past_len)
    sample = (to_seq_major(q_s[0]), to_seq_major(k_s[0]), to_seq_major(v_s[0]), pooled_s, u_s[0])
    return (x1, h2, k, v, u), sample


def _rows_to_seq(t, n_steps, n_seq):
    return t.reshape(n_steps, n_seq, t.shape[-1]).transpose(1, 0, 2)


def _seq_to_rows(t):
    return t.transpose(1, 0, 2).reshape(-1, t.shape[-1])


def _layer(xp, xs_tm, pool_hist, conv_hist, cache_k, cache_v, page_table, lp, n_seq, n_steps):
    B, S, D = xp.shape
    F = lp["w_gate"].shape[1]
    rows_s = xs_tm.shape[0]
    (x1p, h2p, kp, vp, up), (qs, ks, vs, pooled_s, us) = _mixer_inputs(xp, xs_tm, pool_hist, page_table, lp,
                                                                       n_seq, n_steps)
    tm = _pick_tile(S, 512)
    tn = _pick_tile(F, 512)
    fp, tail, attn_s = _ffn_up(h2p.reshape(B, S, D), lp["w_gate"], lp["w_up"], lp["conv_w"], lp["conv_b"],
                               qs, ks, vs, cache_k, cache_v, page_table, tm, tn, _pick_tile(tm, 256))
    attn_tm = attn_s.transpose(2, 0, 1, 3).reshape(rows_s, -1).astype(BF16)
    x1s, h2s = _outproj(pooled_s, attn_tm, xs_tm, lp["w_out"], lp["g_mix_post"], lp["g_ffn_pre"], rows_s)
    fs, gts = _ffn_up_sample(h2s, lp["w_gate"], lp["w_up"], lp["conv_w"], lp["conv_b"],
                             _seq_to_rows(conv_hist), n_steps, n_seq, tn)
    yp, ys = _ffn_down(fp.reshape(B * S, F), fs, lp["w_down"], x1p, x1s, lp["g_ffn_post"],
                       _pick_tile(B * S, 256))
    pool_p = up[:, S - POOL_HIST:, :]
    conv_p = tail[:, V7X_SUBLANES - (CONV_W - 1):, :]
    pool_s = jnp.concatenate([pool_hist, _rows_to_seq(us, n_steps, n_seq)], axis=1)[:, n_steps:]
    conv_s = jnp.concatenate([conv_hist, _rows_to_seq(gts, n_steps, n_seq)], axis=1)[:, n_steps:]
    return yp.reshape(B, S, D), ys, (kp, vp, ks, vs, pool_p, pool_s, conv_p, conv_s)


def kernel(x_prompt, x_sample, cache_k, cache_v, state_pool, state_conv, page_table,
           w_in, w_pool, pool_scale, w_out, g_mix_pre, g_mix_post,
           w_gate, w_up, conv_w, conv_b, w_down, g_ffn_pre, g_ffn_post):
    depth = w_in.shape[0]
    n_seq, n_steps, D = x_sample.shape
    assert w_in.shape[2] == 4 * pool_scale.shape[1], "pooling and attention widths must match"
    yp = x_prompt
    ys = x_sample.transpose(1, 0, 2).reshape(n_steps * n_seq, D)
    outs = [[] for _ in range(8)]
    for l in range(depth):
        lp = {"w_in": w_in[l], "w_pool": w_pool[l].astype(BF16), "pool_scale": pool_scale[l],
              "w_out": w_out[l].astype(BF16), "g_mix_pre": g_mix_pre[l], "g_mix_post": g_mix_post[l],
              "w_gate": w_gate[l], "w_up": w_up[l], "conv_w": conv_w[l],
              "conv_b": conv_b[l], "w_down": w_down[l], "g_ffn_pre": g_ffn_pre[l],
              "g_ffn_post": g_ffn_post[l]}
        yp, ys, states = _layer(yp, ys, state_pool[l], state_conv[l], cache_k[l], cache_v[l],
                                page_table, lp, n_seq, n_steps)
        for lst, val in zip(outs, states):
            lst.append(val)
    y_sample = ys.reshape(n_steps, n_seq, D).transpose(1, 0, 2)
    return (yp, y_sample) + tuple(jnp.stack(o) for o in outs)
```

```python
import functools

import jax
import jax.numpy as jnp
from jax import lax
from jax.experimental import pallas as pl
from jax.experimental.pallas import tpu as pltpu

F32 = jnp.float32
BF16 = jnp.bfloat16

POOL_WINDOWS = (2, 4, 8, 16)
POOL_HIST = max(POOL_WINDOWS) - 1
HEAD_DIM = 128
ROT_DIM = HEAD_DIM // 4
ROPE_THETA = 500000.0
MOBA_BLOCK = 256
MOBA_TOPK = 3
QUERY_BLOCKS_PER_DOT = 2
PAGE_SIZE = 128
CONV_W = 3
EPS = 1e-6
NEG_INF = float("-inf")
LOG2E = 1.4426950408889634

V7X_SUBLANES = 8
V7X_VMEM_BYTES = 64 * 1024 * 1024
VMEM_RESERVE_BYTES = 4 * 1024 * 1024


def _vmem_limit(block_bytes):
    return int(min(2 * block_bytes, V7X_VMEM_BYTES - VMEM_RESERVE_BYTES))


def _params(semantics, block_bytes):
    return pltpu.CompilerParams(dimension_semantics=semantics,
                                vmem_limit_bytes=_vmem_limit(block_bytes))


def _resident_params(semantics, block_bytes):
    limit = int(min(block_bytes + VMEM_RESERVE_BYTES, V7X_VMEM_BYTES - VMEM_RESERVE_BYTES))
    return pltpu.CompilerParams(dimension_semantics=semantics, vmem_limit_bytes=limit)


def _rms(x, g):
    return x * lax.rsqrt(jnp.mean(x * x, axis=-1, keepdims=True) + EPS) * g


def _dot(a, b):
    return jnp.dot(a, b, preferred_element_type=F32)


def _dot_nt(a, b, precision=None):
    return lax.dot_general(a, b, (((1,), (1,)), ((), ())), precision=precision,
                           preferred_element_type=F32)


def _rope_tables(pos):
    half = ROT_DIM // 2
    inv = ROPE_THETA ** (-jnp.arange(half, dtype=F32) * (2.0 / ROT_DIM))
    ang = pos.astype(F32)[:, None] * inv[None, :]
    cos, sin = jnp.cos(ang), jnp.sin(ang)
    n = pos.shape[0]
    c = jnp.concatenate([cos, cos, jnp.ones((n, HEAD_DIM - ROT_DIM), F32)], axis=1)
    s_lo = jnp.concatenate([-sin, jnp.zeros((n, HEAD_DIM - half), F32)], axis=1)
    s_hi = jnp.concatenate([jnp.zeros((n, half), F32), sin,
                            jnp.zeros((n, HEAD_DIM - ROT_DIM), F32)], axis=1)
    return c, s_lo, s_hi


def _round_weight_once(w_hbm, wb_ref, stage_ref, sem, chunk):
    n_chunks = wb_ref.shape[0] // chunk

    def w_copy(c):
        return pltpu.make_async_copy(w_hbm.at[pl.ds(c * chunk, chunk), :], stage_ref.at[c % 2], sem.at[c % 2])

    w_copy(0).start()
    for c in range(n_chunks):
        if c + 1 < n_chunks:
            w_copy(c + 1).start()
        w_copy(c).wait()
        wb_ref[c * chunk:(c + 1) * chunk, :] = stage_ref[c % 2].astype(BF16)


def _inproj_kernel(x_ref, xs_ref, g_ref, w_hbm, c_ref, slo_ref, shi_ref, cs_ref, slos_ref, shis_ref,
                   wp_ref, scale_ref,
                   u_ref, q_ref, k_ref, v_ref, pooled_ref, us_ref, qs_ref, ks_ref, vs_ref,
                   wb_ref, hn_ref, stage_ref, sem, halo_ref, *, n_heads, width, n_tiles, n_s, chunk):
    i = pl.program_id(0)
    pair = 2 * HEAD_DIM
    pad = V7X_SUBLANES
    seq_tile = lax.rem(i, n_s)

    @pl.when(i == 0)
    def _():
        _round_weight_once(w_hbm, wb_ref, stage_ref, sem, chunk)

    @pl.when((i < n_tiles) & (seq_tile == 0))
    def _():
        halo_ref[...] = jnp.zeros(halo_ref.shape, F32)

    def pool(p, x):
        rows = x.shape[0]
        w = POOL_WINDOWS[p]
        slot = sum(v.bit_length() - 1 for v in POOL_WINDOWS[:p])
        acc, span = x, 1
        while span < w:
            prev8 = halo_ref[slot]
            halo_ref[slot] = acc[rows - pad:rows, :]
            acc = acc + _shift_rows(acc, prev8, span)
            slot += 1
            span *= 2
        pos = seq_tile * rows + lax.broadcasted_iota(jnp.int32, (rows, 1), 0)
        return (acc / jnp.minimum(pos + 1, w).astype(F32) - x).astype(BF16)

    def project(x, tables, outs, pooled_out=None):
        c, s_lo, s_hi = (t[...] for t in tables)
        u_out, q_out, k_out, v_out = outs
        rows = x.shape[0]
        hn_ref[0:rows, :] = _rms(x, g_ref[...]).astype(BF16)

        def rope(h):
            return (h * c + pltpu.roll(h, HEAD_DIM - ROT_DIM // 2, 1) * s_lo
                    + pltpu.roll(h, ROT_DIM // 2, 1) * s_hi)

        deltas = []
        for p in range(width // pair):
            cols = slice(p * pair, (p + 1) * pair)
            res = _dot(hn_ref[0:rows, :], wb_ref[:, cols])
            u_out[0, :, cols] = res
            if pooled_out is not None:
                deltas.append(pool(p, res))
        for out_ref, part, rotary in ((q_out, 1, True), (k_out, 2, True), (v_out, 3, False)):
            for p in range(n_heads // 2):
                col = part * width + p * pair
                res = _dot(hn_ref[0:rows, :], wb_ref[:, col:col + pair])
                for half in range(2):
                    h = res[:, half * HEAD_DIM:(half + 1) * HEAD_DIM]
                    out_ref[0, 2 * p + half] = rope(h) if rotary else h
        for p, d in enumerate(deltas):
            cols = slice(p * pair, (p + 1) * pair)
            pooled_out[0, :, cols] = (_dot(d, wp_ref[p]) * scale_ref[:, cols]).astype(BF16)

    @pl.when(i < n_tiles)
    def _():
        project(x_ref[0], (c_ref, slo_ref, shi_ref), (u_ref, q_ref, k_ref, v_ref), pooled_ref)

    @pl.when(i == n_tiles)
    def _():
        project(xs_ref[0], (cs_ref, slos_ref, shis_ref), (us_ref, qs_ref, ks_ref, vs_ref))


def _inproj(x, x_s, g, w_in, w_pool_bf, pool_scale, pos, pos_s, tm):
    B, S, D = x.shape
    rows_s = x_s.shape[1]
    width = w_in.shape[1] // 4
    n_heads = width // HEAD_DIM
    n_groups, group, _ = w_pool_bf.shape
    assert n_heads % 2 == 0 and rows_s <= tm
    assert group == 2 * HEAD_DIM and n_groups == len(POOL_WINDOWS) == width // group
    assert all(w & (w - 1) == 0 and w <= 2 * V7X_SUBLANES for w in POOL_WINDOWS)
    n_levels = sum(w.bit_length() - 1 for w in POOL_WINDOWS)
    n_s = S // tm
    n_tiles = B * n_s
    chunk = _pick_tile(D, 128)

    def tile(i):
        ii = jnp.minimum(i, n_tiles - 1)
        return ii // n_s, lax.rem(ii, n_s)

    tab_spec = pl.BlockSpec((tm, HEAD_DIM), lambda i: (tile(i)[1], 0))
    hm_spec = pl.BlockSpec((1, n_heads, tm, HEAD_DIM), lambda i: (tile(i)[0], 0, tile(i)[1], 0))
    hm_shape = jax.ShapeDtypeStruct((B, n_heads, S, HEAD_DIM), F32)
    whole = lambda shape: pl.BlockSpec(shape, lambda i: (0,) * len(shape))
    stab_spec = whole((rows_s, HEAD_DIM))
    shm_spec = whole((1, n_heads, rows_s, HEAD_DIM))
    shm_shape = jax.ShapeDtypeStruct((1, n_heads, rows_s, HEAD_DIM), F32)
    block_bytes = (2 * tm * D * 4 + tm * D * 2 + D * 4 * width * 2 + 2 * chunk * 4 * width * 4
                   + 2 * 4 * tm * width * 4 + 2 * tm * width * 2 + 2 * rows_s * (D + 4 * width) * 4
                   + 2 * n_groups * group * group * 2)
    row_spec = pl.BlockSpec((1, tm, width), lambda i: (tile(i)[0], tile(i)[1], 0))
    return pl.pallas_call(
        functools.partial(_inproj_kernel, n_heads=n_heads, width=width, n_tiles=n_tiles, n_s=n_s, chunk=chunk),
        grid=(n_tiles + 1,),
        in_specs=[
            pl.BlockSpec((1, tm, D), lambda i: (tile(i)[0], tile(i)[1], 0)),
            whole((1, rows_s, D)),
            whole((1, D)),
            pl.BlockSpec(memory_space=pl.ANY),
            tab_spec, tab_spec, tab_spec, stab_spec, stab_spec, stab_spec,
            whole((n_groups, group, group)), whole((1, width)),
        ],
        out_specs=[row_spec, hm_spec, hm_spec, hm_spec, row_spec,
                   whole((1, rows_s, width)), shm_spec, shm_spec, shm_spec],
        out_shape=[jax.ShapeDtypeStruct((B, S, width), F32), hm_shape, hm_shape, hm_shape,
                   jax.ShapeDtypeStruct((B, S, width), BF16),
                   jax.ShapeDtypeStruct((1, rows_s, width), F32), shm_shape, shm_shape, shm_shape],
        scratch_shapes=[pltpu.VMEM((D, 4 * width), BF16), pltpu.VMEM((tm, D), BF16),
                        pltpu.VMEM((2, chunk, 4 * width), F32), pltpu.SemaphoreType.DMA((2,)),
                        pltpu.VMEM((n_levels, V7X_SUBLANES, group), F32)],
        compiler_params=_resident_params(("arbitrary",), block_bytes),
        name="inproj",
    )(x, x_s, g.reshape(1, D), w_in, *_rope_tables(pos), *_rope_tables(pos_s), w_pool_bf,
      pool_scale.reshape(1, width))


def _shift_rows(x, prev8, k):
    pad = V7X_SUBLANES
    rolled = pltpu.roll(x, k, 0)
    row8 = lax.broadcasted_iota(jnp.int32, prev8.shape, 0)
    head = jnp.where(row8 < k, pltpu.roll(prev8, k % pad, 0) if k % pad else prev8, rolled[0:pad])
    return jnp.concatenate([head, rolled[pad:]], axis=0)


def _pool_sample_kernel(u_ref, hist_ref, w_ref, scale_ref, o_ref, *, n_steps, n_seq, group, past_len):
    def ext(i):
        if i < POOL_HIST:
            return hist_ref[i * n_seq:(i + 1) * n_seq, :]
        return u_ref[(i - POOL_HIST) * n_seq:(i - POOL_HIST + 1) * n_seq, :]

    for gi, w in enumerate(POOL_WINDOWS):
        cols = slice(gi * group, (gi + 1) * group)
        ds = []
        for t in range(n_steps):
            cur = ext(POOL_HIST + t)[:, cols]
            acc = cur
            for back in range(1, w):
                acc = acc + ext(POOL_HIST + t - back)[:, cols]
            cnt = float(min(past_len + t + 1, w))
            ds.append((acc / cnt - cur).astype(BF16))
        d = jnp.concatenate(ds, axis=0)
        y = _dot(d, w_ref[gi]) * scale_ref[:, cols]
        o_ref[:, cols] = y.astype(BF16)


def _pool_sample(u_tm, hist_tm, w_pool_bf, scale, n_steps, n_seq, past_len):
    rows, C = u_tm.shape
    n_groups, group, _ = w_pool_bf.shape
    block_bytes = 2 * (rows + hist_tm.shape[0]) * C * 4 + 2 * n_groups * group * group * 2 + 2 * rows * C * 2
    return pl.pallas_call(
        functools.partial(_pool_sample_kernel, n_steps=n_steps, n_seq=n_seq, group=group,
                          past_len=past_len),
        out_shape=jax.ShapeDtypeStruct((rows, C), BF16),
        compiler_params=pltpu.CompilerParams(vmem_limit_bytes=_vmem_limit(block_bytes)),
        name="pool_sample",
    )(u_tm, hist_tm, w_pool_bf, scale.reshape(1, C))


class _PerHead:
    def __init__(self, refs):
        self.refs = refs

    def __getitem__(self, idx):
        return self.refs[idx[0]][idx[1:]] if isinstance(idx, tuple) else self.refs[idx][...]

    def __setitem__(self, idx, value):
        self.refs[idx[0]][idx[1:]] = value


def _moba_prompt_kernel(q_ref, k_ref, v_ref, o_ref, *scratch, seq, hp):
    blk = MOBA_BLOCK
    nb = seq // blk
    n_sel = min(MOBA_TOPK, nb - 1)
    scale = HEAD_DIM ** -0.5
    kb_ref, vt_ref, km_ref, s_ref, p_ref = (_PerHead(scratch[kind * hp:(kind + 1) * hp]) for kind in range(5))

    for hh in range(hp):
        for n in range(nb):
            rows = slice(n * blk, (n + 1) * blk)
            kn = k_ref[0, hh, rows, :]
            kb_ref[hh, rows, :] = kn.astype(BF16)
            km_ref[hh, n:n + 1, :] = jnp.sum(kn, axis=0, keepdims=True) * (1.0 / blk)
            vt_ref[hh, :, rows] = v_ref[0, hh, rows, :].T.astype(BF16)

    blk_id = lax.broadcasted_iota(jnp.int32, (nb, blk), 0)
    key_i = lax.broadcasted_iota(jnp.int32, (blk, blk), 0)
    qry_i = lax.broadcasted_iota(jnp.int32, (blk, blk), 1)
    causal_bias = jnp.where(key_i <= qry_i, 0.0, NEG_INF).astype(F32)

    def scores(hh, qblocks):
        first, last = qblocks[0], qblocks[-1]
        nk = (last + 1) * blk
        width = len(qblocks) * blk
        qg = q_ref[0, hh, first * blk:nk, :]
        s_ref[hh, 0:nk, 0:width] = _dot_nt(kb_ref[hh, 0:nk, :], (qg * (scale * LOG2E)).astype(BF16))
        return _dot_nt(km_ref[hh], qg, precision=lax.Precision.HIGHEST) if last > 0 else None

    def softmax(hh, qblocks, gate):
        last = qblocks[-1]
        sums = []
        for a, i in enumerate(qblocks):
            cols = slice(a * blk, (a + 1) * blk)
            own = slice(i * blk, (i + 1) * blk)
            biases = []
            if i > 0:
                gate_a = gate[:, cols]
                valid = jnp.where(blk_id < i, 1.0, 0.0)
                for n in range(i):
                    gn = gate_a[n:n + 1, :]
                    beats = jnp.where(blk_id < n, jnp.where(gate_a >= gn, 1.0, 0.0),
                                      jnp.where(gate_a > gn, 1.0, 0.0))
                    rank = jnp.sum(beats * valid, axis=0, keepdims=True)
                    biases.append(jnp.where(rank < n_sel, 0.0, NEG_INF).astype(F32))

            s_own = s_ref[hh, own, cols] + causal_bias
            m = jnp.max(s_own, axis=0, keepdims=True)
            for n in range(i):
                m = jnp.maximum(m, jnp.max(s_ref[hh, n * blk:(n + 1) * blk, cols], axis=0, keepdims=True)
                                + biases[n])

            p = jnp.exp2(s_own - m)
            l = jnp.sum(p, axis=0, keepdims=True)
            p_ref[hh, own, cols] = p.astype(BF16)
            for n in range(i):
                rows = slice(n * blk, (n + 1) * blk)
                p = jnp.exp2(s_ref[hh, rows, cols] + (biases[n] - m))
                l = l + jnp.sum(p, axis=0, keepdims=True)
                p_ref[hh, rows, cols] = p.astype(BF16)
            for n in range(i + 1, last + 1):
                p_ref[hh, n * blk:(n + 1) * blk, cols] = jnp.zeros((blk, blk), BF16)
            sums.append(l)
        return sums

    def output(hh, qblocks, sums):
        nk = (qblocks[-1] + 1) * blk
        width = len(qblocks) * blk
        o_t = _dot(vt_ref[hh, :, 0:nk], p_ref[hh, 0:nk, 0:width])
        for a, i in enumerate(qblocks):
            o_a = o_t[:, a * blk:(a + 1) * blk] / sums[a]
            o_ref[0, i * blk:(i + 1) * blk, hh * HEAD_DIM:(hh + 1) * HEAD_DIM] = o_a.T.astype(BF16)

    items = [(hh, list(range(first, min(first + QUERY_BLOCKS_PER_DOT, nb))))
             for first in range(0, nb, QUERY_BLOCKS_PER_DOT) for hh in range(hp)]
    gate = scores(*items[0])
    for cur, nxt in zip(items, items[1:] + [None]):
        next_gate = scores(*nxt) if nxt is not None else None
        output(*cur, softmax(*cur, gate))
        gate = next_gate


def _moba_prompt(q, k, v):
    B, H, S, Dh = q.shape
    nb = S // MOBA_BLOCK
    hp = 2 if H % 2 == 0 else 1
    width = QUERY_BLOCKS_PER_DOT * MOBA_BLOCK
    in_spec = pl.BlockSpec((1, hp, S, Dh), lambda b, g: (b, g, 0, 0))
    block_bytes = hp * (2 * 3 * S * Dh * 4 + 2 * S * Dh * 2 + 2 * S * Dh * 2 + S * width * 6)
    return pl.pallas_call(
        functools.partial(_moba_prompt_kernel, seq=S, hp=hp),
        grid=(B, H // hp),
        in_specs=[in_spec, in_spec, in_spec],
        out_specs=pl.BlockSpec((1, S, hp * Dh), lambda b, g: (b, 0, g)),
        out_shape=jax.ShapeDtypeStruct((B, S, H * Dh), BF16),
        scratch_shapes=(
            [pltpu.VMEM((S, Dh), BF16)] * hp
            + [pltpu.VMEM((Dh, S), BF16)] * hp
            + [pltpu.VMEM((nb, Dh), F32)] * hp
            + [pltpu.VMEM((S, width), F32)] * hp
            + [pltpu.VMEM((S, width), BF16)] * hp
        ),
        compiler_params=_params(("arbitrary", "arbitrary"), block_bytes),
        name="moba_prompt",
    )(q, k, v)


class _PagedAttention:
    def __init__(self, pt_ref, qc_ref, qp_ref, kn_ref, vn_ref, ck_ref, cv_ref, o_ref,
                 kbuf, ksel, vbuf, km_ref, q8_ref, sel_ref, ksem, vsem, *, n_pages, n_groups, hp, n_new):
        self.__dict__.update(locals())
        self.ppb = MOBA_BLOCK // PAGE_SIZE
        self.nblk = n_pages // self.ppb
        self.n_sel = min(MOBA_TOPK, self.nblk)

    def unit_bh(self, u):
        return u // self.n_groups, lax.rem(u, self.n_groups) * self.hp

    def k_copy(self, u, page):
        bb, h0 = self.unit_bh(u)
        sl = lax.rem(u, 2)
        return pltpu.make_async_copy(self.ck_ref.at[self.pt_ref[bb, page], pl.ds(h0, self.hp)],
                                     self.kbuf.at[sl, :, pl.ds(page * PAGE_SIZE, PAGE_SIZE), :],
                                     self.ksem.at[sl])

    def start_k(self, u):
        def body(page, carry):
            self.k_copy(u, page).start()
            return carry
        lax.fori_loop(0, self.n_pages, body, 0, unroll=8)

    def wait_k(self, u):
        for page in range(self.n_pages):
            self.k_copy(u, page).wait()

    def sel_index(self, hh, j, c):
        return (hh * self.n_new + j) * self.n_sel + c

    def v_copies(self, u, heads):
        bb, h0 = self.unit_bh(u)
        sl = lax.rem(u, 2)
        for hh in heads:
            for j in range(self.n_new):
                for c in range(self.n_sel):
                    blk = self.sel_ref[sl, self.sel_index(hh, j, c)]
                    for pg in range(self.ppb):
                        page = self.pt_ref[bb, blk * self.ppb + pg]
                        dst = ((j * self.n_sel + c) * self.ppb + pg) * PAGE_SIZE
                        yield pltpu.make_async_copy(self.cv_ref.at[page, h0 + hh],
                                                    self.vbuf.at[sl, hh, pl.ds(dst, PAGE_SIZE), :],
                                                    self.vsem.at[sl])

    def start_v(self, u, heads):
        for cp in self.v_copies(u, heads):
            cp.start()

    def wait_v(self, u):
        for cp in self.v_copies(u, range(self.hp)):
            cp.wait()

    def block_means(self, u, heads):
        sl = lax.rem(u, 2)
        for hh in heads:
            for n in range(self.nblk):
                kn_blk = self.kbuf[sl, hh, n * MOBA_BLOCK:(n + 1) * MOBA_BLOCK, :]
                self.km_ref[hh, n:n + 1, :] = jnp.sum(kn_blk, axis=0, keepdims=True) * (1.0 / MOBA_BLOCK)

    def select(self, u, heads):
        sl = lax.rem(u, 2)
        for hh in heads:
            self.q8_ref[hh] = jnp.zeros(self.q8_ref.shape[1:], F32)
            self.q8_ref[hh, 0:self.n_new, :] = self.qc_ref[0, hh]
            gate = _dot_nt(self.q8_ref[hh], self.km_ref[hh], precision=lax.Precision.HIGHEST)
            lane = lax.broadcasted_iota(jnp.int32, gate.shape, 1)
            picks = [jnp.zeros((gate.shape[0], 1), F32) for _ in range(self.n_sel)]
            for n in range(self.nblk):
                gn = gate[:, n:n + 1]
                beats = jnp.where(lane < n, jnp.where(gate >= gn, 1.0, 0.0), jnp.where(gate > gn, 1.0, 0.0))
                rank = jnp.sum(beats, axis=1, keepdims=True)
                for c in range(self.n_sel):
                    picks[c] = jnp.where(rank == float(c), float(n), picks[c])
            for j in range(self.n_new):
                for c in range(self.n_sel):
                    blk = picks[c][j, 0].astype(jnp.int32)
                    self.sel_ref[sl, self.sel_index(hh, j, c)] = blk
                    row0 = (j * self.n_sel + c) * MOBA_BLOCK
                    start = pl.multiple_of(blk * MOBA_BLOCK, MOBA_BLOCK)
                    self.ksel[sl, hh, row0:row0 + MOBA_BLOCK, :] = self.kbuf[sl, hh, pl.ds(start, MOBA_BLOCK), :]

    def attend(self, u, heads):
        sl = lax.rem(u, 2)
        scale = HEAD_DIM ** -0.5
        new_row = lax.broadcasted_iota(jnp.int32, (self.n_new, 1), 0)
        for hh in heads:
            q = self.qp_ref[0, hh]
            kn = self.kn_ref[0, hh]
            vn = self.vn_ref[0, hh]
            for j in range(self.n_new):
                qj = q[j:j + 1, :] * (scale * LOG2E)
                rows = [slice((j * self.n_sel + c) * MOBA_BLOCK, (j * self.n_sel + c + 1) * MOBA_BLOCK)
                        for c in range(self.n_sel)]
                s_sel = [jnp.sum(self.ksel[sl, hh, r, :] * qj, axis=1, keepdims=True) for r in rows]
                s_new = jnp.sum(kn * qj, axis=1, keepdims=True)
                s_new = jnp.where(new_row <= j, s_new, NEG_INF)
                m = jnp.max(s_new, axis=0, keepdims=True)
                for s in s_sel:
                    m = jnp.maximum(m, jnp.max(s, axis=0, keepdims=True))
                p_new = jnp.exp2(s_new - m)
                l = jnp.sum(p_new, axis=0, keepdims=True)
                acc = jnp.sum(p_new * vn, axis=0, keepdims=True)
                for s, r in zip(s_sel, rows):
                    p = jnp.exp2(s - m)
                    l = l + jnp.sum(p, axis=0, keepdims=True)
                    acc = acc + jnp.sum(p * self.vbuf[sl, hh, r, :], axis=0, keepdims=True)
                self.o_ref[0, hh, j:j + 1, :] = acc / l


def _paged_attention_scratch(n_pages, hp, n_new):
    nblk = n_pages // (MOBA_BLOCK // PAGE_SIZE)
    n_sel = min(MOBA_TOPK, nblk)
    picked_rows = n_new * n_sel * MOBA_BLOCK
    shapes = [
        pltpu.VMEM((2, hp, n_pages * PAGE_SIZE, HEAD_DIM), F32),
        pltpu.VMEM((2, hp, picked_rows, HEAD_DIM), F32),
        pltpu.VMEM((2, hp, picked_rows, HEAD_DIM), F32),
        pltpu.VMEM((hp, nblk, HEAD_DIM), F32),
        pltpu.VMEM((hp, V7X_SUBLANES, HEAD_DIM), F32),
        pltpu.SMEM((2, hp * n_new * n_sel), jnp.int32),
        pltpu.SemaphoreType.DMA((2,)),
        pltpu.SemaphoreType.DMA((2,)),
    ]
    n_bytes = 4 * HEAD_DIM * 2 * hp * (n_pages * PAGE_SIZE + 2 * picked_rows)
    return shapes, n_bytes


def _outproj_kernel(pooled_ref, attn_ref, x_ref, w_ref, gpost_ref, gffn_ref, x1_ref, h2_ref, *, split, sub):
    for r in range(x_ref.shape[0] // sub):
        rows = slice(r * sub, (r + 1) * sub)
        mix = _dot(pooled_ref[rows, :], w_ref[0:split, :]) + _dot(attn_ref[rows, :], w_ref[split:, :])
        x1 = x_ref[rows, :] + _rms(mix, gpost_ref[...])
        x1_ref[rows, :] = x1
        h2_ref[rows, :] = _rms(x1, gffn_ref[...]).astype(BF16)


def _outproj(pooled, attn, x, w_bf, g_post, g_ffn, tm):
    rows, D = x.shape
    split = pooled.shape[1]
    wa = attn.shape[1]
    row_spec = lambda width: pl.BlockSpec((tm, width), lambda i: (i, 0))
    vec_spec = pl.BlockSpec((1, D), lambda i: (0, 0))
    block_bytes = 2 * tm * (split + wa) * 2 + 2 * tm * D * 4 + 2 * (split + wa) * D * 2 \
        + 2 * tm * D * 6 + 2 * tm * D * 4
    return pl.pallas_call(
        functools.partial(_outproj_kernel, split=split, sub=_pick_tile(tm, 128)),
        grid=(rows // tm,),
        in_specs=[row_spec(split), row_spec(wa), row_spec(D),
                  pl.BlockSpec((split + wa, D), lambda i: (0, 0)), vec_spec, vec_spec],
        out_specs=[row_spec(D), row_spec(D)],
        out_shape=[jax.ShapeDtypeStruct((rows, D), F32), jax.ShapeDtypeStruct((rows, D), BF16)],
        compiler_params=_params(("arbitrary",), block_bytes),
        name="outproj",
    )(pooled, attn, x, w_bf, g_post.reshape(1, D), g_ffn.reshape(1, D))


def _gelu_tanh(c):
    return c * (0.5 * (1.0 + jnp.tanh(0.7978845608028654 * (c + 0.044715 * (c * c * c)))))


def _conv_gelu_gate(gt, prev8, up, cw_ref, cb_ref):
    c = cb_ref[...]
    for i in range(CONV_W):
        back = CONV_W - 1 - i
        c = c + (_shift_rows(gt, prev8, back) if back else gt) * cw_ref[i:i + 1, :]
    return _gelu_tanh(c) * up


def _ffn_up_kernel(pt_ref, h_ref, wg_ref, wu_ref, cw_ref, cb_ref, qc_ref, qp_ref, kn_ref, vn_ref, ck_ref, cv_ref,
                   f_ref, tail_ref, o_ref, wgb_ref, wub_ref, halo_ref, *attn_scratch,
                   tm, sub, n_b, n_s, n_units, attn_params):
    j, b, s = pl.program_id(0), pl.program_id(1), pl.program_id(2)
    t = (j * n_b + b) * n_s + s
    pad = V7X_SUBLANES
    last_sub = tm // sub - 1
    attn = _PagedAttention(pt_ref, qc_ref, qp_ref, kn_ref, vn_ref, ck_ref, cv_ref, o_ref,
                           *attn_scratch, **attn_params)

    @pl.when((b == 0) & (s == 0))
    def _():
        wgb_ref[...] = wg_ref[...].astype(BF16)
        wub_ref[...] = wu_ref[...].astype(BF16)

    @pl.when(s == 0)
    def _():
        halo_ref[...] = jnp.zeros(halo_ref.shape, F32)

    @pl.when(t == 0)
    def _():
        attn.start_k(t)

    @pl.when(t < n_units)
    def _():
        attn.wait_k(t)

    @pl.when(t + 1 < n_units)
    def _():
        attn.start_k(t + 1)

    def gate_up(before=None, middle=None, end=()):
        before, middle = before or {}, middle or {}
        prev8 = halo_ref[...]
        half = sub // 2
        for r in range(tm // sub):
            for stage in before.get(r, ()):
                stage()
            row0 = r * sub
            if r in middle:
                lo = h_ref[0, row0:row0 + half, :]
                gt_lo, up_lo = _dot(lo, wgb_ref[...]), _dot(lo, wub_ref[...])
                for stage in middle[r]:
                    stage()
                hi = h_ref[0, row0 + half:row0 + sub, :]
                gt = jnp.concatenate([gt_lo, _dot(hi, wgb_ref[...])], axis=0)
                up = jnp.concatenate([up_lo, _dot(hi, wub_ref[...])], axis=0)
            else:
                hr = h_ref[0, row0:row0 + sub, :]
                gt, up = _dot(hr, wgb_ref[...]), _dot(hr, wub_ref[...])
            f_ref[0, row0:row0 + sub, :] = _conv_gelu_gate(gt, prev8, up, cw_ref, cb_ref).astype(BF16)
            prev8 = gt[sub - pad:sub, :]
        halo_ref[...] = prev8
        tail_ref[0] = prev8
        for stage in end:
            stage()

    heads = tuple(range(attn.hp))

    def means():
        attn.block_means(t, heads)

    def select():
        attn.select(t, heads)

    def attend_previous():
        attn.wait_v(t - 1)
        attn.attend(t - 1, heads)

    def fetch_values():
        attn.start_v(t, heads)

    @pl.when(t == 0)
    def _():
        gate_up(before={0: [means]}, middle={0: [select]}, end=[fetch_values])

    @pl.when((t >= 1) & (t < n_units))
    def _():
        before = {0: [means]}
        before.setdefault(last_sub, []).append(attend_previous)
        gate_up(before=before, middle={0: [select]}, end=[fetch_values])

    @pl.when(t == n_units)
    def _():
        gate_up(before={last_sub: [attend_previous]})

    @pl.when(t > n_units)
    def _():
        gate_up()


def _ffn_up(h2, w_gate, w_up, conv_w, conv_b, q, k_new, v_new, cache_k, cache_v, page_table, tm, tn, sub):
    B, S, D = h2.shape
    F = w_gate.shape[1]
    Bd, H, L, Dh = q.shape
    n_pages = page_table.shape[1]
    ppb = MOBA_BLOCK // PAGE_SIZE
    assert n_pages % ppb == 0, "past length must be a whole number of MoBA blocks"
    assert n_pages >= ppb and L <= V7X_SUBLANES
    hp = 2 if H % 2 == 0 else 1
    n_groups = H // hp
    n_units = Bd * n_groups
    n_b, n_s = B, S // tm
    assert (F // tn) * n_b * n_s > n_units >= 2, "not enough grid steps to host the sample attention"
    pad = V7X_SUBLANES

    def unit_spec(lag):
        def index_map(j, b, s, pt):
            u = jnp.clip((j * n_b + b) * n_s + s - lag, 0, n_units - 1)
            return (u // n_groups, lax.rem(u, n_groups), 0, 0)
        return pl.BlockSpec((1, hp, L, Dh), index_map)

    w_spec = pl.BlockSpec((D, tn), lambda j, b, s, pt: (0, j))
    any_spec = pl.BlockSpec(memory_space=pl.ANY)
    attn_scratch, attn_bytes = _paged_attention_scratch(n_pages, hp, L)
    block_bytes = (2 * tm * D * 2 + 2 * 2 * D * tn * 4 + 2 * D * tn * 2 + 2 * tm * tn * 2 + 8 * sub * tn * 4
                   + attn_bytes)
    grid_spec = pltpu.PrefetchScalarGridSpec(
        num_scalar_prefetch=1,
        grid=(F // tn, n_b, n_s),
        in_specs=[
            pl.BlockSpec((1, tm, D), lambda j, b, s, pt: (b, s, 0)),
            w_spec, w_spec,
            pl.BlockSpec((CONV_W, tn), lambda j, b, s, pt: (0, j)),
            pl.BlockSpec((1, tn), lambda j, b, s, pt: (0, j)),
            unit_spec(0), unit_spec(1), unit_spec(1), unit_spec(1), any_spec, any_spec,
        ],
        out_specs=[pl.BlockSpec((1, tm, tn), lambda j, b, s, pt: (b, s, j)),
                   pl.BlockSpec((1, pad, tn), lambda j, b, s, pt: (b, 0, j)),
                   unit_spec(1)],
        scratch_shapes=[pltpu.VMEM((D, tn), BF16), pltpu.VMEM((D, tn), BF16), pltpu.VMEM((pad, tn), F32)]
        + attn_scratch,
    )
    return pl.pallas_call(
        functools.partial(_ffn_up_kernel, tm=tm, sub=sub, n_b=n_b, n_s=n_s, n_units=n_units,
                          attn_params=dict(n_pages=n_pages, n_groups=n_groups, hp=hp, n_new=L)),
        grid_spec=grid_spec,
        out_shape=[jax.ShapeDtypeStruct((B, S, F), BF16), jax.ShapeDtypeStruct((B, pad, F), F32),
                   jax.ShapeDtypeStruct((Bd, H, L, Dh), F32)],
        compiler_params=_resident_params(("arbitrary", "arbitrary", "arbitrary"), block_bytes),
        name="ffn_up",
    )(page_table, h2, w_gate, w_up, conv_w, conv_b.reshape(1, F), q, q, k_new, v_new, cache_k, cache_v)


def _ffn_up_sample_kernel(h_ref, wg_ref, wu_ref, cw_ref, cb_ref, hist_ref, f_ref, gt_ref, *, n_steps, n_seq):
    h = h_ref[...]
    gt = _dot(h, wg_ref[...].astype(BF16))
    up = _dot(h, wu_ref[...].astype(BF16))
    gt_ref[...] = gt
    ext = [hist_ref[i * n_seq:(i + 1) * n_seq, :] for i in range(CONV_W - 1)]
    ext += [gt[t * n_seq:(t + 1) * n_seq, :] for t in range(n_steps)]
    for t in range(n_steps):
        c = cb_ref[...]
        for i in range(CONV_W):
            c = c + ext[t + i] * cw_ref[i:i + 1, :]
        rows = slice(t * n_seq, (t + 1) * n_seq)
        f_ref[rows, :] = (_gelu_tanh(c) * up[rows, :]).astype(BF16)


def _ffn_up_sample(h2, w_gate, w_up, conv_w, conv_b, hist_tm, n_steps, n_seq, tn):
    rows, D = h2.shape
    F = w_gate.shape[1]
    col_spec = lambda r: pl.BlockSpec((r, tn), lambda j: (0, j))
    block_bytes = 2 * rows * D * 2 + 2 * 2 * D * tn * 4 + 2 * D * tn * 2 + 2 * rows * tn * 6 \
        + 2 * hist_tm.shape[0] * tn * 4 + 4 * rows * tn * 4
    return pl.pallas_call(
        functools.partial(_ffn_up_sample_kernel, n_steps=n_steps, n_seq=n_seq),
        grid=(F // tn,),
        in_specs=[pl.BlockSpec((rows, D), lambda j: (0, 0)), col_spec(D), col_spec(D),
                  col_spec(CONV_W), col_spec(1), col_spec(hist_tm.shape[0])],
        out_specs=[col_spec(rows), col_spec(rows)],
        out_shape=[jax.ShapeDtypeStruct((rows, F), BF16), jax.ShapeDtypeStruct((rows, F), F32)],
        compiler_params=_params(("arbitrary",), block_bytes),
        name="ffn_up_sample",
    )(h2, w_gate, w_up, conv_w, conv_b.reshape(1, F), hist_tm)


def _ffn_down_kernel(f_ref, fs_ref, w_hbm, x1_ref, x1s_ref, g_ref, y_ref, ys_ref, wb_ref, stage_ref, sem,
                     *, n_tiles, chunk):
    i = pl.program_id(0)

    @pl.when(i == 0)
    def _():
        _round_weight_once(w_hbm, wb_ref, stage_ref, sem, chunk)

    @pl.when(i < n_tiles)
    def _():
        y_ref[...] = x1_ref[...] + _rms(_dot(f_ref[...], wb_ref[...]), g_ref[...])

    @pl.when(i == n_tiles)
    def _():
        ys_ref[...] = x1s_ref[...] + _rms(_dot(fs_ref[...], wb_ref[...]), g_ref[...])


def _ffn_down(f, f_s, w_down, x1, x1_s, g, tm):
    rows, F = f.shape
    rows_s = f_s.shape[0]
    D = w_down.shape[1]
    n_tiles = rows // tm
    chunk = _pick_tile(F, 512)
    while (F // chunk) % 2 == 0 and chunk * D * 4 > 4 * 1024 * 1024:
        chunk //= 2
    assert chunk % 16 == 0
    tile = lambda r: pl.BlockSpec((tm, r), lambda i: (jnp.minimum(i, n_tiles - 1), 0))
    whole = lambda r, c: pl.BlockSpec((r, c), lambda i: (0, 0))
    block_bytes = (2 * tm * F * 2 + 2 * rows_s * F * 2 + F * D * 2 + 2 * chunk * D * 4 + 4 * tm * D * 4
                   + 4 * rows_s * D * 4 + 2 * tm * D * 4)
    return pl.pallas_call(
        functools.partial(_ffn_down_kernel, n_tiles=n_tiles, chunk=chunk),
        grid=(n_tiles + 1,),
        in_specs=[tile(F), whole(rows_s, F), pl.BlockSpec(memory_space=pl.ANY), tile(D), whole(rows_s, D),
                  whole(1, D)],
        out_specs=[tile(D), whole(rows_s, D)],
        out_shape=[jax.ShapeDtypeStruct((rows, D), F32), jax.ShapeDtypeStruct((rows_s, D), F32)],
        scratch_shapes=[pltpu.VMEM((F, D), BF16), pltpu.VMEM((2, chunk, D), F32), pltpu.SemaphoreType.DMA((2,))],
        compiler_params=_resident_params(("arbitrary",), block_bytes),
        name="ffn_down",
    )(f, f_s, w_down, x1, x1_s, g.reshape(1, D))


def _pick_tile(n, target):
    t = min(n, target)
    while n % t:
        t //= 2
    return t


def _mixer_inputs(x, x_tm, pool_hist, page_table, lp, n_seq, n_steps):
    B, S, D = x.shape
    tm = _pick_tile(S, 512)
    past_len = page_table.shape[1] * PAGE_SIZE
    pos_s = past_len + jnp.repeat(jnp.arange(n_steps), n_seq)
    u, q, k, v, pooled, u_s, q_s, k_s, v_s = _inproj(x, x_tm[None], lp["g_mix_pre"], lp["w_in"], lp["w_pool"],
                                                     lp["pool_scale"], jnp.arange(S), pos_s, tm)
    attn = _moba_prompt(q, k, v)
    x1, h2 = _outproj(pooled.reshape(B * S, -1), attn.reshape(B * S, -1), x.reshape(B * S, D),
                      lp["w_out"], lp["g_mix_post"], lp["g_ffn_pre"], tm)

    def to_seq_major(t):
        H = t.shape[0]
        return t.reshape(H, n_steps, n_seq, HEAD_DIM).transpose(2, 0, 1, 3)

    pooled_s = _pool_sample(u_s[0], _seq_to_rows(pool_hist), lp["w_pool"], lp["pool_scale"],
                            n_steps, n_seq, past_len)
    sample = (to_seq_major(q_s[0]), to_seq_major(k_s[0]), to_seq_major(v_s[0]), pooled_s, u_s[0])
    return (x1, h2, k, v, u), sample


def _rows_to_seq(t, n_steps, n_seq):
    return t.reshape(n_steps, n_seq, t.shape[-1]).transpose(1, 0, 2)


def _seq_to_rows(t):
    return t.transpose(1, 0, 2).reshape(-1, t.shape[-1])


def _layer(xp, xs_tm, pool_hist, conv_hist, cache_k, cache_v, page_table, lp, n_seq, n_steps):
    B, S, D = xp.shape
    F = lp["w_gate"].shape[1]
    rows_s = xs_tm.shape[0]
    (x1p, h2p, kp, vp, up), (qs, ks, vs, pooled_s, us) = _mixer_inputs(xp, xs_tm, pool_hist, page_table, lp,
                                                                       n_seq, n_steps)
    tm = _pick_tile(S, 512)
    tn = _pick_tile(F, 512)
    fp, tail, attn_s = _ffn_up(h2p.reshape(B, S, D), lp["w_gate"], lp["w_up"], lp["conv_w"], lp["conv_b"],
                               qs, ks, vs, cache_k, cache_v, page_table, tm, tn, _pick_tile(tm, 256))
    attn_tm = attn_s.transpose(2, 0, 1, 3).reshape(rows_s, -1).astype(BF16)
    x1s, h2s = _outproj(pooled_s, attn_tm, xs_tm, lp["w_out"], lp["g_mix_post"], lp["g_ffn_pre"], rows_s)
    fs, gts = _ffn_up_sample(h2s, lp["w_gate"], lp["w_up"], lp["conv_w"], lp["conv_b"],
                             _seq_to_rows(conv_hist), n_steps, n_seq, tn)
    yp, ys = _ffn_down(fp.reshape(B * S, F), fs, lp["w_down"], x1p, x1s, lp["g_ffn_post"],
                       _pick_tile(B * S, 256))
    pool_p = up[:, S - POOL_HIST:, :]
    conv_p = tail[:, V7X_SUBLANES - (CONV_W - 1):, :]
    pool_s = jnp.concatenate([pool_hist, _rows_to_seq(us, n_steps, n_seq)], axis=1)[:, n_steps:]
    conv_s = jnp.concatenate([conv_hist, _rows_to_seq(gts, n_steps, n_seq)], axis=1)[:, n_steps:]
    return yp.reshape(B, S, D), ys, (kp, vp, ks, vs, pool_p, pool_s, conv_p, conv_s)


def kernel(x_prompt, x_sample, cache_k, cache_v, state_pool, state_conv, page_table,
           w_in, w_pool, pool_scale, w_out, g_mix_pre, g_mix_post,
           w_gate, w_up, conv_w, conv_b, w_down, g_ffn_pre, g_ffn_post):
    depth = w_in.shape[0]
    n_seq, n_steps, D = x_sample.shape
    assert w_in.shape[2] == 4 * pool_scale.shape[1], "pooling and attention widths must match"
    yp = x_prompt
    ys = x_sample.transpose(1, 0, 2).reshape(n_steps * n_seq, D)
    outs = [[] for _ in range(8)]
    for l in range(depth):
        lp = {"w_in": w_in[l], "w_pool": w_pool[l].astype(BF16), "pool_scale": pool_scale[l],
              "w_out": w_out[l].astype(BF16), "g_mix_pre": g_mix_pre[l], "g_mix_post": g_mix_post[l],
              "w_gate": w_gate[l], "w_up": w_up[l], "conv_w": conv_w[l],
              "conv_b": conv_b[l], "w_down": w_down[l], "g_ffn_pre": g_ffn_pre[l],
              "g_ffn_post": g_ffn_post[l]}
        yp, ys, states = _layer(yp, ys, state_pool[l], state_conv[l], cache_k[l], cache_v[l],
                                page_table, lp, n_seq, n_steps)
        for lst, val in zip(outs, states):
            lst.append(val)
    y_sample = ys.reshape(n_steps, n_seq, D).transpose(1, 0, 2)
    return (yp, y_sample) + tuple(jnp.stack(o) for o in outs)
```

```python
import functools

import jax
import jax.numpy as jnp
from jax import lax
from jax.experimental import pallas as pl
from jax.experimental.pallas import tpu as pltpu

F32 = jnp.float32
BF16 = jnp.bfloat16

POOL_WINDOWS = (2, 4, 8, 16)
POOL_HIST = max(POOL_WINDOWS) - 1
HEAD_DIM = 128
ROT_DIM = HEAD_DIM // 4
ROPE_THETA = 500000.0
MOBA_BLOCK = 256
MOBA_TOPK = 3
QUERY_BLOCKS_PER_DOT = 2
PAGE_SIZE = 128
CONV_W = 3
EPS = 1e-6
NEG_INF = float("-inf")
LOG2E = 1.4426950408889634

V7X_SUBLANES = 8
V7X_VMEM_BYTES = 64 * 1024 * 1024
VMEM_RESERVE_BYTES = 4 * 1024 * 1024


def _vmem_limit(block_bytes):
    return int(min(2 * block_bytes, V7X_VMEM_BYTES - VMEM_RESERVE_BYTES))


def _params(semantics, block_bytes):
    return pltpu.CompilerParams(dimension_semantics=semantics,
                                vmem_limit_bytes=_vmem_limit(block_bytes))


def _resident_params(semantics, block_bytes):
    limit = int(min(block_bytes + VMEM_RESERVE_BYTES, V7X_VMEM_BYTES - VMEM_RESERVE_BYTES))
    return pltpu.CompilerParams(dimension_semantics=semantics, vmem_limit_bytes=limit)


def _rms(x, g):
    return x * lax.rsqrt(jnp.mean(x * x, axis=-1, keepdims=True) + EPS) * g


def _dot(a, b):
    return jnp.dot(a, b, preferred_element_type=F32)


def _dot_nt(a, b, precision=None):
    return lax.dot_general(a, b, (((1,), (1,)), ((), ())), precision=precision,
                           preferred_element_type=F32)


def _rope_tables(pos):
    half = ROT_DIM // 2
    inv = ROPE_THETA ** (-jnp.arange(half, dtype=F32) * (2.0 / ROT_DIM))
    ang = pos.astype(F32)[:, None] * inv[None, :]
    cos, sin = jnp.cos(ang), jnp.sin(ang)
    n = pos.shape[0]
    c = jnp.concatenate([cos, cos, jnp.ones((n, HEAD_DIM - ROT_DIM), F32)], axis=1)
    s_lo = jnp.concatenate([-sin, jnp.zeros((n, HEAD_DIM - half), F32)], axis=1)
    s_hi = jnp.concatenate([jnp.zeros((n, half), F32), sin,
                            jnp.zeros((n, HEAD_DIM - ROT_DIM), F32)], axis=1)
    return c, s_lo, s_hi


def _round_weight_once(w_hbm, wb_ref, stage_ref, sem, chunk):
    n_chunks = wb_ref.shape[0] // chunk

    def w_copy(c):
        return pltpu.make_async_copy(w_hbm.at[pl.ds(c * chunk, chunk), :], stage_ref.at[c % 2], sem.at[c % 2])

    w_copy(0).start()
    for c in range(n_chunks):
        if c + 1 < n_chunks:
            w_copy(c + 1).start()
        w_copy(c).wait()
        wb_ref[c * chunk:(c + 1) * chunk, :] = stage_ref[c % 2].astype(BF16)


def _inproj_kernel(x_ref, xs_ref, g_ref, w_hbm, c_ref, slo_ref, shi_ref, cs_ref, slos_ref, shis_ref,
                   wp_ref, scale_ref,
                   u_ref, q_ref, k_ref, v_ref, pooled_ref, us_ref, qs_ref, ks_ref, vs_ref,
                   wb_ref, hn_ref, stage_ref, sem, halo_ref, *, n_heads, width, n_tiles, n_s, chunk):
    i = pl.program_id(0)
    pair = 2 * HEAD_DIM
    pad = V7X_SUBLANES
    seq_tile = lax.rem(i, n_s)

    @pl.when(i == 0)
    def _():
        _round_weight_once(w_hbm, wb_ref, stage_ref, sem, chunk)

    @pl.when((i < n_tiles) & (seq_tile == 0))
    def _():
        halo_ref[...] = jnp.zeros(halo_ref.shape, F32)

    def pool(p, x):
        rows = x.shape[0]
        w = POOL_WINDOWS[p]
        slot = sum(v.bit_length() - 1 for v in POOL_WINDOWS[:p])
        acc, span = x, 1
        while span < w:
            prev8 = halo_ref[slot]
            halo_ref[slot] = acc[rows - pad:rows, :]
            acc = acc + _shift_rows(acc, prev8, span)
            slot += 1
            span *= 2
        pos = seq_tile * rows + lax.broadcasted_iota(jnp.int32, (rows, 1), 0)
        return (acc / jnp.minimum(pos + 1, w).astype(F32) - x).astype(BF16)

    def project(x, tables, outs, pooled_out=None):
        c, s_lo, s_hi = (t[...] for t in tables)
        u_out, q_out, k_out, v_out = outs
        rows = x.shape[0]
        hn_ref[0:rows, :] = _rms(x, g_ref[...]).astype(BF16)

        def rope(h):
            return (h * c + pltpu.roll(h, HEAD_DIM - ROT_DIM // 2, 1) * s_lo
                    + pltpu.roll(h, ROT_DIM // 2, 1) * s_hi)

        deltas = []
        for p in range(width // pair):
            cols = slice(p * pair, (p + 1) * pair)
            res = _dot(hn_ref[0:rows, :], wb_ref[:, cols])
            u_out[0, :, cols] = res
            if pooled_out is not None:
                deltas.append(pool(p, res))
        for out_ref, part, rotary in ((q_out, 1, True), (k_out, 2, True), (v_out, 3, False)):
            for p in range(n_heads // 2):
                col = part * width + p * pair
                res = _dot(hn_ref[0:rows, :], wb_ref[:, col:col + pair])
                for half in range(2):
                    h = res[:, half * HEAD_DIM:(half + 1) * HEAD_DIM]
                    out_ref[0, 2 * p + half] = rope(h) if rotary else h
        for p, d in enumerate(deltas):
            cols = slice(p * pair, (p + 1) * pair)
            pooled_out[0, :, cols] = (_dot(d, wp_ref[p]) * scale_ref[:, cols]).astype(BF16)

    @pl.when(i < n_tiles)
    def _():
        project(x_ref[0], (c_ref, slo_ref, shi_ref), (u_ref, q_ref, k_ref, v_ref), pooled_ref)

    @pl.when(i == n_tiles)
    def _():
        project(xs_ref[0], (cs_ref, slos_ref, shis_ref), (us_ref, qs_ref, ks_ref, vs_ref))


def _inproj(x, x_s, g, w_in, w_pool_bf, pool_scale, pos, pos_s, tm):
    B, S, D = x.shape
    rows_s = x_s.shape[1]
    width = w_in.shape[1] // 4
    n_heads = width // HEAD_DIM
    n_groups, group, _ = w_pool_bf.shape
    assert n_heads % 2 == 0 and rows_s <= tm
    assert group == 2 * HEAD_DIM and n_groups == len(POOL_WINDOWS) == width // group
    assert all(w & (w - 1) == 0 and w <= 2 * V7X_SUBLANES for w in POOL_WINDOWS)
    n_levels = sum(w.bit_length() - 1 for w in POOL_WINDOWS)
    n_s = S // tm
    n_tiles = B * n_s
    chunk = _pick_tile(D, 128)

    def tile(i):
        ii = jnp.minimum(i, n_tiles - 1)
        return ii // n_s, lax.rem(ii, n_s)

    tab_spec = pl.BlockSpec((tm, HEAD_DIM), lambda i: (tile(i)[1], 0))
    hm_spec = pl.BlockSpec((1, n_heads, tm, HEAD_DIM), lambda i: (tile(i)[0], 0, tile(i)[1], 0))
    hm_shape = jax.ShapeDtypeStruct((B, n_heads, S, HEAD_DIM), F32)
    whole = lambda shape: pl.BlockSpec(shape, lambda i: (0,) * len(shape))
    stab_spec = whole((rows_s, HEAD_DIM))
    shm_spec = whole((1, n_heads, rows_s, HEAD_DIM))
    shm_shape = jax.ShapeDtypeStruct((1, n_heads, rows_s, HEAD_DIM), F32)
    block_bytes = (2 * tm * D * 4 + tm * D * 2 + D * 4 * width * 2 + 2 * chunk * 4 * width * 4
                   + 2 * 4 * tm * width * 4 + 2 * tm * width * 2 + 2 * rows_s * (D + 4 * width) * 4
                   + 2 * n_groups * group * group * 2)
    row_spec = pl.BlockSpec((1, tm, width), lambda i: (tile(i)[0], tile(i)[1], 0))
    return pl.pallas_call(
        functools.partial(_inproj_kernel, n_heads=n_heads, width=width, n_tiles=n_tiles, n_s=n_s, chunk=chunk),
        grid=(n_tiles + 1,),
        in_specs=[
            pl.BlockSpec((1, tm, D), lambda i: (tile(i)[0], tile(i)[1], 0)),
            whole((1, rows_s, D)),
            whole((1, D)),
            pl.BlockSpec(memory_space=pl.ANY),
            tab_spec, tab_spec, tab_spec, stab_spec, stab_spec, stab_spec,
            whole((n_groups, group, group)), whole((1, width)),
        ],
        out_specs=[row_spec, hm_spec, hm_spec, hm_spec, row_spec,
                   whole((1, rows_s, width)), shm_spec, shm_spec, shm_spec],
        out_shape=[jax.ShapeDtypeStruct((B, S, width), F32), hm_shape, hm_shape, hm_shape,
                   jax.ShapeDtypeStruct((B, S, width), BF16),
                   jax.ShapeDtypeStruct((1, rows_s, width), F32), shm_shape, shm_shape, shm_shape],
        scratch_shapes=[pltpu.VMEM((D, 4 * width), BF16), pltpu.VMEM((tm, D), BF16),
                        pltpu.VMEM((2, chunk, 4 * width), F32), pltpu.SemaphoreType.DMA((2,)),
                        pltpu.VMEM((n_levels, V7X_SUBLANES, group), F32)],
        compiler_params=_resident_params(("arbitrary",), block_bytes),
        name="inproj",
    )(x, x_s, g.reshape(1, D), w_in, *_rope_tables(pos), *_rope_tables(pos_s), w_pool_bf,
      pool_scale.reshape(1, width))


def _shift_rows(x, prev8, k):
    pad = V7X_SUBLANES
    rolled = pltpu.roll(x, k, 0)
    row8 = lax.broadcasted_iota(jnp.int32, prev8.shape, 0)
    head = jnp.where(row8 < k, pltpu.roll(prev8, k % pad, 0) if k % pad else prev8, rolled[0:pad])
    return jnp.concatenate([head, rolled[pad:]], axis=0)


def _pool_sample_kernel(u_ref, hist_ref, w_ref, scale_ref, o_ref, *, n_steps, n_seq, group, past_len):
    def ext(i):
        if i < POOL_HIST:
            return hist_ref[i * n_seq:(i + 1) * n_seq, :]
        return u_ref[(i - POOL_HIST) * n_seq:(i - POOL_HIST + 1) * n_seq, :]

    for gi, w in enumerate(POOL_WINDOWS):
        cols = slice(gi * group, (gi + 1) * group)
        ds = []
        for t in range(n_steps):
            cur = ext(POOL_HIST + t)[:, cols]
            acc = cur
            for back in range(1, w):
                acc = acc + ext(POOL_HIST + t - back)[:, cols]
            cnt = float(min(past_len + t + 1, w))
            ds.append((acc / cnt - cur).astype(BF16))
        d = jnp.concatenate(ds, axis=0)
        y = _dot(d, w_ref[gi]) * scale_ref[:, cols]
        o_ref[:, cols] = y.astype(BF16)


def _pool_sample(u_tm, hist_tm, w_pool_bf, scale, n_steps, n_seq, past_len):
    rows, C = u_tm.shape
    n_groups, group, _ = w_pool_bf.shape
    block_bytes = 2 * (rows + hist_tm.shape[0]) * C * 4 + 2 * n_groups * group * group * 2 + 2 * rows * C * 2
    return pl.pallas_call(
        functools.partial(_pool_sample_kernel, n_steps=n_steps, n_seq=n_seq, group=group,
                          past_len=past_len),
        out_shape=jax.ShapeDtypeStruct((rows, C), BF16),
        compiler_params=pltpu.CompilerParams(vmem_limit_bytes=_vmem_limit(block_bytes)),
        name="pool_sample",
    )(u_tm, hist_tm, w_pool_bf, scale.reshape(1, C))


class _PerHead:
    def __init__(self, refs):
        self.refs = refs

    def __getitem__(self, idx):
        return self.refs[idx[0]][idx[1:]] if isinstance(idx, tuple) else self.refs[idx][...]

    def __setitem__(self, idx, value):
        self.refs[idx[0]][idx[1:]] = value


def _moba_prompt_kernel(q_ref, k_ref, v_ref, o_ref, *scratch, seq, hp):
    blk = MOBA_BLOCK
    nb = seq // blk
    n_sel = min(MOBA_TOPK, nb - 1)
    scale = HEAD_DIM ** -0.5
    kb_ref, vt_ref, km_ref, s_ref, p_ref = (_PerHead(scratch[kind * hp:(kind + 1) * hp]) for kind in range(5))

    for hh in range(hp):
        for n in range(nb):
            rows = slice(n * blk, (n + 1) * blk)
            kn = k_ref[0, hh, rows, :]
            kb_ref[hh, rows, :] = kn.astype(BF16)
            km_ref[hh, n:n + 1, :] = jnp.sum(kn, axis=0, keepdims=True) * (1.0 / blk)
            vt_ref[hh, :, rows] = v_ref[0, hh, rows, :].T.astype(BF16)

    blk_id = lax.broadcasted_iota(jnp.int32, (nb, blk), 0)
    key_i = lax.broadcasted_iota(jnp.int32, (blk, blk), 0)
    qry_i = lax.broadcasted_iota(jnp.int32, (blk, blk), 1)
    causal_bias = jnp.where(key_i <= qry_i, 0.0, NEG_INF).astype(F32)

    def scores(hh, qblocks):
        first, last = qblocks[0], qblocks[-1]
        nk = (last + 1) * blk
        width = len(qblocks) * blk
        qg = q_ref[0, hh, first * blk:nk, :]
        s_ref[hh, 0:nk, 0:width] = _dot_nt(kb_ref[hh, 0:nk, :], (qg * (scale * LOG2E)).astype(BF16))
        return _dot_nt(km_ref[hh], qg, precision=lax.Precision.HIGHEST) if last > 0 else None

    def softmax(hh, qblocks, gate):
        last = qblocks[-1]
        sums = []
        for a, i in enumerate(qblocks):
            cols = slice(a * blk, (a + 1) * blk)
            own = slice(i * blk, (i + 1) * blk)
            biases = []
            if i > 0:
                gate_a = gate[:, cols]
                valid = jnp.where(blk_id < i, 1.0, 0.0)
                for n in range(i):
                    gn = gate_a[n:n + 1, :]
                    beats = jnp.where(blk_id < n, jnp.where(gate_a >= gn, 1.0, 0.0),
                                      jnp.where(gate_a > gn, 1.0, 0.0))
                    rank = jnp.sum(beats * valid, axis=0, keepdims=True)
                    biases.append(jnp.where(rank < n_sel, 0.0, NEG_INF).astype(F32))

            s_own = s_ref[hh, own, cols] + causal_bias
            m = jnp.max(s_own, axis=0, keepdims=True)
            for n in range(i):
                m = jnp.maximum(m, jnp.max(s_ref[hh, n * blk:(n + 1) * blk, cols], axis=0, keepdims=True)
                                + biases[n])

            p = jnp.exp2(s_own - m)
            l = jnp.sum(p, axis=0, keepdims=True)
            p_ref[hh, own, cols] = p.astype(BF16)
            for n in range(i):
                rows = slice(n * blk, (n + 1) * blk)
                p = jnp.exp2(s_ref[hh, rows, cols] + (biases[n] - m))
                l = l + jnp.sum(p, axis=0, keepdims=True)
                p_ref[hh, rows, cols] = p.astype(BF16)
            for n in range(i + 1, last + 1):
                p_ref[hh, n * blk:(n + 1) * blk, cols] = jnp.zeros((blk, blk), BF16)
            sums.append(l)
        return sums

    def output(hh, qblocks, sums):
        nk = (qblocks[-1] + 1) * blk
        width = len(qblocks) * blk
        o_t = _dot(vt_ref[hh, :, 0:nk], p_ref[hh, 0:nk, 0:width])
        for a, i in enumerate(qblocks):
            o_a = o_t[:, a * blk:(a + 1) * blk] / sums[a]
            o_ref[0, i * blk:(i + 1) * blk, hh * HEAD_DIM:(hh + 1) * HEAD_DIM] = o_a.T.astype(BF16)

    items = [(hh, list(range(first, min(first + QUERY_BLOCKS_PER_DOT, nb))))
             for first in range(0, nb, QUERY_BLOCKS_PER_DOT) for hh in range(hp)]
    gate = scores(*items[0])
    for cur, nxt in zip(items, items[1:] + [None]):
        next_gate = scores(*nxt) if nxt is not None else None
        output(*cur, softmax(*cur, gate))
        gate = next_gate


def _moba_prompt(q, k, v):
    B, H, S, Dh = q.shape
    nb = S // MOBA_BLOCK
    hp = 2 if H % 2 == 0 else 1
    width = QUERY_BLOCKS_PER_DOT * MOBA_BLOCK
    in_spec = pl.BlockSpec((1, hp, S, Dh), lambda b, g: (b, g, 0, 0))
    block_bytes = hp * (2 * 3 * S * Dh * 4 + 2 * S * Dh * 2 + 2 * S * Dh * 2 + S * width * 6)
    return pl.pallas_call(
        functools.partial(_moba_prompt_kernel, seq=S, hp=hp),
        grid=(B, H // hp),
        in_specs=[in_spec, in_spec, in_spec],
        out_specs=pl.BlockSpec((1, S, hp * Dh), lambda b, g: (b, 0, g)),
        out_shape=jax.ShapeDtypeStruct((B, S, H * Dh), BF16),
        scratch_shapes=(
            [pltpu.VMEM((S, Dh), BF16)] * hp
            + [pltpu.VMEM((Dh, S), BF16)] * hp
            + [pltpu.VMEM((nb, Dh), F32)] * hp
            + [pltpu.VMEM((S, width), F32)] * hp
            + [pltpu.VMEM((S, width), BF16)] * hp
        ),
        compiler_params=_params(("arbitrary", "arbitrary"), block_bytes),
        name="moba_prompt",
    )(q, k, v)


class _PagedAttention:
    def __init__(self, pt_ref, qc_ref, qp_ref, kn_ref, vn_ref, ck_ref, cv_ref, o_ref,
                 kbuf, ksel, vbuf, km_ref, q8_ref, sel_ref, ksem, vsem, *, n_pages, n_groups, hp, n_new):
        self.__dict__.update(locals())
        self.ppb = MOBA_BLOCK // PAGE_SIZE
        self.nblk = n_pages // self.ppb
        self.n_sel = min(MOBA_TOPK, self.nblk)

    def unit_bh(self, u):
        return u // self.n_groups, lax.rem(u, self.n_groups) * self.hp

    def k_copy(self, u, page):
        bb, h0 = self.unit_bh(u)
        sl = lax.rem(u, 2)
        return pltpu.make_async_copy(self.ck_ref.at[self.pt_ref[bb, page], pl.ds(h0, self.hp)],
                                     self.kbuf.at[sl, :, pl.ds(page * PAGE_SIZE, PAGE_SIZE), :],
                                     self.ksem.at[sl])

    def start_k(self, u):
        def body(page, carry):
            self.k_copy(u, page).start()
            return carry
        lax.fori_loop(0, self.n_pages, body, 0, unroll=8)

    def wait_k(self, u):
        for page in range(self.n_pages):
            self.k_copy(u, page).wait()

    def sel_index(self, hh, j, c):
        return (hh * self.n_new + j) * self.n_sel + c

    def v_copies(self, u, heads):
        bb, h0 = self.unit_bh(u)
        sl = lax.rem(u, 2)
        for hh in heads:
            for j in range(self.n_new):
                for c in range(self.n_sel):
                    blk = self.sel_ref[sl, self.sel_index(hh, j, c)]
                    for pg in range(self.ppb):
                        page = self.pt_ref[bb, blk * self.ppb + pg]
                        dst = ((j * self.n_sel + c) * self.ppb + pg) * PAGE_SIZE
                        yield pltpu.make_async_copy(self.cv_ref.at[page, h0 + hh],
                                                    self.vbuf.at[sl, hh, pl.ds(dst, PAGE_SIZE), :],
                                                    self.vsem.at[sl])

    def start_v(self, u, heads):
        for cp in self.v_copies(u, heads):
            cp.start()

    def wait_v(self, u):
        for cp in self.v_copies(u, range(self.hp)):
            cp.wait()

    def block_means(self, u, heads):
        sl = lax.rem(u, 2)
        for hh in heads:
            for n in range(self.nblk):
                kn_blk = self.kbuf[sl, hh, n * MOBA_BLOCK:(n + 1) * MOBA_BLOCK, :]
                self.km_ref[hh, n:n + 1, :] = jnp.sum(kn_blk, axis=0, keepdims=True) * (1.0 / MOBA_BLOCK)

    def select(self, u, heads):
        sl = lax.rem(u, 2)
        for hh in heads:
            self.q8_ref[hh] = jnp.zeros(self.q8_ref.shape[1:], F32)
            self.q8_ref[hh, 0:self.n_new, :] = self.qc_ref[0, hh]
            gate = _dot_nt(self.q8_ref[hh], self.km_ref[hh], precision=lax.Precision.HIGHEST)
            lane = lax.broadcasted_iota(jnp.int32, gate.shape, 1)
            picks = [jnp.zeros((gate.shape[0], 1), F32) for _ in range(self.n_sel)]
            for n in range(self.nblk):
                gn = gate[:, n:n + 1]
                beats = jnp.where(lane < n, jnp.where(gate >= gn, 1.0, 0.0), jnp.where(gate > gn, 1.0, 0.0))
                rank = jnp.sum(beats, axis=1, keepdims=True)
                for c in range(self.n_sel):
                    picks[c] = jnp.where(rank == float(c), float(n), picks[c])
            for j in range(self.n_new):
                for c in range(self.n_sel):
                    blk = picks[c][j, 0].astype(jnp.int32)
                    self.sel_ref[sl, self.sel_index(hh, j, c)] = blk
                    row0 = (j * self.n_sel + c) * MOBA_BLOCK
                    start = pl.multiple_of(blk * MOBA_BLOCK, MOBA_BLOCK)
                    self.ksel[sl, hh, row0:row0 + MOBA_BLOCK, :] = self.kbuf[sl, hh, pl.ds(start, MOBA_BLOCK), :]

    def attend(self, u, heads):
        sl = lax.rem(u, 2)
        scale = HEAD_DIM ** -0.5
        new_row = lax.broadcasted_iota(jnp.int32, (self.n_new, 1), 0)
        for hh in heads:
            q = self.qp_ref[0, hh]
            kn = self.kn_ref[0, hh]
            vn = self.vn_ref[0, hh]
            for j in range(self.n_new):
                qj = q[j:j + 1, :] * (scale * LOG2E)
                rows = [slice((j * self.n_sel + c) * MOBA_BLOCK, (j * self.n_sel + c + 1) * MOBA_BLOCK)
                        for c in range(self.n_sel)]
                s_sel = [jnp.sum(self.ksel[sl, hh, r, :] * qj, axis=1, keepdims=True) for r in rows]
                s_new = jnp.sum(kn * qj, axis=1, keepdims=True)
                s_new = jnp.where(new_row <= j, s_new, NEG_INF)
                m = jnp.max(s_new, axis=0, keepdims=True)
                for s in s_sel:
                    m = jnp.maximum(m, jnp.max(s, axis=0, keepdims=True))
                p_new = jnp.exp2(s_new - m)
                l = jnp.sum(p_new, axis=0, keepdims=True)
                acc = jnp.sum(p_new * vn, axis=0, keepdims=True)
                for s, r in zip(s_sel, rows):
                    p = jnp.exp2(s - m)
                    l = l + jnp.sum(p, axis=0, keepdims=True)
                    acc = acc + jnp.sum(p * self.vbuf[sl, hh, r, :], axis=0, keepdims=True)
                self.o_ref[0, hh, j:j + 1, :] = acc / l


def _paged_attention_scratch(n_pages, hp, n_new):
    nblk = n_pages // (MOBA_BLOCK // PAGE_SIZE)
    n_sel = min(MOBA_TOPK, nblk)
    picked_rows = n_new * n_sel * MOBA_BLOCK
    shapes = [
        pltpu.VMEM((2, hp, n_pages * PAGE_SIZE, HEAD_DIM), F32),
        pltpu.VMEM((2, hp, picked_rows, HEAD_DIM), F32),
        pltpu.VMEM((2, hp, picked_rows, HEAD_DIM), F32),
        pltpu.VMEM((hp, nblk, HEAD_DIM), F32),
        pltpu.VMEM((hp, V7X_SUBLANES, HEAD_DIM), F32),
        pltpu.SMEM((2, hp * n_new * n_sel), jnp.int32),
        pltpu.SemaphoreType.DMA((2,)),
        pltpu.SemaphoreType.DMA((2,)),
    ]
    n_bytes = 4 * HEAD_DIM * 2 * hp * (n_pages * PAGE_SIZE + 2 * picked_rows)
    return shapes, n_bytes


def _outproj_kernel(pooled_ref, attn_ref, x_ref, w_ref, gpost_ref, gffn_ref, x1_ref, h2_ref, *, split, sub):
    for r in range(x_ref.shape[0] // sub):
        rows = slice(r * sub, (r + 1) * sub)
        mix = _dot(pooled_ref[rows, :], w_ref[0:split, :]) + _dot(attn_ref[rows, :], w_ref[split:, :])
        x1 = x_ref[rows, :] + _rms(mix, gpost_ref[...])
        x1_ref[rows, :] = x1
        h2_ref[rows, :] = _rms(x1, gffn_ref[...]).astype(BF16)


def _outproj(pooled, attn, x, w_bf, g_post, g_ffn, tm):
    rows, D = x.shape
    split = pooled.shape[1]
    wa = attn.shape[1]
    row_spec = lambda width: pl.BlockSpec((tm, width), lambda i: (i, 0))
    vec_spec = pl.BlockSpec((1, D), lambda i: (0, 0))
    block_bytes = 2 * tm * (split + wa) * 2 + 2 * tm * D * 4 + 2 * (split + wa) * D * 2 \
        + 2 * tm * D * 6 + 2 * tm * D * 4
    return pl.pallas_call(
        functools.partial(_outproj_kernel, split=split, sub=_pick_tile(tm, 128)),
        grid=(rows // tm,),
        in_specs=[row_spec(split), row_spec(wa), row_spec(D),
                  pl.BlockSpec((split + wa, D), lambda i: (0, 0)), vec_spec, vec_spec],
        out_specs=[row_spec(D), row_spec(D)],
        out_shape=[jax.ShapeDtypeStruct((rows, D), F32), jax.ShapeDtypeStruct((rows, D), BF16)],
        compiler_params=_params(("arbitrary",), block_bytes),
        name="outproj",
    )(pooled, attn, x, w_bf, g_post.reshape(1, D), g_ffn.reshape(1, D))


def _gelu_tanh(c):
    return c * (0.5 * (1.0 + jnp.tanh(0.7978845608028654 * (c + 0.044715 * (c * c * c)))))


def _conv_gelu_gate(gt, prev8, up, cw_ref, cb_ref):
    c = cb_ref[...]
    for i in range(CONV_W):
        back = CONV_W - 1 - i
        c = c + (_shift_rows(gt, prev8, back) if back else gt) * cw_ref[i:i + 1, :]
    return _gelu_tanh(c) * up


def _ffn_up_kernel(pt_ref, h_ref, wg_ref, wu_ref, cw_ref, cb_ref, qc_ref, qp_ref, kn_ref, vn_ref, ck_ref, cv_ref,
                   f_ref, tail_ref, o_ref, wgb_ref, wub_ref, halo_ref, *attn_scratch,
                   tm, sub, n_b, n_s, n_units, attn_params):
    j, b, s = pl.program_id(0), pl.program_id(1), pl.program_id(2)
    t = (j * n_b + b) * n_s + s
    pad = V7X_SUBLANES
    last_sub = tm // sub - 1
    attn = _PagedAttention(pt_ref, qc_ref, qp_ref, kn_ref, vn_ref, ck_ref, cv_ref, o_ref,
                           *attn_scratch, **attn_params)

    @pl.when((b == 0) & (s == 0))
    def _():
        wgb_ref[...] = wg_ref[...].astype(BF16)
        wub_ref[...] = wu_ref[...].astype(BF16)

    @pl.when(s == 0)
    def _():
        halo_ref[...] = jnp.zeros(halo_ref.shape, F32)

    @pl.when(t == 0)
    def _():
        attn.start_k(t)

    @pl.when(t < n_units)
    def _():
        attn.wait_k(t)

    @pl.when(t + 1 < n_units)
    def _():
        attn.start_k(t + 1)

    def gate_up(before=None, middle=None, end=()):
        before, middle = before or {}, middle or {}
        prev8 = halo_ref[...]
        half = sub // 2
        for r in range(tm // sub):
            for stage in before.get(r, ()):
                stage()
            row0 = r * sub
            if r in middle:
                lo = h_ref[0, row0:row0 + half, :]
                gt_lo, up_lo = _dot(lo, wgb_ref[...]), _dot(lo, wub_ref[...])
                for stage in middle[r]:
                    stage()
                hi = h_ref[0, row0 + half:row0 + sub, :]
                gt = jnp.concatenate([gt_lo, _dot(hi, wgb_ref[...])], axis=0)
                up = jnp.concatenate([up_lo, _dot(hi, wub_ref[...])], axis=0)
            else:
                hr = h_ref[0, row0:row0 + sub, :]
                gt, up = _dot(hr, wgb_ref[...]), _dot(hr, wub_ref[...])
            f_ref[0, row0:row0 + sub, :] = _conv_gelu_gate(gt, prev8, up, cw_ref, cb_ref).astype(BF16)
            prev8 = gt[sub - pad:sub, :]
        halo_ref[...] = prev8
        tail_ref[0] = prev8
        for stage in end:
            stage()

    heads = tuple(range(attn.hp))

    def means():
        attn.block_means(t, heads)

    def select():
        attn.select(t, heads)

    def attend_previous():
        attn.wait_v(t - 1)
        attn.attend(t - 1, heads)

    def fetch_values():
        attn.start_v(t, heads)

    @pl.when(t == 0)
    def _():
        gate_up(before={0: [means]}, middle={0: [select]}, end=[fetch_values])

    @pl.when((t >= 1) & (t < n_units))
    def _():
        before = {0: [means]}
        before.setdefault(last_sub, []).append(attend_previous)
        gate_up(before=before, middle={0: [select]}, end=[fetch_values])

    @pl.when(t == n_units)
    def _():
        gate_up(before={last_sub: [attend_previous]})

    @pl.when(t > n_units)
    def _():
        gate_up()


def _ffn_up(h2, w_gate, w_up, conv_w, conv_b, q, k_new, v_new, cache_k, cache_v, page_table, tm, tn, sub):
    B, S, D = h2.shape
    F = w_gate.shape[1]
    Bd, H, L, Dh = q.shape
    n_pages = page_table.shape[1]
    ppb = MOBA_BLOCK // PAGE_SIZE
    assert n_pages % ppb == 0, "past length must be a whole number of MoBA blocks"
    assert n_pages >= ppb and L <= V7X_SUBLANES
    hp = 2 if H % 2 == 0 else 1
    n_groups = H // hp
    n_units = Bd * n_groups
    n_b, n_s = B, S // tm
    assert (F // tn) * n_b * n_s > n_units >= 2, "not enough grid steps to host the sample attention"
    pad = V7X_SUBLANES

    def unit_spec(lag):
        def index_map(j, b, s, pt):
            u = jnp.clip((j * n_b + b) * n_s + s - lag, 0, n_units - 1)
            return (u // n_groups, lax.rem(u, n_groups), 0, 0)
        return pl.BlockSpec((1, hp, L, Dh), index_map)

    w_spec = pl.BlockSpec((D, tn), lambda j, b, s, pt: (0, j))
    any_spec = pl.BlockSpec(memory_space=pl.ANY)
    attn_scratch, attn_bytes = _paged_attention_scratch(n_pages, hp, L)
    block_bytes = (2 * tm * D * 2 + 2 * 2 * D * tn * 4 + 2 * D * tn * 2 + 2 * tm * tn * 2 + 8 * sub * tn * 4
                   + attn_bytes)
    grid_spec = pltpu.PrefetchScalarGridSpec(
        num_scalar_prefetch=1,
        grid=(F // tn, n_b, n_s),
        in_specs=[
            pl.BlockSpec((1, tm, D), lambda j, b, s, pt: (b, s, 0)),
            w_spec, w_spec,
            pl.BlockSpec((CONV_W, tn), lambda j, b, s, pt: (0, j)),
            pl.BlockSpec((1, tn), lambda j, b, s, pt: (0, j)),
            unit_spec(0), unit_spec(1), unit_spec(1), unit_spec(1), any_spec, any_spec,
        ],
        out_specs=[pl.BlockSpec((1, tm, tn), lambda j, b, s, pt: (b, s, j)),
                   pl.BlockSpec((1, pad, tn), lambda j, b, s, pt: (b, 0, j)),
                   unit_spec(1)],
        scratch_shapes=[pltpu.VMEM((D, tn), BF16), pltpu.VMEM((D, tn), BF16), pltpu.VMEM((pad, tn), F32)]
        + attn_scratch,
    )
    return pl.pallas_call(
        functools.partial(_ffn_up_kernel, tm=tm, sub=sub, n_b=n_b, n_s=n_s, n_units=n_units,
                          attn_params=dict(n_pages=n_pages, n_groups=n_groups, hp=hp, n_new=L)),
        grid_spec=grid_spec,
        out_shape=[jax.ShapeDtypeStruct((B, S, F), BF16), jax.ShapeDtypeStruct((B, pad, F), F32),
                   jax.ShapeDtypeStruct((Bd, H, L, Dh), F32)],
        compiler_params=_resident_params(("arbitrary", "arbitrary", "arbitrary"), block_bytes),
        name="ffn_up",
    )(page_table, h2, w_gate, w_up, conv_w, conv_b.reshape(1, F), q, q, k_new, v_new, cache_k, cache_v)


def _ffn_up_sample_kernel(h_ref, wg_ref, wu_ref, cw_ref, cb_ref, hist_ref, f_ref, gt_ref, *, n_steps, n_seq):
    h = h_ref[...]
    gt = _dot(h, wg_ref[...].astype(BF16))
    up = _dot(h, wu_ref[...].astype(BF16))
    gt_ref[...] = gt
    ext = [hist_ref[i * n_seq:(i + 1) * n_seq, :] for i in range(CONV_W - 1)]
    ext += [gt[t * n_seq:(t + 1) * n_seq, :] for t in range(n_steps)]
    for t in range(n_steps):
        c = cb_ref[...]
        for i in range(CONV_W):
            c = c + ext[t + i] * cw_ref[i:i + 1, :]
        rows = slice(t * n_seq, (t + 1) * n_seq)
        f_ref[rows, :] = (_gelu_tanh(c) * up[rows, :]).astype(BF16)


def _ffn_up_sample(h2, w_gate, w_up, conv_w, conv_b, hist_tm, n_steps, n_seq, tn):
    rows, D = h2.shape
    F = w_gate.shape[1]
    col_spec = lambda r: pl.BlockSpec((r, tn), lambda j: (0, j))
    block_bytes = 2 * rows * D * 2 + 2 * 2 * D * tn * 4 + 2 * D * tn * 2 + 2 * rows * tn * 6 \
        + 2 * hist_tm.shape[0] * tn * 4 + 4 * rows * tn * 4
    return pl.pallas_call(
        functools.partial(_ffn_up_sample_kernel, n_steps=n_steps, n_seq=n_seq),
        grid=(F // tn,),
        in_specs=[pl.BlockSpec((rows, D), lambda j: (0, 0)), col_spec(D), col_spec(D),
                  col_spec(CONV_W), col_spec(1), col_spec(hist_tm.shape[0])],
        out_specs=[col_spec(rows), col_spec(rows)],
        out_shape=[jax.ShapeDtypeStruct((rows, F), BF16), jax.ShapeDtypeStruct((rows, F), F32)],
        compiler_params=_params(("arbitrary",), block_bytes),
        name="ffn_up_sample",
    )(h2, w_gate, w_up, conv_w, conv_b.reshape(1, F), hist_tm)


def _ffn_down_kernel(f_ref, fs_ref, w_hbm, x1_ref, x1s_ref, g_ref, y_ref, ys_ref, wb_ref, stage_ref, sem,
                     *, n_tiles, chunk):
    i = pl.program_id(0)

    @pl.when(i == 0)
    def _():
        _round_weight_once(w_hbm, wb_ref, stage_ref, sem, chunk)

    @pl.when(i < n_tiles)
    def _():
        y_ref[...] = x1_ref[...] + _rms(_dot(f_ref[...], wb_ref[...]), g_ref[...])

    @pl.when(i == n_tiles)
    def _():
        ys_ref[...] = x1s_ref[...] + _rms(_dot(fs_ref[...], wb_ref[...]), g_ref[...])


def _ffn_down(f, f_s, w_down, x1, x1_s, g, tm):
    rows, F = f.shape
    rows_s = f_s.shape[0]
    D = w_down.shape[1]
    n_tiles = rows // tm
    chunk = _pick_tile(F, 512)
    assert chunk % 16 == 0
    tile = lambda r: pl.BlockSpec((tm, r), lambda i: (jnp.minimum(i, n_tiles - 1), 0))
    whole = lambda r, c: pl.BlockSpec((r, c), lambda i: (0, 0))
    block_bytes = (2 * tm * F * 2 + 2 * rows_s * F * 2 + F * D * 2 + 2 * chunk * D * 4 + 4 * tm * D * 4
                   + 4 * rows_s * D * 4 + 2 * tm * D * 4)
    return pl.pallas_call(
        functools.partial(_ffn_down_kernel, n_tiles=n_tiles, chunk=chunk),
        grid=(n_tiles + 1,),
        in_specs=[tile(F), whole(rows_s, F), pl.BlockSpec(memory_space=pl.ANY), tile(D), whole(rows_s, D),
                  whole(1, D)],
        out_specs=[tile(D), whole(rows_s, D)],
        out_shape=[jax.ShapeDtypeStruct((rows, D), F32), jax.ShapeDtypeStruct((rows_s, D), F32)],
        scratch_shapes=[pltpu.VMEM((F, D), BF16), pltpu.VMEM((2, chunk, D), F32), pltpu.SemaphoreType.DMA((2,))],
        compiler_params=_resident_params(("arbitrary",), block_bytes),
        name="ffn_down",
    )(f, f_s, w_down, x1, x1_s, g.reshape(1, D))


def _pick_tile(n, target):
    t = min(n, target)
    while n % t:
        t //= 2
    return t


def _mixer_inputs(x, x_tm, pool_hist, page_table, lp, n_seq, n_steps):
    B, S, D = x.shape
    tm = _pick_tile(S, 512)
    past_len = page_table.shape[1] * PAGE_SIZE
    pos_s = past_len + jnp.repeat(jnp.arange(n_steps), n_seq)
    u, q, k, v, pooled, u_s, q_s, k_s, v_s = _inproj(x, x_tm[None], lp["g_mix_pre"], lp["w_in"], lp["w_pool"],
                                                     lp["pool_scale"], jnp.arange(S), pos_s, tm)
    attn = _moba_prompt(q, k, v)
    x1, h2 = _outproj(pooled.reshape(B * S, -1), attn.reshape(B * S, -1), x.reshape(B * S, D),
                      lp["w_out"], lp["g_mix_post"], lp["g_ffn_pre"], tm)

    def to_seq_major(t):
        H = t.shape[0]
        return t.reshape(H, n_steps, n_seq, HEAD_DIM).transpose(2, 0, 1, 3)

    pooled_s = _pool_sample(u_s[0], _seq_to_rows(pool_hist), lp["w_pool"], lp["pool_scale"],
                            n_steps, n_seq, past_len)
    sample = (to_seq_major(q_s[0]), to_seq_major(k_s[0]), to_seq_major(v_s[0]), pooled_s, u_s[0])
    return (x1, h2, k, v, u), sample


def _rows_to_seq(t, n_steps, n_seq):
    return t.reshape(n_steps, n_seq, t.shape[-1]).transpose(1, 0, 2)


def _seq_to_rows(t):
    return t.transpose(1, 0, 2).reshape(-1, t.shape[-1])


def _layer(xp, xs_tm, pool_hist, conv_hist, cache_k, cache_v, page_table, lp, n_seq, n_steps):
    B, S, D = xp.shape
    F = lp["w_gate"].shape[1]
    rows_s = xs_tm.shape[0]
    (x1p, h2p, kp, vp, up), (qs, ks, vs, pooled_s, us) = _mixer_inputs(xp, xs_tm, pool_hist, page_table, lp,
                                                                       n_seq, n_steps)
    tm = _pick_tile(S, 512)
    tn = _pick_tile(F, 512)
    fp, tail, attn_s = _ffn_up(h2p.reshape(B, S, D), lp["w_gate"], lp["w_up"], lp["conv_w"], lp["conv_b"],
                               qs, ks, vs, cache_k, cache_v, page_table, tm, tn, _pick_tile(tm, 256))
    attn_tm = attn_s.transpose(2, 0, 1, 3).reshape(rows_s, -1).astype(BF16)
    x1s, h2s = _outproj(pooled_s, attn_tm, xs_tm, lp["w_out"], lp["g_mix_post"], lp["g_ffn_pre"], rows_s)
    fs, gts = _ffn_up_sample(h2s, lp["w_gate"], lp["w_up"], lp["conv_w"], lp["conv_b"],
                             _seq_to_rows(conv_hist), n_steps, n_seq, tn)
    yp, ys = _ffn_down(fp.reshape(B * S, F), fs, lp["w_down"], x1p, x1s, lp["g_ffn_post"],
                       _pick_tile(B * S, 256))
    pool_p = up[:, S - POOL_HIST:, :]
    conv_p = tail[:, V7X_SUBLANES - (CONV_W - 1):, :]
    pool_s = jnp.concatenate([pool_hist, _rows_to_seq(us, n_steps, n_seq)], axis=1)[:, n_steps:]
    conv_s = jnp.concatenate([conv_hist, _rows_to_seq(gts, n_steps, n_seq)], axis=1)[:, n_steps:]
    return yp.reshape(B, S, D), ys, (kp, vp, ks, vs, pool_p, pool_s, conv_p, conv_s)


def kernel(x_prompt, x_sample, cache_k, cache_v, state_pool, state_conv, page_table,
           w_in, w_pool, pool_scale, w_out, g_mix_pre, g_mix_post,
           w_gate, w_up, conv_w, conv_b, w_down, g_ffn_pre, g_ffn_post):
    depth = w_in.shape[0]
    n_seq, n_steps, D = x_sample.shape
    assert w_in.shape[2] == 4 * pool_scale.shape[1], "pooling and attention widths must match"
    yp = x_prompt
    ys = x_sample.transpose(1, 0, 2).reshape(n_steps * n_seq, D)
    outs = [[] for _ in range(8)]
    for l in range(depth):
        lp = {"w_in": w_in[l], "w_pool": w_pool[l].astype(BF16), "pool_scale": pool_scale[l],
              "w_out": w_out[l].astype(BF16), "g_mix_pre": g_mix_pre[l], "g_mix_post": g_mix_post[l],
              "w_gate": w_gate[l], "w_up": w_up[l], "conv_w": conv_w[l],
              "conv_b": conv_b[l], "w_down": w_down[l], "g_ffn_pre": g_ffn_pre[l],
              "g_ffn_post": g_ffn_post[l]}
        yp, ys, states = _layer(yp, ys, state_pool[l], state_conv[l], cache_k[l], cache_v[l],
                                page_table, lp, n_seq, n_steps)
        for lst, val in zip(outs, states):
            lst.append(val)
    y_sample = ys.reshape(n_steps, n_seq, D).transpose(1, 0, 2)
    return (yp, y_sample) + tuple(jnp.stack(o) for o in outs)
```

```python
import functools

import jax
import jax.numpy as jnp
from jax import lax
from jax.experimental import pallas as pl
from jax.experimental.pallas import tpu as pltpu

F32 = jnp.float32
BF16 = jnp.bfloat16

POOL_WINDOWS = (2, 4, 8, 16)
POOL_HIST = max(POOL_WINDOWS) - 1
HEAD_DIM = 128
ROT_DIM = HEAD_DIM // 4
ROPE_THETA = 500000.0
MOBA_BLOCK = 256
MOBA_TOPK = 3
QUERY_BLOCKS_PER_DOT = 1
PAGE_SIZE = 128
CONV_W = 3
EPS = 1e-6
NEG_INF = float("-inf")
LOG2E = 1.4426950408889634

V7X_SUBLANES = 8
V7X_VMEM_BYTES = 64 * 1024 * 1024
VMEM_RESERVE_BYTES = 4 * 1024 * 1024


def _vmem_limit(block_bytes):
    return int(min(2 * block_bytes, V7X_VMEM_BYTES - VMEM_RESERVE_BYTES))


def _params(semantics, block_bytes):
    return pltpu.CompilerParams(dimension_semantics=semantics,
                                vmem_limit_bytes=_vmem_limit(block_bytes))


def _resident_params(semantics, block_bytes):
    limit = int(min(block_bytes + VMEM_RESERVE_BYTES, V7X_VMEM_BYTES - VMEM_RESERVE_BYTES))
    return pltpu.CompilerParams(dimension_semantics=semantics, vmem_limit_bytes=limit)


def _rms(x, g):
    return x * lax.rsqrt(jnp.mean(x * x, axis=-1, keepdims=True) + EPS) * g


def _dot(a, b):
    return jnp.dot(a, b, preferred_element_type=F32)


def _dot_nt(a, b, precision=None):
    return lax.dot_general(a, b, (((1,), (1,)), ((), ())), precision=precision,
                           preferred_element_type=F32)


def _rope_tables(pos):
    half = ROT_DIM // 2
    inv = ROPE_THETA ** (-jnp.arange(half, dtype=F32) * (2.0 / ROT_DIM))
    ang = pos.astype(F32)[:, None] * inv[None, :]
    cos, sin = jnp.cos(ang), jnp.sin(ang)
    n = pos.shape[0]
    c = jnp.concatenate([cos, cos, jnp.ones((n, HEAD_DIM - ROT_DIM), F32)], axis=1)
    s_lo = jnp.concatenate([-sin, jnp.zeros((n, HEAD_DIM - half), F32)], axis=1)
    s_hi = jnp.concatenate([jnp.zeros((n, half), F32), sin,
                            jnp.zeros((n, HEAD_DIM - ROT_DIM), F32)], axis=1)
    return c, s_lo, s_hi


def _round_weight_once(w_hbm, wb_ref, stage_ref, sem, chunk):
    n_chunks = wb_ref.shape[0] // chunk

    def w_copy(c):
        return pltpu.make_async_copy(w_hbm.at[pl.ds(c * chunk, chunk), :], stage_ref.at[c % 2], sem.at[c % 2])

    w_copy(0).start()
    for c in range(n_chunks):
        if c + 1 < n_chunks:
            w_copy(c + 1).start()
        w_copy(c).wait()
        wb_ref[c * chunk:(c + 1) * chunk, :] = stage_ref[c % 2].astype(BF16)


def _inproj_kernel(x_ref, xs_ref, g_ref, w_hbm, c_ref, slo_ref, shi_ref, cs_ref, slos_ref, shis_ref,
                   wp_ref, scale_ref,
                   u_ref, q_ref, k_ref, v_ref, pooled_ref, us_ref, qs_ref, ks_ref, vs_ref,
                   wb_ref, hn_ref, stage_ref, sem, halo_ref, *, n_heads, width, n_tiles, n_s, chunk):
    i = pl.program_id(0)
    pair = 2 * HEAD_DIM
    pad = V7X_SUBLANES
    seq_tile = lax.rem(i, n_s)

    @pl.when(i == 0)
    def _():
        _round_weight_once(w_hbm, wb_ref, stage_ref, sem, chunk)

    @pl.when((i < n_tiles) & (seq_tile == 0))
    def _():
        halo_ref[...] = jnp.zeros(halo_ref.shape, F32)

    def pool(p, x):
        rows = x.shape[0]
        w = POOL_WINDOWS[p]
        slot = sum(v.bit_length() - 1 for v in POOL_WINDOWS[:p])
        acc, span = x, 1
        while span < w:
            prev8 = halo_ref[slot]
            halo_ref[slot] = acc[rows - pad:rows, :]
            acc = acc + _shift_rows(acc, prev8, span)
            slot += 1
            span *= 2
        pos = seq_tile * rows + lax.broadcasted_iota(jnp.int32, (rows, 1), 0)
        return (acc / jnp.minimum(pos + 1, w).astype(F32) - x).astype(BF16)

    def project(x, tables, outs, pooled_out=None):
        c, s_lo, s_hi = (t[...] for t in tables)
        u_out, q_out, k_out, v_out = outs
        rows = x.shape[0]
        hn_ref[0:rows, :] = _rms(x, g_ref[...]).astype(BF16)

        def rope(h):
            return (h * c + pltpu.roll(h, HEAD_DIM - ROT_DIM // 2, 1) * s_lo
                    + pltpu.roll(h, ROT_DIM // 2, 1) * s_hi)

        deltas = []
        for p in range(width // pair):
            cols = slice(p * pair, (p + 1) * pair)
            res = _dot(hn_ref[0:rows, :], wb_ref[:, cols])
            u_out[0, :, cols] = res
            if pooled_out is not None:
                deltas.append(pool(p, res))
        for out_ref, part, rotary in ((q_out, 1, True), (k_out, 2, True), (v_out, 3, False)):
            for p in range(n_heads // 2):
                col = part * width + p * pair
                res = _dot(hn_ref[0:rows, :], wb_ref[:, col:col + pair])
                for half in range(2):
                    h = res[:, half * HEAD_DIM:(half + 1) * HEAD_DIM]
                    out_ref[0, 2 * p + half] = rope(h) if rotary else h
        for p, d in enumerate(deltas):
            cols = slice(p * pair, (p + 1) * pair)
            pooled_out[0, :, cols] = (_dot(d, wp_ref[p]) * scale_ref[:, cols]).astype(BF16)

    @pl.when(i < n_tiles)
    def _():
        project(x_ref[0], (c_ref, slo_ref, shi_ref), (u_ref, q_ref, k_ref, v_ref), pooled_ref)

    @pl.when(i == n_tiles)
    def _():
        project(xs_ref[0], (cs_ref, slos_ref, shis_ref), (us_ref, qs_ref, ks_ref, vs_ref))


def _inproj(x, x_s, g, w_in, w_pool_bf, pool_scale, pos, pos_s, tm):
    B, S, D = x.shape
    rows_s = x_s.shape[1]
    width = w_in.shape[1] // 4
    n_heads = width // HEAD_DIM
    n_groups, group, _ = w_pool_bf.shape
    assert n_heads % 2 == 0 and rows_s <= tm
    assert group == 2 * HEAD_DIM and n_groups == len(POOL_WINDOWS) == width // group
    assert all(w & (w - 1) == 0 and w <= 2 * V7X_SUBLANES for w in POOL_WINDOWS)
    n_levels = sum(w.bit_length() - 1 for w in POOL_WINDOWS)
    n_s = S // tm
    n_tiles = B * n_s
    chunk = _pick_tile(D, 128)

    def tile(i):
        ii = jnp.minimum(i, n_tiles - 1)
        return ii // n_s, lax.rem(ii, n_s)

    tab_spec = pl.BlockSpec((tm, HEAD_DIM), lambda i: (tile(i)[1], 0))
    hm_spec = pl.BlockSpec((1, n_heads, tm, HEAD_DIM), lambda i: (tile(i)[0], 0, tile(i)[1], 0))
    hm_shape = jax.ShapeDtypeStruct((B, n_heads, S, HEAD_DIM), F32)
    whole = lambda shape: pl.BlockSpec(shape, lambda i: (0,) * len(shape))
    stab_spec = whole((rows_s, HEAD_DIM))
    shm_spec = whole((1, n_heads, rows_s, HEAD_DIM))
    shm_shape = jax.ShapeDtypeStruct((1, n_heads, rows_s, HEAD_DIM), F32)
    block_bytes = (2 * tm * D * 4 + tm * D * 2 + D * 4 * width * 2 + 2 * chunk * 4 * width * 4
                   + 2 * 4 * tm * width * 4 + 2 * tm * width * 2 + 2 * rows_s * (D + 4 * width) * 4
                   + 2 * n_groups * group * group * 2)
    row_spec = pl.BlockSpec((1, tm, width), lambda i: (tile(i)[0], tile(i)[1], 0))
    return pl.pallas_call(
        functools.partial(_inproj_kernel, n_heads=n_heads, width=width, n_tiles=n_tiles, n_s=n_s, chunk=chunk),
        grid=(n_tiles + 1,),
        in_specs=[
            pl.BlockSpec((1, tm, D), lambda i: (tile(i)[0], tile(i)[1], 0)),
            whole((1, rows_s, D)),
            whole((1, D)),
            pl.BlockSpec(memory_space=pl.ANY),
            tab_spec, tab_spec, tab_spec, stab_spec, stab_spec, stab_spec,
            whole((n_groups, group, group)), whole((1, width)),
        ],
        out_specs=[row_spec, hm_spec, hm_spec, hm_spec, row_spec,
                   whole((1, rows_s, width)), shm_spec, shm_spec, shm_spec],
        out_shape=[jax.ShapeDtypeStruct((B, S, width), F32), hm_shape, hm_shape, hm_shape,
                   jax.ShapeDtypeStruct((B, S, width), BF16),
                   jax.ShapeDtypeStruct((1, rows_s, width), F32), shm_shape, shm_shape, shm_shape],
        scratch_shapes=[pltpu.VMEM((D, 4 * width), BF16), pltpu.VMEM((tm, D), BF16),
                        pltpu.VMEM((2, chunk, 4 * width), F32), pltpu.SemaphoreType.DMA((2,)),
                        pltpu.VMEM((n_levels, V7X_SUBLANES, group), F32)],
        compiler_params=_resident_params(("arbitrary",), block_bytes),
        name="inproj",
    )(x, x_s, g.reshape(1, D), w_in, *_rope_tables(pos), *_rope_tables(pos_s), w_pool_bf,
      pool_scale.reshape(1, width))


def _shift_rows(x, prev8, k):
    pad = V7X_SUBLANES
    rolled = pltpu.roll(x, k, 0)
    row8 = lax.broadcasted_iota(jnp.int32, prev8.shape, 0)
    head = jnp.where(row8 < k, pltpu.roll(prev8, k % pad, 0) if k % pad else prev8, rolled[0:pad])
    return jnp.concatenate([head, rolled[pad:]], axis=0)


def _pool_sample_kernel(u_ref, hist_ref, w_ref, scale_ref, o_ref, *, n_steps, n_seq, group, past_len):
    def ext(i):
        if i < POOL_HIST:
            return hist_ref[i * n_seq:(i + 1) * n_seq, :]
        return u_ref[(i - POOL_HIST) * n_seq:(i - POOL_HIST + 1) * n_seq, :]

    for gi, w in enumerate(POOL_WINDOWS):
        cols = slice(gi * group, (gi + 1) * group)
        ds = []
        for t in range(n_steps):
            cur = ext(POOL_HIST + t)[:, cols]
            acc = cur
            for back in range(1, w):
                acc = acc + ext(POOL_HIST + t - back)[:, cols]
            cnt = float(min(past_len + t + 1, w))
            ds.append((acc / cnt - cur).astype(BF16))
        d = jnp.concatenate(ds, axis=0)
        y = _dot(d, w_ref[gi]) * scale_ref[:, cols]
        o_ref[:, cols] = y.astype(BF16)


def _pool_sample(u_tm, hist_tm, w_pool_bf, scale, n_steps, n_seq, past_len):
    rows, C = u_tm.shape
    n_groups, group, _ = w_pool_bf.shape
    block_bytes = 2 * (rows + hist_tm.shape[0]) * C * 4 + 2 * n_groups * group * group * 2 + 2 * rows * C * 2
    return pl.pallas_call(
        functools.partial(_pool_sample_kernel, n_steps=n_steps, n_seq=n_seq, group=group,
                          past_len=past_len),
        out_shape=jax.ShapeDtypeStruct((rows, C), BF16),
        compiler_params=pltpu.CompilerParams(vmem_limit_bytes=_vmem_limit(block_bytes)),
        name="pool_sample",
    )(u_tm, hist_tm, w_pool_bf, scale.reshape(1, C))


class _PerHead:
    def __init__(self, refs):
        self.refs = refs

    def __getitem__(self, idx):
        return self.refs[idx[0]][idx[1:]] if isinstance(idx, tuple) else self.refs[idx][...]

    def __setitem__(self, idx, value):
        self.refs[idx[0]][idx[1:]] = value


def _moba_prompt_kernel(q_ref, k_ref, v_ref, o_ref, *scratch, seq, hp):
    blk = MOBA_BLOCK
    nb = seq // blk
    n_sel = min(MOBA_TOPK, nb - 1)
    scale = HEAD_DIM ** -0.5
    kb_ref, vt_ref, km_ref, s_ref, p_ref = (_PerHead(scratch[kind * hp:(kind + 1) * hp]) for kind in range(5))

    for hh in range(hp):
        for n in range(nb):
            rows = slice(n * blk, (n + 1) * blk)
            kn = k_ref[0, hh, rows, :]
            kb_ref[hh, rows, :] = kn.astype(BF16)
            km_ref[hh, n:n + 1, :] = jnp.sum(kn, axis=0, keepdims=True) * (1.0 / blk)
            vt_ref[hh, :, rows] = v_ref[0, hh, rows, :].T.astype(BF16)

    blk_id = lax.broadcasted_iota(jnp.int32, (nb, blk), 0)
    key_i = lax.broadcasted_iota(jnp.int32, (blk, blk), 0)
    qry_i = lax.broadcasted_iota(jnp.int32, (blk, blk), 1)
    causal_bias = jnp.where(key_i <= qry_i, 0.0, NEG_INF).astype(F32)

    def scores(hh, qblocks):
        first, last = qblocks[0], qblocks[-1]
        nk = (last + 1) * blk
        width = len(qblocks) * blk
        qg = q_ref[0, hh, first * blk:nk, :]
        s_ref[hh, 0:nk, 0:width] = _dot_nt(kb_ref[hh, 0:nk, :], (qg * (scale * LOG2E)).astype(BF16))
        return _dot_nt(km_ref[hh], qg, precision=lax.Precision.HIGHEST) if last > 0 else None

    def softmax(hh, qblocks, gate):
        last = qblocks[-1]
        sums = []
        for a, i in enumerate(qblocks):
            cols = slice(a * blk, (a + 1) * blk)
            own = slice(i * blk, (i + 1) * blk)
            biases = []
            if i > 0:
                gate_a = gate[:, cols]
                valid = jnp.where(blk_id < i, 1.0, 0.0)
                for n in range(i):
                    gn = gate_a[n:n + 1, :]
                    beats = jnp.where(blk_id < n, jnp.where(gate_a >= gn, 1.0, 0.0),
                                      jnp.where(gate_a > gn, 1.0, 0.0))
                    rank = jnp.sum(beats * valid, axis=0, keepdims=True)
                    biases.append(jnp.where(rank < n_sel, 0.0, NEG_INF).astype(F32))

            s_own = s_ref[hh, own, cols] + causal_bias
            m = jnp.max(s_own, axis=0, keepdims=True)
            for n in range(i):
                m = jnp.maximum(m, jnp.max(s_ref[hh, n * blk:(n + 1) * blk, cols], axis=0, keepdims=True)
                                + biases[n])

            p = jnp.exp2(s_own - m)
            l = jnp.sum(p, axis=0, keepdims=True)
            p_ref[hh, own, cols] = p.astype(BF16)
            for n in range(i):
                rows = slice(n * blk, (n + 1) * blk)
                p = jnp.exp2(s_ref[hh, rows, cols] + (biases[n] - m))
                l = l + jnp.sum(p, axis=0, keepdims=True)
                p_ref[hh, rows, cols] = p.astype(BF16)
            for n in range(i + 1, last + 1):
                p_ref[hh, n * blk:(n + 1) * blk, cols] = jnp.zeros((blk, blk), BF16)
            sums.append(l)
        return sums

    def output(hh, qblocks, sums):
        nk = (qblocks[-1] + 1) * blk
        width = len(qblocks) * blk
        o_t = _dot(vt_ref[hh, :, 0:nk], p_ref[hh, 0:nk, 0:width])
        for a, i in enumerate(qblocks):
            o_a = o_t[:, a * blk:(a + 1) * blk] / sums[a]
            o_ref[0, i * blk:(i + 1) * blk, hh * HEAD_DIM:(hh + 1) * HEAD_DIM] = o_a.T.astype(BF16)

    items = [(hh, list(range(first, min(first + QUERY_BLOCKS_PER_DOT, nb))))
             for first in range(0, nb, QUERY_BLOCKS_PER_DOT) for hh in range(hp)]
    gate = scores(*items[0])
    for cur, nxt in zip(items, items[1:] + [None]):
        next_gate = scores(*nxt) if nxt is not None else None
        output(*cur, softmax(*cur, gate))
        gate = next_gate


def _moba_prompt(q, k, v):
    B, H, S, Dh = q.shape
    nb = S // MOBA_BLOCK
    hp = 2 if H % 2 == 0 else 1
    width = QUERY_BLOCKS_PER_DOT * MOBA_BLOCK
    in_spec = pl.BlockSpec((1, hp, S, Dh), lambda b, g: (b, g, 0, 0))
    block_bytes = hp * (2 * 3 * S * Dh * 4 + 2 * S * Dh * 2 + 2 * S * Dh * 2 + S * width * 6)
    return pl.pallas_call(
        functools.partial(_moba_prompt_kernel, seq=S, hp=hp),
        grid=(B, H // hp),
        in_specs=[in_spec, in_spec, in_spec],
        out_specs=pl.BlockSpec((1, S, hp * Dh), lambda b, g: (b, 0, g)),
        out_shape=jax.ShapeDtypeStruct((B, S, H * Dh), BF16),
        scratch_shapes=(
            [pltpu.VMEM((S, Dh), BF16)] * hp
            + [pltpu.VMEM((Dh, S), BF16)] * hp
            + [pltpu.VMEM((nb, Dh), F32)] * hp
            + [pltpu.VMEM((S, width), F32)] * hp
            + [pltpu.VMEM((S, width), BF16)] * hp
        ),
        compiler_params=_params(("arbitrary", "arbitrary"), block_bytes),
        name="moba_prompt",
    )(q, k, v)


class _PagedAttention:
    def __init__(self, pt_ref, qc_ref, qp_ref, kn_ref, vn_ref, ck_ref, cv_ref, o_ref,
                 kbuf, ksel, vbuf, km_ref, q8_ref, sel_ref, ksem, vsem, *, n_pages, n_groups, hp, n_new):
        self.__dict__.update(locals())
        self.ppb = MOBA_BLOCK // PAGE_SIZE
        self.nblk = n_pages // self.ppb
        self.n_sel = min(MOBA_TOPK, self.nblk)

    def unit_bh(self, u):
        return u // self.n_groups, lax.rem(u, self.n_groups) * self.hp

    def k_copy(self, u, page):
        bb, h0 = self.unit_bh(u)
        sl = lax.rem(u, 2)
        return pltpu.make_async_copy(self.ck_ref.at[self.pt_ref[bb, page], pl.ds(h0, self.hp)],
                                     self.kbuf.at[sl, :, pl.ds(page * PAGE_SIZE, PAGE_SIZE), :],
                                     self.ksem.at[sl])

    def start_k(self, u):
        def body(page, carry):
            self.k_copy(u, page).start()
            return carry
        lax.fori_loop(0, self.n_pages, body, 0, unroll=8)

    def wait_k(self, u):
        for page in range(self.n_pages):
            self.k_copy(u, page).wait()

    def sel_index(self, hh, j, c):
        return (hh * self.n_new + j) * self.n_sel + c

    def v_copies(self, u, heads):
        bb, h0 = self.unit_bh(u)
        sl = lax.rem(u, 2)
        for hh in heads:
            for j in range(self.n_new):
                for c in range(self.n_sel):
                    blk = self.sel_ref[sl, self.sel_index(hh, j, c)]
                    for pg in range(self.ppb):
                        page = self.pt_ref[bb, blk * self.ppb + pg]
                        dst = ((j * self.n_sel + c) * self.ppb + pg) * PAGE_SIZE
                        yield pltpu.make_async_copy(self.cv_ref.at[page, h0 + hh],
                                                    self.vbuf.at[sl, hh, pl.ds(dst, PAGE_SIZE), :],
                                                    self.vsem.at[sl])

    def start_v(self, u, heads):
        for cp in self.v_copies(u, heads):
            cp.start()

    def wait_v(self, u):
        for cp in self.v_copies(u, range(self.hp)):
            cp.wait()

    def block_means(self, u, heads):
        sl = lax.rem(u, 2)
        for hh in heads:
            for n in range(self.nblk):
                kn_blk = self.kbuf[sl, hh, n * MOBA_BLOCK:(n + 1) * MOBA_BLOCK, :]
                self.km_ref[hh, n:n + 1, :] = jnp.sum(kn_blk, axis=0, keepdims=True) * (1.0 / MOBA_BLOCK)

    def select(self, u, heads):
        sl = lax.rem(u, 2)
        for hh in heads:
            self.q8_ref[hh] = jnp.zeros(self.q8_ref.shape[1:], F32)
            self.q8_ref[hh, 0:self.n_new, :] = self.qc_ref[0, hh]
            gate = _dot_nt(self.q8_ref[hh], self.km_ref[hh], precision=lax.Precision.HIGHEST)
            lane = lax.broadcasted_iota(jnp.int32, gate.shape, 1)
            picks = [jnp.zeros((gate.shape[0], 1), F32) for _ in range(self.n_sel)]
            for n in range(self.nblk):
                gn = gate[:, n:n + 1]
                beats = jnp.where(lane < n, jnp.where(gate >= gn, 1.0, 0.0), jnp.where(gate > gn, 1.0, 0.0))
                rank = jnp.sum(beats, axis=1, keepdims=True)
                for c in range(self.n_sel):
                    picks[c] = jnp.where(rank == float(c), float(n), picks[c])
            for j in range(self.n_new):
                for c in range(self.n_sel):
                    blk = picks[c][j, 0].astype(jnp.int32)
                    self.sel_ref[sl, self.sel_index(hh, j, c)] = blk
                    row0 = (j * self.n_sel + c) * MOBA_BLOCK
                    start = pl.multiple_of(blk * MOBA_BLOCK, MOBA_BLOCK)
                    self.ksel[sl, hh, row0:row0 + MOBA_BLOCK, :] = self.kbuf[sl, hh, pl.ds(start, MOBA_BLOCK), :]

    def attend(self, u, heads):
        sl = lax.rem(u, 2)
        scale = HEAD_DIM ** -0.5
        new_row = lax.broadcasted_iota(jnp.int32, (self.n_new, 1), 0)
        for hh in heads:
            q = self.qp_ref[0, hh]
            kn = self.kn_ref[0, hh]
            vn = self.vn_ref[0, hh]
            for j in range(self.n_new):
                qj = q[j:j + 1, :] * (scale * LOG2E)
                rows = [slice((j * self.n_sel + c) * MOBA_BLOCK, (j * self.n_sel + c + 1) * MOBA_BLOCK)
                        for c in range(self.n_sel)]
                s_sel = [jnp.sum(self.ksel[sl, hh, r, :] * qj, axis=1, keepdims=True) for r in rows]
                s_new = jnp.sum(kn * qj, axis=1, keepdims=True)
                s_new = jnp.where(new_row <= j, s_new, NEG_INF)
                m = jnp.max(s_new, axis=0, keepdims=True)
                for s in s_sel:
                    m = jnp.maximum(m, jnp.max(s, axis=0, keepdims=True))
                p_new = jnp.exp2(s_new - m)
                l = jnp.sum(p_new, axis=0, keepdims=True)
                acc = jnp.sum(p_new * vn, axis=0, keepdims=True)
                for s, r in zip(s_sel, rows):
                    p = jnp.exp2(s - m)
                    l = l + jnp.sum(p, axis=0, keepdims=True)
                    acc = acc + jnp.sum(p * self.vbuf[sl, hh, r, :], axis=0, keepdims=True)
                self.o_ref[0, hh, j:j + 1, :] = acc / l


def _paged_attention_scratch(n_pages, hp, n_new):
    nblk = n_pages // (MOBA_BLOCK // PAGE_SIZE)
    n_sel = min(MOBA_TOPK, nblk)
    picked_rows = n_new * n_sel * MOBA_BLOCK
    shapes = [
        pltpu.VMEM((2, hp, n_pages * PAGE_SIZE, HEAD_DIM), F32),
        pltpu.VMEM((2, hp, picked_rows, HEAD_DIM), F32),
        pltpu.VMEM((2, hp, picked_rows, HEAD_DIM), F32),
        pltpu.VMEM((hp, nblk, HEAD_DIM), F32),
        pltpu.VMEM((hp, V7X_SUBLANES, HEAD_DIM), F32),
        pltpu.SMEM((2, hp * n_new * n_sel), jnp.int32),
        pltpu.SemaphoreType.DMA((2,)),
        pltpu.SemaphoreType.DMA((2,)),
    ]
    n_bytes = 4 * HEAD_DIM * 2 * hp * (n_pages * PAGE_SIZE + 2 * picked_rows)
    return shapes, n_bytes


def _outproj_kernel(pooled_ref, attn_ref, x_ref, w_ref, gpost_ref, gffn_ref, x1_ref, h2_ref, *, split, sub):
    for r in range(x_ref.shape[0] // sub):
        rows = slice(r * sub, (r + 1) * sub)
        mix = _dot(pooled_ref[rows, :], w_ref[0:split, :]) + _dot(attn_ref[rows, :], w_ref[split:, :])
        x1 = x_ref[rows, :] + _rms(mix, gpost_ref[...])
        x1_ref[rows, :] = x1
        h2_ref[rows, :] = _rms(x1, gffn_ref[...]).astype(BF16)


def _outproj(pooled, attn, x, w_bf, g_post, g_ffn, tm):
    rows, D = x.shape
    split = pooled.shape[1]
    wa = attn.shape[1]
    row_spec = lambda width: pl.BlockSpec((tm, width), lambda i: (i, 0))
    vec_spec = pl.BlockSpec((1, D), lambda i: (0, 0))
    block_bytes = 2 * tm * (split + wa) * 2 + 2 * tm * D * 4 + 2 * (split + wa) * D * 2 \
        + 2 * tm * D * 6 + 2 * tm * D * 4
    return pl.pallas_call(
        functools.partial(_outproj_kernel, split=split, sub=_pick_tile(tm, 128)),
        grid=(rows // tm,),
        in_specs=[row_spec(split), row_spec(wa), row_spec(D),
                  pl.BlockSpec((split + wa, D), lambda i: (0, 0)), vec_spec, vec_spec],
        out_specs=[row_spec(D), row_spec(D)],
        out_shape=[jax.ShapeDtypeStruct((rows, D), F32), jax.ShapeDtypeStruct((rows, D), BF16)],
        compiler_params=_params(("arbitrary",), block_bytes),
        name="outproj",
    )(pooled, attn, x, w_bf, g_post.reshape(1, D), g_ffn.reshape(1, D))


def _gelu_tanh(c):
    return c * (0.5 * (1.0 + jnp.tanh(0.7978845608028654 * (c + 0.044715 * (c * c * c)))))


def _conv_gelu_gate(gt, prev8, up, cw_ref, cb_ref):
    c = cb_ref[...]
    for i in range(CONV_W):
        back = CONV_W - 1 - i
        c = c + (_shift_rows(gt, prev8, back) if back else gt) * cw_ref[i:i + 1, :]
    return _gelu_tanh(c) * up


def _ffn_up_kernel(pt_ref, h_ref, wg_ref, wu_ref, cw_ref, cb_ref, qc_ref, qp_ref, kn_ref, vn_ref, ck_ref, cv_ref,
                   f_ref, tail_ref, o_ref, wgb_ref, wub_ref, halo_ref, *attn_scratch,
                   tm, sub, n_b, n_s, n_units, attn_params):
    j, b, s = pl.program_id(0), pl.program_id(1), pl.program_id(2)
    t = (j * n_b + b) * n_s + s
    pad = V7X_SUBLANES
    last_sub = tm // sub - 1
    attn = _PagedAttention(pt_ref, qc_ref, qp_ref, kn_ref, vn_ref, ck_ref, cv_ref, o_ref,
                           *attn_scratch, **attn_params)

    @pl.when((b == 0) & (s == 0))
    def _():
        wgb_ref[...] = wg_ref[...].astype(BF16)
        wub_ref[...] = wu_ref[...].astype(BF16)

    @pl.when(s == 0)
    def _():
        halo_ref[...] = jnp.zeros(halo_ref.shape, F32)

    @pl.when(t == 0)
    def _():
        attn.start_k(t)

    @pl.when(t < n_units)
    def _():
        attn.wait_k(t)

    @pl.when(t + 1 < n_units)
    def _():
        attn.start_k(t + 1)

    def gate_up(before=None, middle=None, end=()):
        before, middle = before or {}, middle or {}
        prev8 = halo_ref[...]
        half = sub // 2
        for r in range(tm // sub):
            for stage in before.get(r, ()):
                stage()
            row0 = r * sub
            if r in middle:
                lo = h_ref[0, row0:row0 + half, :]
                gt_lo, up_lo = _dot(lo, wgb_ref[...]), _dot(lo, wub_ref[...])
                for stage in middle[r]:
                    stage()
                hi = h_ref[0, row0 + half:row0 + sub, :]
                gt = jnp.concatenate([gt_lo, _dot(hi, wgb_ref[...])], axis=0)
                up = jnp.concatenate([up_lo, _dot(hi, wub_ref[...])], axis=0)
            else:
                hr = h_ref[0, row0:row0 + sub, :]
                gt, up = _dot(hr, wgb_ref[...]), _dot(hr, wub_ref[...])
            f_ref[0, row0:row0 + sub, :] = _conv_gelu_gate(gt, prev8, up, cw_ref, cb_ref).astype(BF16)
            prev8 = gt[sub - pad:sub, :]
        halo_ref[...] = prev8
        tail_ref[0] = prev8
        for stage in end:
            stage()

    heads = tuple(range(attn.hp))

    def means():
        attn.block_means(t, heads)

    def select():
        attn.select(t, heads)

    def attend_previous():
        attn.wait_v(t - 1)
        attn.attend(t - 1, heads)

    def fetch_values():
        attn.start_v(t, heads)

    @pl.when(t == 0)
    def _():
        gate_up(before={0: [means]}, middle={0: [select]}, end=[fetch_values])

    @pl.when((t >= 1) & (t < n_units))
    def _():
        before = {0: [means]}
        before.setdefault(last_sub, []).append(attend_previous)
        gate_up(before=before, middle={0: [select]}, end=[fetch_values])

    @pl.when(t == n_units)
    def _():
        gate_up(before={last_sub: [attend_previous]})

    @pl.when(t > n_units)
    def _():
        gate_up()


def _ffn_up(h2, w_gate, w_up, conv_w, conv_b, q, k_new, v_new, cache_k, cache_v, page_table, tm, tn, sub):
    B, S, D = h2.shape
    F = w_gate.shape[1]
    Bd, H, L, Dh = q.shape
    n_pages = page_table.shape[1]
    ppb = MOBA_BLOCK // PAGE_SIZE
    assert n_pages % ppb == 0, "past length must be a whole number of MoBA blocks"
    assert n_pages >= ppb and L <= V7X_SUBLANES
    hp = 2 if H % 2 == 0 else 1
    n_groups = H // hp
    n_units = Bd * n_groups
    n_b, n_s = B, S // tm
    assert (F // tn) * n_b * n_s > n_units >= 2, "not enough grid steps to host the sample attention"
    pad = V7X_SUBLANES

    def unit_spec(lag):
        def index_map(j, b, s, pt):
            u = jnp.clip((j * n_b + b) * n_s + s - lag, 0, n_units - 1)
            return (u // n_groups, lax.rem(u, n_groups), 0, 0)
        return pl.BlockSpec((1, hp, L, Dh), index_map)

    w_spec = pl.BlockSpec((D, tn), lambda j, b, s, pt: (0, j))
    any_spec = pl.BlockSpec(memory_space=pl.ANY)
    attn_scratch, attn_bytes = _paged_attention_scratch(n_pages, hp, L)
    block_bytes = (2 * tm * D * 2 + 2 * 2 * D * tn * 4 + 2 * D * tn * 2 + 2 * tm * tn * 2 + 8 * sub * tn * 4
                   + attn_bytes)
    grid_spec = pltpu.PrefetchScalarGridSpec(
        num_scalar_prefetch=1,
        grid=(F // tn, n_b, n_s),
        in_specs=[
            pl.BlockSpec((1, tm, D), lambda j, b, s, pt: (b, s, 0)),
            w_spec, w_spec,
            pl.BlockSpec((CONV_W, tn), lambda j, b, s, pt: (0, j)),
            pl.BlockSpec((1, tn), lambda j, b, s, pt: (0, j)),
            unit_spec(0), unit_spec(1), unit_spec(1), unit_spec(1), any_spec, any_spec,
        ],
        out_specs=[pl.BlockSpec((1, tm, tn), lambda j, b, s, pt: (b, s, j)),
                   pl.BlockSpec((1, pad, tn), lambda j, b, s, pt: (b, 0, j)),
                   unit_spec(1)],
        scratch_shapes=[pltpu.VMEM((D, tn), BF16), pltpu.VMEM((D, tn), BF16), pltpu.VMEM((pad, tn), F32)]
        + attn_scratch,
    )
    return pl.pallas_call(
        functools.partial(_ffn_up_kernel, tm=tm, sub=sub, n_b=n_b, n_s=n_s, n_units=n_units,
                          attn_params=dict(n_pages=n_pages, n_groups=n_groups, hp=hp, n_new=L)),
        grid_spec=grid_spec,
        out_shape=[jax.ShapeDtypeStruct((B, S, F), BF16), jax.ShapeDtypeStruct((B, pad, F), F32),
                   jax.ShapeDtypeStruct((Bd, H, L, Dh), F32)],
        compiler_params=_resident_params(("arbitrary", "arbitrary", "arbitrary"), block_bytes),
        name="ffn_up",
    )(page_table, h2, w_gate, w_up, conv_w, conv_b.reshape(1, F), q, q, k_new, v_new, cache_k, cache_v)


def _ffn_up_sample_kernel(h_ref, wg_ref, wu_ref, cw_ref, cb_ref, hist_ref, f_ref, gt_ref, *, n_steps, n_seq):
    h = h_ref[...]
    gt = _dot(h, wg_ref[...].astype(BF16))
    up = _dot(h, wu_ref[...].astype(BF16))
    gt_ref[...] = gt
    ext = [hist_ref[i * n_seq:(i + 1) * n_seq, :] for i in range(CONV_W - 1)]
    ext += [gt[t * n_seq:(t + 1) * n_seq, :] for t in range(n_steps)]
    for t in range(n_steps):
        c = cb_ref[...]
        for i in range(CONV_W):
            c = c + ext[t + i] * cw_ref[i:i + 1, :]
        rows = slice(t * n_seq, (t + 1) * n_seq)
        f_ref[rows, :] = (_gelu_tanh(c) * up[rows, :]).astype(BF16)


def _ffn_up_sample(h2, w_gate, w_up, conv_w, conv_b, hist_tm, n_steps, n_seq, tn):
    rows, D = h2.shape
    F = w_gate.shape[1]
    col_spec = lambda r: pl.BlockSpec((r, tn), lambda j: (0, j))
    block_bytes = 2 * rows * D * 2 + 2 * 2 * D * tn * 4 + 2 * D * tn * 2 + 2 * rows * tn * 6 \
        + 2 * hist_tm.shape[0] * tn * 4 + 4 * rows * tn * 4
    return pl.pallas_call(
        functools.partial(_ffn_up_sample_kernel, n_steps=n_steps, n_seq=n_seq),
        grid=(F // tn,),
        in_specs=[pl.BlockSpec((rows, D), lambda j: (0, 0)), col_spec(D), col_spec(D),
                  col_spec(CONV_W), col_spec(1), col_spec(hist_tm.shape[0])],
        out_specs=[col_spec(rows), col_spec(rows)],
        out_shape=[jax.ShapeDtypeStruct((rows, F), BF16), jax.ShapeDtypeStruct((rows, F), F32)],
        compiler_params=_params(("arbitrary",), block_bytes),
        name="ffn_up_sample",
    )(h2, w_gate, w_up, conv_w, conv_b.reshape(1, F), hist_tm)


def _ffn_down_kernel(f_ref, fs_ref, w_hbm, x1_ref, x1s_ref, g_ref, y_ref, ys_ref, wb_ref, stage_ref, sem,
                     *, n_tiles, chunk):
    i = pl.program_id(0)

    @pl.when(i == 0)
    def _():
        _round_weight_once(w_hbm, wb_ref, stage_ref, sem, chunk)

    @pl.when(i < n_tiles)
    def _():
        y_ref[...] = x1_ref[...] + _rms(_dot(f_ref[...], wb_ref[...]), g_ref[...])

    @pl.when(i == n_tiles)
    def _():
        ys_ref[...] = x1s_ref[...] + _rms(_dot(fs_ref[...], wb_ref[...]), g_ref[...])


def _ffn_down(f, f_s, w_down, x1, x1_s, g, tm):
    rows, F = f.shape
    rows_s = f_s.shape[0]
    D = w_down.shape[1]
    n_tiles = rows // tm
    chunk = _pick_tile(F, 512)
    assert chunk % 16 == 0
    tile = lambda r: pl.BlockSpec((tm, r), lambda i: (jnp.minimum(i, n_tiles - 1), 0))
    whole = lambda r, c: pl.BlockSpec((r, c), lambda i: (0, 0))
    block_bytes = (2 * tm * F * 2 + 2 * rows_s * F * 2 + F * D * 2 + 2 * chunk * D * 4 + 4 * tm * D * 4
                   + 4 * rows_s * D * 4 + 2 * tm * D * 4)
    return pl.pallas_call(
        functools.partial(_ffn_down_kernel, n_tiles=n_tiles, chunk=chunk),
        grid=(n_tiles + 1,),
        in_specs=[tile(F), whole(rows_s, F), pl.BlockSpec(memory_space=pl.ANY), tile(D), whole(rows_s, D),
                  whole(1, D)],
        out_specs=[tile(D), whole(rows_s, D)],
        out_shape=[jax.ShapeDtypeStruct((rows, D), F32), jax.ShapeDtypeStruct((rows_s, D), F32)],
        scratch_shapes=[pltpu.VMEM((F, D), BF16), pltpu.VMEM((2, chunk, D), F32), pltpu.SemaphoreType.DMA((2,))],
        compiler_params=_resident_params(("arbitrary",), block_bytes),
        name="ffn_down",
    )(f, f_s, w_down, x1, x1_s, g.reshape(1, D))


def _pick_tile(n, target):
    t = min(n, target)
    while n % t:
        t //= 2
    return t


def _mixer_inputs(x, x_tm, pool_hist, page_table, lp, n_seq, n_steps):
    B, S, D = x.shape
    tm = _pick_tile(S, 512)
    past_len = page_table.shape[1] * PAGE_SIZE
    pos_s = past_len + jnp.repeat(jnp.arange(n_steps), n_seq)
    u, q, k, v, pooled, u_s, q_s, k_s, v_s = _inproj(x, x_tm[None], lp["g_mix_pre"], lp["w_in"], lp["w_pool"],
                                                     lp["pool_scale"], jnp.arange(S), pos_s, tm)
    attn = _moba_prompt(q, k, v)
    x1, h2 = _outproj(pooled.reshape(B * S, -1), attn.reshape(B * S, -1), x.reshape(B * S, D),
                      lp["w_out"], lp["g_mix_post"], lp["g_ffn_pre"], tm)

    def to_seq_major(t):
        H = t.shape[0]
        return t.reshape(H, n_steps, n_seq, HEAD_DIM).transpose(2, 0, 1, 3)

    pooled_s = _pool_sample(u_s[0], _seq_to_rows(pool_hist), lp["w_pool"], lp["pool_scale"],
                            n_steps, n_seq, past_len)
    sample = (to_seq_major(q_s[0]), to_seq_major(k_s[0]), to_seq_major(v_s[0]), pooled_s, u_s[0])
    return (x1, h2, k, v, u), sample


def _rows_to_seq(t, n_steps, n_seq):
    return t.reshape(n_steps, n_seq, t.shape[-1]).transpose(1, 0, 2)


def _seq_to_rows(t):
    return t.transpose(1, 0, 2).reshape(-1, t.shape[-1])


def _layer(xp, xs_tm, pool_hist, conv_hist, cache_k, cache_v, page_table, lp, n_seq, n_steps):
    B, S, D = xp.shape
    F = lp["w_gate"].shape[1]
    rows_s = xs_tm.shape[0]
    (x1p, h2p, kp, vp, up), (qs, ks, vs, pooled_s, us) = _mixer_inputs(xp, xs_tm, pool_hist, page_table, lp,
                                                                       n_seq, n_steps)
    tm = _pick_tile(S, 512)
    tn = _pick_tile(F, 512)
    fp, tail, attn_s = _ffn_up(h2p.reshape(B, S, D), lp["w_gate"], lp["w_up"], lp["conv_w"], lp["conv_b"],
                               qs, ks, vs, cache_k, cache_v, page_table, tm, tn, _pick_tile(tm, 256))
    attn_tm = attn_s.transpose(2, 0, 1, 3).reshape(rows_s, -1).astype(BF16)
    x1s, h2s = _outproj(pooled_s, attn_tm, xs_tm, lp["w_out"], lp["g_mix_post"], lp["g_ffn_pre"], rows_s)
    fs, gts = _ffn_up_sample(h2s, lp["w_gate"], lp["w_up"], lp["conv_w"], lp["conv_b"],
                             _seq_to_rows(conv_hist), n_steps, n_seq, tn)
    yp, ys = _ffn_down(fp.reshape(B * S, F), fs, lp["w_down"], x1p, x1s, lp["g_ffn_post"],
                       _pick_tile(B * S, 256))
    pool_p = up[:, S - POOL_HIST:, :]
    conv_p = tail[:, V7X_SUBLANES - (CONV_W - 1):, :]
    pool_s = jnp.concatenate([pool_hist, _rows_to_seq(us, n_steps, n_seq)], axis=1)[:, n_steps:]
    conv_s = jnp.concatenate([conv_hist, _rows_to_seq(gts, n_steps, n_seq)], axis=1)[:, n_steps:]
    return yp.reshape(B, S, D), ys, (kp, vp, ks, vs, pool_p, pool_s, conv_p, conv_s)


def kernel(x_prompt, x_sample, cache_k, cache_v, state_pool, state_conv, page_table,
           w_in, w_pool, pool_scale, w_out, g_mix_pre, g_mix_post,
           w_gate, w_up, conv_w, conv_b, w_down, g_ffn_pre, g_ffn_post):
    depth = w_in.shape[0]
    n_seq, n_steps, D = x_sample.shape
    assert w_in.shape[2] == 4 * pool_scale.shape[1], "pooling and attention widths must match"
    yp = x_prompt
    ys = x_sample.transpose(1, 0, 2).reshape(n_steps * n_seq, D)
    outs = [[] for _ in range(8)]
    for l in range(depth):
        lp = {"w_in": w_in[l], "w_pool": w_pool[l].astype(BF16), "pool_scale": pool_scale[l],
              "w_out": w_out[l].astype(BF16), "g_mix_pre": g_mix_pre[l], "g_mix_post": g_mix_post[l],
              "w_gate": w_gate[l], "w_up": w_up[l], "conv_w": conv_w[l],
              "conv_b": conv_b[l], "w_down": w_down[l], "g_ffn_pre": g_ffn_pre[l],
              "g_ffn_post": g_ffn_post[l]}
        yp, ys, states = _layer(yp, ys, state_pool[l], state_conv[l], cache_k[l], cache_v[l],
                                page_table, lp, n_seq, n_steps)
        for lst, val in zip(outs, states):
            lst.append(val)
    y_sample = ys.reshape(n_steps, n_seq, D).transpose(1, 0, 2)
    return (yp, y_sample) + tuple(jnp.stack(o) for o in outs)
```
